```python
import math
import jax, jax.numpy as jnp
from jax import lax
import numpy as np

D_MODEL = 1024
BATCH = 4
SEQ = 4096
DEPTH = 1
DEC_BATCH = 32
DEC_SEQ = 4
PAST_LEN = 8192
PAGE_SIZE = 128

HEAD_DIM = 64
MIX_WIDTH = D_MODEL
SB_WIDTH = MIX_WIDTH // 2
MOBA_WIDTH = MIX_WIDTH - SB_WIDTH
N_HEADS_SB = SB_WIDTH // HEAD_DIM
N_HEADS_MOBA = MOBA_WIDTH // HEAD_DIM
IN_WIDTH = 4 * SB_WIDTH + 4 * MOBA_WIDTH
SB_Q_BLOCK = 128
MOBA_BLOCK = 256
MOBA_TOPK = 3
MOBA_Q_BLOCK = 32
DEEPNORM_ALPHA = (2.0 * DEPTH) ** 0.25
DEEPNORM_BETA = (8.0 * DEPTH) ** -0.25
LN_EPS = 1e-5

kernel_name = "hybrid_stickbreak_moba_decoder_step"


def alibi_slopes(n_heads):
    return jnp.asarray([2.0 ** (-8.0 * (i + 1) / n_heads) for i in range(n_heads)], dtype=jnp.float32)


def layer_norm(x, g, b):
    xf = x.astype(jnp.float32)
    mu = jnp.mean(xf, axis=-1, keepdims=True)
    var = jnp.mean(jnp.square(xf - mu), axis=-1, keepdims=True)
    y = (xf - mu) * lax.rsqrt(var + LN_EPS) * g.astype(jnp.float32) + b.astype(jnp.float32)
    return y.astype(x.dtype)


def project(x, w_in):
    h = jnp.einsum('bsd,de->bse', x, w_in)
    offs = [SB_WIDTH, 2 * SB_WIDTH, 3 * SB_WIDTH, 4 * SB_WIDTH,
            4 * SB_WIDTH + MOBA_WIDTH, 4 * SB_WIDTH + 2 * MOBA_WIDTH, 4 * SB_WIDTH + 3 * MOBA_WIDTH]
    return jnp.split(h, offs, axis=-1)


def heads(x):
    return x.reshape(x.shape[:-1] + (x.shape[-1] // HEAD_DIM, HEAD_DIM))


def stick_breaking(q, k, v, q_pos):
    L = k.shape[1]
    z = jnp.einsum('bqhd,bkhd->bhqk', q, k).astype(jnp.float32) * (HEAD_DIM ** -0.5)
    strict = jnp.arange(L)[None, :] < q_pos[:, None]
    log_keep = jnp.where(strict, -jax.nn.softplus(z), 0.0)
    between = lax.cumsum(log_keep, axis=3, reverse=True) - log_keep
    w = jnp.where(strict, jnp.exp(jax.nn.log_sigmoid(z) + between), 0.0)
    return jnp.einsum('bhqk,bkhd->bqhd', w.astype(v.dtype), v)


def moba_blocks(k, v):
    B, L, H, D = k.shape
    nb = -(-L // MOBA_BLOCK)
    pad = nb * MOBA_BLOCK - L
    kb = jnp.pad(k, ((0, 0), (0, pad), (0, 0), (0, 0))).reshape(B, nb, MOBA_BLOCK, H, D).transpose(0, 3, 1, 2, 4)
    vb = jnp.pad(v, ((0, 0), (0, pad), (0, 0), (0, 0))).reshape(B, nb, MOBA_BLOCK, H, D).transpose(0, 3, 1, 2, 4)
    kmean = jnp.mean(kb.astype(jnp.float32), axis=3)
    return kb, vb, kmean


def moba_attend(q, q_pos, kb, vb, kmean, slopes):
    B, Tq, H, D = q.shape
    nb = kb.shape[2]
    own = q_pos // MOBA_BLOCK
    gate = jnp.einsum('bqhd,bhnd->bhqn', q.astype(jnp.float32), kmean)
    past_ok = jnp.arange(nb)[None, :] < own[:, None]
    gate = jnp.where(past_ok, gate, -jnp.inf)
    k_top = min(MOBA_TOPK, nb)
    _, top_idx = lax.top_k(gate, k_top)
    own_b = jnp.broadcast_to(own[None, None, :, None], (B, H, Tq, 1)).astype(top_idx.dtype)
    idx = jnp.concatenate([top_idx, own_b], axis=-1)
    is_own = jnp.arange(k_top + 1) == k_top
    bi = jnp.arange(B)[:, None, None, None]
    hi = jnp.arange(H)[None, :, None, None]
    k_sel = kb[bi, hi, idx]
    v_sel = vb[bi, hi, idx]
    pos = idx[..., None] * MOBA_BLOCK + jnp.arange(MOBA_BLOCK)
    t = q_pos[None, None, :, None, None]
    slot_ok = is_own | (idx < own[None, None, :, None])
    valid = (pos <= t) & slot_ok[..., None]
    s = jnp.einsum('bqhd,bhqnkd->bhqnk', q, k_sel).astype(jnp.float32) * (HEAD_DIM ** -0.5)
    s = s - slopes[None, :, None, None, None] * (t - pos).astype(jnp.float32)
    s = jnp.where(valid, s, -jnp.inf)
    p = jax.nn.softmax(s.reshape(B, H, Tq, -1), axis=-1).reshape(s.shape)
    return jnp.einsum('bhqnk,bhqnkd->bqhd', p.astype(v_sel.dtype), v_sel)


def blocked(fn, q, q_pos, block):
    B, T, H, D = q.shape
    n = T // block
    qb = q.reshape(B, n, block, H, D).swapaxes(0, 1)
    pb = q_pos.reshape(n, block)
    out = lax.map(lambda a: fn(a[0], a[1]), (qb, pb))
    return out.swapaxes(0, 1).reshape(B, T, H, D)


def gather_pages(cache, page_table):
    g = cache[page_table]
    return g.reshape((page_table.shape[0], -1) + cache.shape[2:])


def mixer_out(x, o_a, g_a, o_b, g_b, w_o, ln_g, ln_b):
    B, S, _ = x.shape
    h = jnp.concatenate([o_a.reshape(B, S, SB_WIDTH) * jax.nn.silu(g_a),
                         o_b.reshape(B, S, MOBA_WIDTH) * jax.nn.silu(g_b)], axis=-1)
    out = jnp.einsum('bse,ed->bsd', h, w_o)
    return layer_norm(DEEPNORM_ALPHA * x + out, ln_g, ln_b)


def setup_inputs(seed: int = 0) -> dict:
    key = jax.random.key(seed)
    ks = jax.random.split(key, 12)
    n_pages = PAST_LEN // PAGE_SIZE
    n_used = DEC_BATCH * n_pages
    n_pool = (n_used * 5) // 4
    cshape = (DEPTH, n_pool, PAGE_SIZE, N_HEADS_SB, HEAD_DIM)
    mshape = (DEPTH, n_pool, PAGE_SIZE, N_HEADS_MOBA, HEAD_DIM)
    page_table = jax.random.permutation(ks[6], n_pool)[:n_used].reshape(DEC_BATCH, n_pages).astype(jnp.int32)
    col_scale = np.ones((IN_WIDTH,), np.float32)
    col_scale[2 * SB_WIDTH:3 * SB_WIDTH] = DEEPNORM_BETA
    col_scale[4 * SB_WIDTH + 2 * MOBA_WIDTH:4 * SB_WIDTH + 3 * MOBA_WIDTH] = DEEPNORM_BETA
    w_in = jax.random.normal(ks[7], (DEPTH, D_MODEL, IN_WIDTH), jnp.float32) * (D_MODEL ** -0.5) * jnp.asarray(col_scale)
    w_out = jax.random.normal(ks[8], (DEPTH, MIX_WIDTH, D_MODEL), jnp.float32) * (MIX_WIDTH ** -0.5) * DEEPNORM_BETA
    return {
        "x_prompt": jax.random.normal(ks[0], (BATCH, SEQ, D_MODEL), jnp.float32),
        "x_sample": jax.random.normal(ks[1], (DEC_BATCH, DEC_SEQ, D_MODEL), jnp.float32),
        "cache_k_sb": jax.random.normal(ks[2], cshape, jnp.float32),
        "cache_v_sb": jax.random.normal(ks[3], cshape, jnp.float32),
        "cache_k_moba": jax.random.normal(ks[4], mshape, jnp.float32),
        "cache_v_moba": jax.random.normal(ks[5], mshape, jnp.float32),
        "page_table": page_table,
        "w_in": w_in,
        "w_out": w_out,
        "ln_gain": 1.0 + 0.02 * jax.random.normal(ks[9], (DEPTH, D_MODEL), jnp.float32),
        "ln_bias": 0.02 * jax.random.normal(ks[10], (DEPTH, D_MODEL), jnp.float32),
    }


def reference(x_prompt, x_sample, cache_k_sb, cache_v_sb, cache_k_moba, cache_v_moba, page_table,
              w_in, w_out, ln_gain, ln_bias):
    slopes = alibi_slopes(N_HEADS_MOBA)
    S = x_prompt.shape[1]
    T = x_sample.shape[1]
    past_len = page_table.shape[1] * cache_k_sb.shape[2]
    pos_p = jnp.arange(S, dtype=jnp.int32)
    pos_s = past_len + jnp.arange(T, dtype=jnp.int32)
    xp, xs = x_prompt, x_sample
    kap, vap, kbp, vbp, kas, vas, kbs, vbs = [], [], [], [], [], [], [], []
    for l in range(DEPTH):
        qa, ka, va, ga, qb, kb_, vb_, gb = project(xp, w_in[l])
        qa, ka, va, qb, kb_, vb_ = (heads(a) for a in (qa, ka, va, qb, kb_, vb_))
        o_a = blocked(lambda qq, pp: stick_breaking(qq, ka, va, pp), qa, pos_p, SB_Q_BLOCK)
        blk_p = moba_blocks(kb_, vb_)
        o_b = blocked(lambda qq, pp: moba_attend(qq, pp, blk_p[0], blk_p[1], blk_p[2], slopes), qb, pos_p, MOBA_Q_BLOCK)
        xp = mixer_out(xp, o_a, ga, o_b, gb, w_out[l], ln_gain[l], ln_bias[l])
        kap.append(ka); vap.append(va); kbp.append(kb_); vbp.append(vb_)
        sqa, ska, sva, sga, sqb, skb, svb, sgb = project(xs, w_in[l])
        sqa, ska, sva, sqb, skb, svb = (heads(a) for a in (sqa, ska, sva, sqb, skb, svb))
        ka_all = jnp.concatenate([gather_pages(cache_k_sb[l], page_table).astype(ska.dtype), ska], axis=1)
        va_all = jnp.concatenate([gather_pages(cache_v_sb[l], page_table).astype(sva.dtype), sva], axis=1)
        kb_all = jnp.concatenate([gather_pages(cache_k_moba[l], page_table).astype(skb.dtype), skb], axis=1)
        vb_all = jnp.concatenate([gather_pages(cache_v_moba[l], page_table).astype(svb.dtype), svb], axis=1)
        so_a = stick_breaking(sqa, ka_all, va_all, pos_s)
        blk_s = moba_blocks(kb_all, vb_all)
        so_b = moba_attend(sqb, pos_s, blk_s[0], blk_s[1], blk_s[2], slopes)
        xs = mixer_out(xs, so_a, sga, so_b, sgb, w_out[l], ln_gain[l], ln_bias[l])
        kas.append(ska); vas.append(sva); kbs.append(skb); vbs.append(svb)
    return (xp, xs,
            jnp.stack(kap), jnp.stack(vap), jnp.stack(kbp), jnp.stack(vbp),
            jnp.stack(kas), jnp.stack(vas), jnp.stack(kbs), jnp.stack(vbs))
```

```python
import functools

import jax
import jax.numpy as jnp
from jax import lax
from jax.experimental import pallas as pl
from jax.experimental.pallas import tpu as pltpu

HEAD_DIM = 64
N_HEADS = 8
GROUP_WIDTH = N_HEADS * HEAD_DIM
HEAD_PAIR_WIDTH = 2 * HEAD_DIM
MOBA_BLOCK = 256
MOBA_TOPK = 3
ATT_BLOCK = 256
QK_SCALE = HEAD_DIM ** -0.5
LN_EPS = 1e-5
NEG_BIG = -1e30
NEW_PAD = 128
VMEM_LIMIT = 48 * 1024 * 1024

F32 = jnp.float32
BF16 = jnp.bfloat16


def _dot(a, b):
    return jnp.dot(a, b, preferred_element_type=F32)


def _dot_nt(a, b):
    return lax.dot_general(a, b, (((1,), (1,)), ((), ())), preferred_element_type=F32)


def _split(x):
    hi = x.astype(BF16)
    lo = (x - hi.astype(F32)).astype(BF16)
    return hi, lo


def _dot_3pass(a, b):
    ah, al = _split(a)
    bh, bl = _split(b)
    return _dot(ah, bh) + (_dot(ah, bl) + _dot(al, bh))


def _suffix_matrix(n):
    j = lax.broadcasted_iota(jnp.int32, (n, n), 0)
    s = lax.broadcasted_iota(jnp.int32, (n, n), 1)
    return jnp.where(j > s, 1.0, 0.0).astype(BF16)


def _suffix_sums(x, t_mat):
    hi, lo = _split(x)
    m = x.shape[0]
    cs = _dot(jnp.concatenate([hi, lo], axis=0), t_mat)
    return cs[:m] + cs[m:]


def _softplus(z):
    return jnp.maximum(z, 0.0) + jnp.log(1.0 + jnp.exp(-jnp.abs(z)))


def _rank_before(g, n_valid, n_cand):
    lane = lax.broadcasted_iota(jnp.int32, g.shape, 1)
    cnt = jnp.zeros(g.shape, jnp.int32)
    for m in range(n_cand):
        gm = g[:, m:m + 1]
        beats = (gm > g) | ((gm == g) & (m < lane))
        cnt = cnt + jnp.where(beats & (m < n_valid), 1, 0)
    return cnt, lane


ROW_GROUPS = (0, 3, 4, 7)
KV_GROUPS = (1, 2, 5, 6)


def _proj_rows_kernel(x_ref, w_ref, wlo_ref, *out_refs):
    xh, xl = _split(x_ref[...])
    g = GROUP_WIDTH
    for c, o_ref in enumerate(out_refs):
        wc = w_ref[:, c * g:(c + 1) * g]
        out = _dot(xh, wc)
        if c in (4, 5):
            out = out + (_dot(xh, wlo_ref[:, (c - 4) * g:(c - 3) * g]) + _dot(xl, wc))
        o_ref[...] = out


def _proj_prompt_kernel(x_ref, w_ref, wlo_ref, wt_ref, wtlo_ref,
                        qa_ref, ga_ref, qb_ref, gb_ref, ka_ref, va_ref, kb_ref, vb_ref):
    xh, xl = _split(x_ref[...])
    g = GROUP_WIDTH
    for c, o_ref in enumerate((qa_ref, ga_ref, qb_ref, gb_ref)):
        wc = w_ref[:, c * g:(c + 1) * g]
        out = _dot(xh, wc)
        if o_ref is qb_ref:
            out = out + (_dot(xh, wlo_ref[...]) + _dot(xl, wc))
        o_ref[...] = out
    for c, o_ref in enumerate((ka_ref, va_ref, kb_ref, vb_ref)):
        wc = wt_ref[c * g:(c + 1) * g, :]
        out = _dot_nt(wc, xh)
        if o_ref is kb_ref:
            out = out + (_dot_nt(wtlo_ref[...], xh) + _dot_nt(wc, xl))
        o_ref[0] = out


def _full_spec(a):
    return pl.BlockSpec(a.shape, lambda i: (0,) * a.ndim)


def _project_rows(x2d, w_hi, w_lo):
    rows, d_model = x2d.shape
    out = jax.ShapeDtypeStruct((rows, GROUP_WIDTH), F32)
    return pl.pallas_call(
        _proj_rows_kernel,
        grid=(1,),
        in_specs=[_full_spec(x2d), _full_spec(w_hi), _full_spec(w_lo)],
        out_specs=[pl.BlockSpec((rows, GROUP_WIDTH), lambda i: (0, 0))] * 8,
        out_shape=[out] * 8,
        compiler_params=pltpu.CompilerParams(dimension_semantics=("arbitrary",),
                                             vmem_limit_bytes=VMEM_LIMIT),
        name="proj_rows",
    )(x2d, w_hi, w_lo)


def _project_prompt(x2d, w_row, w_row_lo, w_feat, w_feat_lo, batch, block_rows):
    rows, d_model = x2d.shape
    seq = rows // batch
    per_seq = seq // block_rows
    row_out = jax.ShapeDtypeStruct((rows, GROUP_WIDTH), F32)
    row_spec = pl.BlockSpec((block_rows, GROUP_WIDTH), lambda i: (i, 0))
    kv_out = jax.ShapeDtypeStruct((batch, GROUP_WIDTH, seq), F32)
    kv_spec = pl.BlockSpec((1, GROUP_WIDTH, block_rows), lambda i: (i // per_seq, 0, i % per_seq))
    return pl.pallas_call(
        _proj_prompt_kernel,
        grid=(rows // block_rows,),
        in_specs=[pl.BlockSpec((block_rows, d_model), lambda i: (i, 0)),
                  _full_spec(w_row), _full_spec(w_row_lo), _full_spec(w_feat), _full_spec(w_feat_lo)],
        out_specs=[row_spec] * 4 + [kv_spec] * 4,
        out_shape=[row_out] * 4 + [kv_out] * 4,
        compiler_params=pltpu.CompilerParams(dimension_semantics=("arbitrary",),
                                             vmem_limit_bytes=VMEM_LIMIT),
        name="proj_prompt",
    )(x2d, w_row, w_row_lo, w_feat, w_feat_lo)


def _head_pair_rows(q2):
    lane = lax.broadcasted_iota(jnp.int32, q2.shape, 1)
    zero = jnp.zeros_like(q2)
    return jnp.concatenate([jnp.where(lane < HEAD_DIM, q2, zero),
                            jnp.where(lane >= HEAD_DIM, q2, zero)], axis=0)


def _head_pair_merge(acc, tq):
    lane = lax.broadcasted_iota(jnp.int32, (tq, HEAD_PAIR_WIDTH), 1)
    return jnp.where(lane < HEAD_DIM, acc[:tq], acc[tq:])


def _key_block(ref, kj):
    cols = pl.ds(pl.multiple_of(kj * ATT_BLOCK, ATT_BLOCK), ATT_BLOCK)
    return ref[0, :, cols].astype(BF16)


def _sb_prompt_kernel(q_ref, kt_ref, vt_ref, o_ref, t_ref):
    b, hp, qi = pl.program_id(0), pl.program_id(1), pl.program_id(2)
    tq = ATT_BLOCK

    @pl.when((b == 0) & (hp == 0) & (qi == 0))
    def _():
        t_ref[...] = _suffix_matrix(ATT_BLOCK)

    qm = _head_pair_rows((q_ref[0] * QK_SCALE).astype(BF16))
    t_mat = t_ref[...]

    def step(kj, carry, diagonal):
        run, acc = carry
        z = _dot(qm, _key_block(kt_ref, kj))
        sp = _softplus(z)
        log_keep = -sp
        if diagonal:
            r = lax.broadcasted_iota(jnp.int32, z.shape, 0) & (tq - 1)
            c = lax.broadcasted_iota(jnp.int32, z.shape, 1)
            strict = c < r
            log_keep = jnp.where(strict, log_keep, 0.0)
        between = _suffix_sums(log_keep, t_mat) + run
        w = jnp.exp((z - sp) + between)
        if diagonal:
            w = jnp.where(strict, w, 0.0)
        run = run + jnp.sum(log_keep, axis=1, keepdims=True)
        acc = acc + _dot_nt(w.astype(BF16), _key_block(vt_ref, kj))
        return run, acc

    carry = (jnp.zeros((2 * tq, 1), F32), jnp.zeros((2 * tq, HEAD_PAIR_WIDTH), F32))
    carry = step(qi, carry, True)
    carry = lax.fori_loop(0, qi, lambda i, c: step(qi - 1 - i, c, False), carry)
    o_ref[0] = _head_pair_merge(carry[1], tq)


def _prompt_specs(q, kt):
    bsz, seq, _ = q.shape
    n_pairs = GROUP_WIDTH // HEAD_PAIR_WIDTH
    q_spec = pl.BlockSpec((1, ATT_BLOCK, HEAD_PAIR_WIDTH), lambda b, h, i: (b, i, h))
    kv_spec = pl.BlockSpec((1, HEAD_PAIR_WIDTH, seq), lambda b, h, i: (b, h, 0))
    return (bsz, n_pairs, seq // ATT_BLOCK), q_spec, kv_spec


def _sb_prompt(q, kt, vt):
    grid, q_spec, kv_spec = _prompt_specs(q, kt)
    return pl.pallas_call(
        _sb_prompt_kernel,
        grid=grid,
        in_specs=[q_spec, kv_spec, kv_spec],
        out_specs=q_spec,
        out_shape=jax.ShapeDtypeStruct(q.shape, F32),
        scratch_shapes=[pltpu.VMEM((ATT_BLOCK, ATT_BLOCK), BF16)],
        compiler_params=pltpu.CompilerParams(dimension_semantics=("arbitrary",) * 3,
                                             vmem_limit_bytes=VMEM_LIMIT),
        name="sb_prompt",
    )(q, kt, vt)


def _moba_prompt_kernel(slopes_ref, q_ref, kt_ref, vt_ref, o_ref, kmean_ref):
    hp, qi = pl.program_id(1), pl.program_id(2)
    tq = ATT_BLOCK
    n_blocks = kt_ref.shape[2] // MOBA_BLOCK

    @pl.when(qi == 0)
    def _():
        lane = lax.broadcasted_iota(jnp.int32, kmean_ref.shape, 1)
        km = jnp.zeros(kmean_ref.shape, F32)
        for n in range(n_blocks):
            blk = kt_ref[0, :, n * MOBA_BLOCK:(n + 1) * MOBA_BLOCK]
            km = jnp.where(lane == n, jnp.sum(blk, axis=1, keepdims=True) * (1.0 / MOBA_BLOCK), km)
        kmean_ref[...] = km

    q2 = q_ref[0]
    gate = _dot_3pass(_head_pair_rows(q2), kmean_ref[...])
    cnt, lane = _rank_before(gate, qi, n_blocks)
    sel = jnp.where((lane < qi) & (cnt < MOBA_TOPK), 1.0, 0.0)

    qm = _head_pair_rows((q2 * QK_SCALE).astype(BF16))
    row = lax.broadcasted_iota(jnp.int32, (2 * tq, 1), 0)
    slope = jnp.where(row < tq, slopes_ref[2 * hp], slopes_ref[2 * hp + 1])
    r_loc = lax.broadcasted_iota(jnp.int32, (2 * tq, MOBA_BLOCK), 0) & (tq - 1)
    c_loc = lax.broadcasted_iota(jnp.int32, (2 * tq, MOBA_BLOCK), 1)
    d0 = (r_loc - c_loc).astype(F32)

    def step(kj, carry, own):
        m_run, l_run, acc = carry
        s = _dot(qm, _key_block(kt_ref, kj))
        dist = d0 + ((qi - kj) * MOBA_BLOCK).astype(F32)
        s = s - slope * dist
        if own:
            valid = c_loc <= r_loc
        else:
            picked = jnp.sum(jnp.where(lane == kj, sel, 0.0), axis=1, keepdims=True)
            valid = picked > 0.5
        s = jnp.where(valid, s, NEG_BIG)
        m_new = jnp.maximum(m_run, jnp.max(s, axis=1, keepdims=True))
        p = jnp.exp(s - m_new)
        alpha = jnp.exp(m_run - m_new)
        l_run = alpha * l_run + jnp.sum(p, axis=1, keepdims=True)
        acc = alpha * acc + _dot_nt(p.astype(BF16), _key_block(vt_ref, kj))
        return m_new, l_run, acc

    carry = (jnp.full((2 * tq, 1), NEG_BIG, F32), jnp.zeros((2 * tq, 1), F32),
             jnp.zeros((2 * tq, HEAD_PAIR_WIDTH), F32))
    carry = step(qi, carry, True)
    carry = lax.fori_loop(0, qi, lambda i, c: step(qi - 1 - i, c, False), carry)
    o_ref[0] = _head_pair_merge(carry[2] / carry[1], tq)


def _moba_prompt(slopes, q, kt, vt):
    grid, q_spec, kv_spec = _prompt_specs(q, kt)
    return pl.pallas_call(
        _moba_prompt_kernel,
        grid=grid,
        in_specs=[pl.BlockSpec(memory_space=pltpu.SMEM), q_spec, kv_spec, kv_spec],
        out_specs=q_spec,
        out_shape=jax.ShapeDtypeStruct(q.shape, F32),
        scratch_shapes=[pltpu.VMEM((HEAD_PAIR_WIDTH, 128), F32)],
        compiler_params=pltpu.CompilerParams(dimension_semantics=("arbitrary",) * 3,
                                             vmem_limit_bytes=VMEM_LIMIT),
        name="moba_prompt",
    )(slopes, q, kt, vt)


def _sample_kernel(pt_ref, slope_ref, qa_ref, qb_ref, kan_ref, van_ref, kbn_ref, vbn_ref,
                   ka0_ref, ka1_ref, va0_ref, va1_ref, kb0_ref, kb1_ref, vb0_ref, vb1_ref,
                   oa_ref, ob_ref,
                   t_ref, qa_s, qb_s, run_s, acca_s, gate_s, mx_s, l_s, accb_s,
                   mo_s, lo_s, acco_s, new_s):
    del pt_ref
    j = pl.program_id(1)
    n_steps = pl.num_programs(1)
    n = n_steps - 1 - j
    n_tok = qa_ref.shape[1]
    n_rows = n_tok * N_HEADS
    blk = MOBA_BLOCK

    row = lax.broadcasted_iota(jnp.int32, (n_rows, GROUP_WIDTH), 0)
    lane_w = lax.broadcasted_iota(jnp.int32, (n_rows, GROUP_WIDTH), 1)
    head_mask = (lane_w // HEAD_DIM) == (row % N_HEADS)
    slope = slope_ref[:, 0:1]
    lane = lax.broadcasted_iota(jnp.int32, (n_rows, 128), 1)

    @pl.when(j == 0)
    def _():
        t_ref[...] = _suffix_matrix(blk)

        def expand(q):
            rep = jnp.concatenate([jnp.broadcast_to(q[t:t + 1], (N_HEADS, GROUP_WIDTH))
                                   for t in range(n_tok)], axis=0)
            return jnp.where(head_mask, rep, 0.0)

        qa_s[...] = (expand(qa_ref[0]) * QK_SCALE).astype(BF16)
        qh, ql = _split(expand(qb_ref[0]))
        qb_s[...] = jnp.concatenate([qh, ql], axis=0)
        gate_s[...] = jnp.zeros_like(gate_s)
        mx_s[...] = jnp.zeros_like(mx_s)
        l_s[...] = jnp.zeros_like(l_s)

        def padded(ref):
            new_s[...] = jnp.zeros_like(new_s)
            new_s[0:n_tok, :] = ref[0]
            return new_s[...].astype(BF16)

        tok = (lax.broadcasted_iota(jnp.int32, (n_rows, NEW_PAD), 0) // N_HEADS)
        col = lax.broadcasted_iota(jnp.int32, (n_rows, NEW_PAD), 1)
        z = _dot_nt(qa_s[...], padded(kan_ref))
        strict = col < tok
        sp = _softplus(z)
        log_keep = jnp.where(strict, -sp, 0.0)
        between = _suffix_sums(log_keep, t_ref[0:NEW_PAD, 0:NEW_PAD])
        w = jnp.where(strict, jnp.exp((z - sp) + between), 0.0)
        run_s[...] = jnp.sum(log_keep, axis=1, keepdims=True)
        acca_s[...] = _dot(w.astype(BF16), padded(van_ref))
        s = _dot_nt(qb_s[0:n_rows, :], padded(kbn_ref)) * QK_SCALE
        s = s - slope * (tok - col).astype(F32)
        s = jnp.where(col <= tok, s, NEG_BIG)
        m_o = jnp.max(s, axis=1, keepdims=True)
        p = jnp.exp(s - m_o)
        mo_s[...] = m_o
        lo_s[...] = jnp.sum(p, axis=1, keepdims=True)
        acco_s[...] = _dot(p.astype(BF16), padded(vbn_ref))

    def both(r0, r1):
        return jnp.concatenate([r0[0], r1[0]], axis=1)

    ka = both(ka0_ref, ka1_ref).astype(BF16)
    va = both(va0_ref, va1_ref).astype(BF16)
    z = _dot(qa_s[...], ka)
    sp = _softplus(z)
    log_keep = -sp
    between = _suffix_sums(log_keep, t_ref[...]) + run_s[...]
    w = jnp.exp((z - sp) + between)
    run_s[...] = run_s[...] + jnp.sum(log_keep, axis=1, keepdims=True)
    acca_s[...] = acca_s[...] + _dot_nt(w.astype(BF16), va)

    kb_h, kb_l = _split(both(kb0_ref, kb1_ref))
    vb = both(vb0_ref, vb1_ref).astype(BF16)
    raw = _dot(qb_s[...], kb_h)
    raw_hh = raw[0:n_rows]
    raw3 = raw_hh + (raw[n_rows:] + _dot(qb_s[0:n_rows, :], kb_l))
    gate_n = jnp.sum(raw3, axis=1, keepdims=True) * (1.0 / blk)
    tok = lax.broadcasted_iota(jnp.int32, (n_rows, blk), 0) // N_HEADS
    col = lax.broadcasted_iota(jnp.int32, (n_rows, blk), 1)
    dist = ((n_steps - n) * blk + tok - col).astype(F32)
    s = raw_hh * QK_SCALE - slope * dist
    m_n = jnp.max(s, axis=1, keepdims=True)
    p = jnp.exp(s - m_n)
    l_n = jnp.sum(p, axis=1, keepdims=True)
    here = lane == n
    gate_s[...] = jnp.where(here, gate_n, gate_s[...])
    mx_s[...] = jnp.where(here, m_n, mx_s[...])
    l_s[...] = jnp.where(here, l_n, l_s[...])
    accb_s[n] = _dot_nt(p.astype(BF16), vb)

    @pl.when(j == n_steps - 1)
    def _():
        def fold(acc):
            kept = jnp.where(head_mask, acc, 0.0)
            return jnp.sum(kept.reshape(n_tok, N_HEADS, GROUP_WIDTH), axis=1)

        oa_ref[0] = fold(acca_s[...])
        g = gate_s[...]
        n_cached = accb_s.shape[0]
        cnt, ln = _rank_before(g, n_cached, n_cached)
        sel = (ln < n_cached) & (cnt < MOBA_TOPK)
        mx = jnp.where(sel, mx_s[...], NEG_BIG)
        m_o = mo_s[...]
        m_all = jnp.maximum(m_o, jnp.max(mx, axis=1, keepdims=True))
        c = jnp.where(sel, jnp.exp(mx - m_all), 0.0)
        c_o = jnp.exp(m_o - m_all)
        l_tot = c_o * lo_s[...] + jnp.sum(c * l_s[...], axis=1, keepdims=True)
        acc = c_o * acco_s[...]
        for nb in range(n_cached):
            acc = acc + c[:, nb:nb + 1] * accb_s[nb]
        ob_ref[0] = fold(acc / l_tot)


def _sample_attention(page_table, slope_rows, qa, qb, ka_new, va_new, kb_new, vb_new,
                      ck_sb, cv_sb, ck_mb, cv_mb):
    dbs, n_tok, _ = qa.shape
    n_pages = page_table.shape[1]
    page = ck_sb.shape[2]
    assert 2 * page == MOBA_BLOCK and n_pages % 2 == 0
    n_steps = n_pages // 2
    n_rows = n_tok * N_HEADS

    tok_spec = pl.BlockSpec((1, n_tok, GROUP_WIDTH), lambda b, j, pt: (b, 0, 0))

    def page_spec(which):
        return pl.BlockSpec((1, GROUP_WIDTH, page),
                            lambda b, j, pt: (pt[b, 2 * (n_steps - 1 - j) + which], 0, 0))

    grid_spec = pltpu.PrefetchScalarGridSpec(
        num_scalar_prefetch=1,
        grid=(dbs, n_steps),
        in_specs=[pl.BlockSpec((n_rows, 128), lambda b, j, pt: (0, 0))] + [tok_spec] * 6
                 + [page_spec(0), page_spec(1)] * 4,
        out_specs=[tok_spec, tok_spec],
        scratch_shapes=[
            pltpu.VMEM((MOBA_BLOCK, MOBA_BLOCK), BF16),
            pltpu.VMEM((n_rows, GROUP_WIDTH), BF16),
            pltpu.VMEM((2 * n_rows, GROUP_WIDTH), BF16),
            pltpu.VMEM((n_rows, 1), F32),
            pltpu.VMEM((n_rows, GROUP_WIDTH), F32),
            pltpu.VMEM((n_rows, 128), F32),
            pltpu.VMEM((n_rows, 128), F32),
            pltpu.VMEM((n_rows, 128), F32),
            pltpu.VMEM((n_steps, n_rows, GROUP_WIDTH), F32),
            pltpu.VMEM((n_rows, 1), F32),
            pltpu.VMEM((n_rows, 1), F32),
            pltpu.VMEM((n_rows, GROUP_WIDTH), F32),
            pltpu.VMEM((NEW_PAD, GROUP_WIDTH), F32),
        ],
    )
    out = jax.ShapeDtypeStruct((dbs, n_tok, GROUP_WIDTH), F32)
    return pl.pallas_call(
        _sample_kernel,
        grid_spec=grid_spec,
        out_shape=[out, out],
        compiler_params=pltpu.CompilerParams(dimension_semantics=("arbitrary", "arbitrary"),
                                             vmem_limit_bytes=VMEM_LIMIT),
        name="sample_attn",
    )(page_table, slope_rows, qa, qb, ka_new, va_new, kb_new, vb_new,
      ck_sb, ck_sb, cv_sb, cv_sb, ck_mb, ck_mb, cv_mb, cv_mb)


def _mix_out_kernel(alpha, x_ref, oa_ref, ga_ref, ob_ref, gb_ref, w_ref, gain_ref, bias_ref, y_ref):
    h = jnp.concatenate([oa_ref[...] * jax.nn.silu(ga_ref[...]),
                         ob_ref[...] * jax.nn.silu(gb_ref[...])], axis=-1)
    out = _dot(h.astype(BF16), w_ref[...])
    y = alpha * x_ref[...] + out
    mu = jnp.mean(y, axis=-1, keepdims=True)
    var = jnp.mean(jnp.square(y - mu), axis=-1, keepdims=True)
    y_ref[...] = (y - mu) * lax.rsqrt(var + LN_EPS) * gain_ref[...] + bias_ref[...]


def _mix_out(x2d, oa, ga, ob, gb, w_out, gain, bias, alpha, block_rows):
    rows, d_model = x2d.shape
    g_spec = pl.BlockSpec((block_rows, GROUP_WIDTH), lambda i: (i, 0))
    x_spec = pl.BlockSpec((block_rows, d_model), lambda i: (i, 0))
    vec_spec = pl.BlockSpec((1, d_model), lambda i: (0, 0))
    return pl.pallas_call(
        functools.partial(_mix_out_kernel, alpha),
        grid=(rows // block_rows,),
        in_specs=[x_spec, g_spec, g_spec, g_spec, g_spec,
                  pl.BlockSpec(w_out.shape, lambda i: (0, 0)), vec_spec, vec_spec],
        out_specs=x_spec,
        out_shape=jax.ShapeDtypeStruct(x2d.shape, F32),
        compiler_params=pltpu.CompilerParams(dimension_semantics=("arbitrary",),
                                             vmem_limit_bytes=VMEM_LIMIT),
        name="mix_out",
    )(x2d, oa, ga, ob, gb, w_out, gain, bias)


def kernel(x_prompt, x_sample, cache_k_sb, cache_v_sb, cache_k_moba, cache_v_moba, page_table,
           w_in, w_out, ln_gain, ln_bias):
    depth = w_in.shape[0]
    assert depth == 1, "single-layer trunk"
    bsz, seq, d_model = x_prompt.shape
    dbs, n_tok, _ = x_sample.shape
    alpha = (2.0 * depth) ** 0.25
    slopes = jnp.asarray([2.0 ** (-8.0 * (i + 1) / N_HEADS) for i in range(N_HEADS)], dtype=F32)
    slope_rows = jnp.broadcast_to(jnp.tile(slopes, n_tok)[:, None], (n_tok * N_HEADS, 128))

    g = GROUP_WIDTH
    w = w_in[0]
    w_hi = w.astype(BF16)
    w_lo = (w[:, 4 * g:6 * g] - w_hi[:, 4 * g:6 * g].astype(F32)).astype(BF16)
    w_row = jnp.concatenate([w_hi[:, c * g:(c + 1) * g] for c in ROW_GROUPS], axis=1)
    w_feat = jnp.concatenate([w_hi[:, c * g:(c + 1) * g] for c in KV_GROUPS], axis=1).T
    w_row_lo = w_lo[:, 0:g]
    w_feat_lo = w_lo[:, g:2 * g].T
    w_o = w_out[0].astype(BF16)
    gain = ln_gain[0][None, :]
    bias = ln_bias[0][None, :]

    def pages(c):
        return jnp.transpose(c[0], (0, 2, 3, 1)).reshape(c.shape[1], GROUP_WIDTH, c.shape[2])

    def heads_from_features(a):
        return jnp.transpose(a.reshape(bsz, N_HEADS, HEAD_DIM, seq), (0, 3, 1, 2))[None]

    def heads_from_rows(a):
        return a.reshape(1, dbs, n_tok, N_HEADS, HEAD_DIM)

    xp = x_prompt.reshape(bsz * seq, d_model)
    qa, ga, qb, gb, kat, vat, kbt, vbt = _project_prompt(xp, w_row, w_row_lo, w_feat, w_feat_lo, bsz, 512)
    shp = (bsz, seq, GROUP_WIDTH)
    o_a = _sb_prompt(qa.reshape(shp), kat, vat)
    o_b = _moba_prompt(slopes, qb.reshape(shp), kbt, vbt)
    y_p = _mix_out(xp, o_a.reshape(xp.shape[0], GROUP_WIDTH), ga, o_b.reshape(xp.shape[0], GROUP_WIDTH), gb,
                   w_o, gain, bias, alpha, 512)

    xs = x_sample.reshape(dbs * n_tok, d_model)
    sqa, ska, sva, sga, sqb, skb, svb, sgb = _project_rows(xs, w_hi, w_lo)
    sshp = (dbs, n_tok, GROUP_WIDTH)
    so_a, so_b = _sample_attention(
        page_table, slope_rows, sqa.reshape(sshp), sqb.reshape(sshp),
        ska.reshape(sshp), sva.reshape(sshp), skb.reshape(sshp), svb.reshape(sshp),
        pages(cache_k_sb), pages(cache_v_sb), pages(cache_k_moba), pages(cache_v_moba))
    y_s = _mix_out(xs, so_a.reshape(xs.shape[0], GROUP_WIDTH), sga, so_b.reshape(xs.shape[0], GROUP_WIDTH), sgb,
                   w_o, gain, bias, alpha, dbs * n_tok)

    return (y_p.reshape(x_prompt.shape), y_s.reshape(x_sample.shape),
            heads_from_features(kat), heads_from_features(vat),
            heads_from_features(kbt), heads_from_features(vbt),
            heads_from_rows(ska), heads_from_rows(sva), heads_from_rows(skb), heads_from_rows(svb))
```

```python
import functools
import math

import jax
import jax.numpy as jnp
from jax import lax
from jax.experimental import pallas as pl
from jax.experimental.pallas import tpu as pltpu

HEAD_DIM = 64
N_HEADS = 8
GROUP_WIDTH = N_HEADS * HEAD_DIM
HEAD_PAIR_WIDTH = 2 * HEAD_DIM
LANES = 128
MOBA_BLOCK = 256
MOBA_TOPK = 3
ATT_BLOCK = 256
QK_SCALE = HEAD_DIM ** -0.5
LOG2E = math.log2(math.e)
LN_EPS = 1e-5
NEG_BIG = -1e30
SOFTPLUS2_CLAMP = 100.0
NEW_PAD = 128
PAGES_PER_STEP = 4
ONES_ROWS = 16
VMEM_LIMIT = 48 * 1024 * 1024

AUG_ONE = 0
AUG_KEYPOS = 3
AUG_BLOCK = 8

F32 = jnp.float32
BF16 = jnp.bfloat16


def _dot(a, b):
    return jnp.dot(a, b, preferred_element_type=F32)


def _dot_nt(a, b):
    return lax.dot_general(a, b, (((1,), (1,)), ((), ())), preferred_element_type=F32)


def _split(x):
    hi = x.astype(BF16)
    lo = (x - hi.astype(F32)).astype(BF16)
    return hi, lo


def _split3(x):
    hi = x.astype(BF16)
    r = x - hi.astype(F32)
    mid = r.astype(BF16)
    lo = (r - mid.astype(F32)).astype(BF16)
    return hi, mid, lo


def _dot_3pass(a, b):
    ah, al = _split(a)
    bh, bl = _split(b)
    return _dot(ah, bh) + (_dot(ah, bl) + _dot(al, bh))


def _softplus(z):
    return jnp.maximum(z, 0.0) + jnp.log(1.0 + jnp.exp(-jnp.abs(z)))


def _softplus2(z):
    return jnp.maximum(jnp.log2(1.0 + jnp.exp2(jnp.minimum(z, SOFTPLUS2_CLAMP))), z)


def _full_spec(a, single_buffer=False):
    mode = pl.Buffered(1) if single_buffer else None
    return pl.BlockSpec(a.shape, lambda *_: (0,) * a.ndim, pipeline_mode=mode)


def _proj_rows_kernel(x_ref, w_ref, wlo_ref, *out_refs):
    xh, xl = _split(x_ref[...])
    g = GROUP_WIDTH
    for c, o_ref in enumerate(out_refs):
        wc = w_ref[:, c * g:(c + 1) * g]
        out = _dot(xh, wc)
        if c in (4, 5):
            out = out + (_dot(xh, wlo_ref[:, (c - 4) * g:(c - 3) * g]) + _dot(xl, wc))
        o_ref[...] = out


def _project_rows(x2d, w_hi, w_lo):
    rows, _ = x2d.shape
    out = jax.ShapeDtypeStruct((rows, GROUP_WIDTH), F32)
    return pl.pallas_call(
        _proj_rows_kernel,
        grid=(1,),
        in_specs=[_full_spec(x2d), _full_spec(w_hi), _full_spec(w_lo)],
        out_specs=[pl.BlockSpec((rows, GROUP_WIDTH), lambda i: (0, 0))] * 8,
        out_shape=[out] * 8,
        compiler_params=pltpu.CompilerParams(dimension_semantics=("arbitrary",),
                                             vmem_limit_bytes=VMEM_LIMIT),
        name="proj_rows",
    )(x2d, w_hi, w_lo)


ROW_GROUPS = (3, 7, 1, 5)
FEAT_GROUPS = (0, 4, 1, 2, 5, 6)
FEAT_3PASS = (1, 4)


def _proj_prompt_kernel(x_ref, wr_ref, wrlo_ref, wf_ref, wflo_ref,
                        ga_ref, gb_ref, kar_ref, kbr_ref, kmean_ref,
                        qat_ref, qbt_ref, kat_ref, vat_ref, kbt_ref, vbt_ref):
    x = x_ref[...]
    xh, xl = _split(x)
    g = GROUP_WIDTH
    for c, o_ref in enumerate((ga_ref, gb_ref, kar_ref, kbr_ref)):
        o_ref[...] = _dot(xh, wr_ref[:, c * g:(c + 1) * g]).astype(o_ref.dtype)
    lo_slot = 0
    for c, o_ref in enumerate((qat_ref, qbt_ref, kat_ref, vat_ref, kbt_ref, vbt_ref)):
        wc = wf_ref[c * g:(c + 1) * g, :]
        out = _dot_nt(wc, xh)
        if c in FEAT_3PASS:
            out = out + (_dot_nt(wflo_ref[lo_slot * g:(lo_slot + 1) * g, :], xh) + _dot_nt(wc, xl))
            lo_slot += 1
        o_ref[0] = out
    n_blk = x.shape[0] // MOBA_BLOCK
    row8 = lax.broadcasted_iota(jnp.int32, (8, x.shape[1]), 0)
    xbar = jnp.zeros((8, x.shape[1]), F32)
    for i in range(n_blk):
        mean_i = jnp.sum(x[i * MOBA_BLOCK:(i + 1) * MOBA_BLOCK], axis=0, keepdims=True) * (1.0 / MOBA_BLOCK)
        xbar = jnp.where(row8 == i, mean_i, xbar)
    bh, bl = _split(xbar)
    wk = wr_ref[:, 3 * g:4 * g]
    km = _dot(bh, wk) + (_dot(bh, wrlo_ref[...]) + _dot(bl, wk))
    kmean_ref[0] = km[0:n_blk]


def _project_prompt(x2d, w_row, w_row_lo, w_feat, w_feat_lo, batch, block_rows):
    rows, d_model = x2d.shape
    seq = rows // batch
    per_seq = seq // block_rows
    n_blk = block_rows // MOBA_BLOCK
    row_spec = pl.BlockSpec((block_rows, GROUP_WIDTH), lambda i: (i, 0))
    feat_spec = pl.BlockSpec((1, GROUP_WIDTH, block_rows), lambda i: (i // per_seq, 0, i % per_seq))
    row_f32 = jax.ShapeDtypeStruct((rows, GROUP_WIDTH), F32)
    row_bf16 = jax.ShapeDtypeStruct((rows, GROUP_WIDTH), BF16)
    feat = jax.ShapeDtypeStruct((batch, GROUP_WIDTH, seq), F32)
    kmean = jax.ShapeDtypeStruct((rows // block_rows, n_blk, GROUP_WIDTH), F32)
    return pl.pallas_call(
        _proj_prompt_kernel,
        grid=(rows // block_rows,),
        in_specs=[pl.BlockSpec((block_rows, d_model), lambda i: (i, 0)),
                  _full_spec(w_row, True), _full_spec(w_row_lo, True),
                  _full_spec(w_feat, True), _full_spec(w_feat_lo, True)],
        out_specs=[row_spec] * 4 + [pl.BlockSpec((1, n_blk, GROUP_WIDTH), lambda i: (i, 0, 0))] + [feat_spec] * 6,
        out_shape=[row_f32, row_f32, row_bf16, row_bf16, kmean] + [feat] * 6,
        compiler_params=pltpu.CompilerParams(dimension_semantics=("arbitrary",),
                                             vmem_limit_bytes=VMEM_LIMIT),
        name="proj_prompt",
    )(x2d, w_row, w_row_lo, w_feat, w_feat_lo)


def _pair_columns(qt, tq):
    row = lax.broadcasted_iota(jnp.int32, qt.shape, 0)
    zero = jnp.zeros_like(qt)
    return jnp.concatenate([jnp.where(row < HEAD_DIM, qt, zero),
                            jnp.where(row >= HEAD_DIM, qt, zero)], axis=1)


def _pair_merge_rows(acc_t, tq):
    row = lax.broadcasted_iota(jnp.int32, (HEAD_PAIR_WIDTH, tq), 0)
    return jnp.where(row < HEAD_DIM, acc_t[:, :tq], acc_t[:, tq:]).T


def _blk(kj):
    return pl.ds(pl.multiple_of(kj * ATT_BLOCK, ATT_BLOCK), ATT_BLOCK)


def _prompt_grid(seq, bsz):
    return (bsz, GROUP_WIDTH // HEAD_PAIR_WIDTH, seq // ATT_BLOCK)


def _qt_spec():
    return pl.BlockSpec((1, HEAD_PAIR_WIDTH, ATT_BLOCK), lambda b, h, i: (b, h, i))


def _krows_spec(seq):
    return pl.BlockSpec((1, seq, HEAD_PAIR_WIDTH), lambda b, h, i: (b, 0, h))


def _vt_spec(seq):
    return pl.BlockSpec((1, HEAD_PAIR_WIDTH, seq), lambda b, h, i: (b, h, 0))


def _orow_spec():
    return pl.BlockSpec((1, ATT_BLOCK, HEAD_PAIR_WIDTH), lambda b, h, i: (b, i, h))


def _sb_prompt_kernel(qt_ref, k_ref, vt_ref, o_ref,
                      u_ref, vtb_ref, hilo_s, z_s, loc_s, run_s, acc_s):
    b, hp, qi = pl.program_id(0), pl.program_id(1), pl.program_id(2)
    tq = ATT_BLOCK
    half = tq // 2
    u_rows = half + ONES_ROWS

    @pl.when((b == 0) & (hp == 0) & (qi == 0))
    def _():
        s_i = lax.broadcasted_iota(jnp.int32, (u_rows, tq), 0)
        j_i = lax.broadcasted_iota(jnp.int32, (u_rows, tq), 1) & (half - 1)
        u_ref[...] = jnp.where((j_i >= s_i) | (s_i >= half), -1.0, 0.0).astype(BF16)

    @pl.when(qi == 0)
    def _():
        vtb_ref[...] = vt_ref[0].astype(BF16)

    q_cols = _pair_columns((qt_ref[0] * (QK_SCALE * LOG2E)).astype(BF16), tq)

    def scores(t, diagonal):
        z = _dot(k_ref[0, _blk(qi - t), :], q_cols)
        sp = _softplus2(z)
        if diagonal:
            key = lax.broadcasted_iota(jnp.int32, z.shape, 0)
            qry = lax.broadcasted_iota(jnp.int32, z.shape, 1) & (tq - 1)
            strict = key < qry
            sp = jnp.where(strict, sp, 0.0)
            z = jnp.where(strict, z, NEG_BIG)
        hi, lo = _split(sp)
        s2, s4 = t & 1, t & 3
        for h in range(2):
            rows = slice(h * half, (h + 1) * half)
            hilo_s[s2, h * tq:h * tq + half, :] = hi[rows]
            hilo_s[s2, h * tq + half:(h + 1) * tq, :] = lo[rows]
        z_s[s4] = z

    def suffix(t):
        s2 = t & 1
        neg_u = u_ref[...]
        for h in range(2):
            loc_s[s2, h * u_rows:(h + 1) * u_rows, :] = _dot(neg_u, hilo_s[s2, h * tq:(h + 1) * tq, :])

    def weigh(t):
        s2, s4 = t & 1, t & 3
        run = run_s[...]
        tot_first = loc_s[s2, half:half + 1, :]
        tot_second = loc_s[s2, u_rows + half:u_rows + half + 1, :]
        first = (z_s[s4, 0:half, :] + loc_s[s2, 0:half, :]) + (run + tot_second)
        second = (z_s[s4, half:tq, :] + loc_s[s2, u_rows:u_rows + half, :]) + run
        w = jnp.exp2(jnp.concatenate([first, second], axis=0)).astype(BF16)
        run_s[...] = run + (tot_first + tot_second)
        acc_s[...] = acc_s[...] + _dot(vtb_ref[:, _blk(qi - t)], w)

    run_s[...] = jnp.zeros_like(run_s)
    acc_s[...] = jnp.zeros_like(acc_s)
    scores(0, True)

    @pl.when(qi >= 1)
    def _():
        scores(1, False)

    suffix(0)

    def steady(t, carry):
        weigh(t - 2)
        suffix(t - 1)
        scores(t, False)
        return carry

    lax.fori_loop(2, qi + 1, steady, 0)

    @pl.when(qi >= 1)
    def _():
        weigh(qi - 1)
        suffix(qi)

    weigh(qi)
    o_ref[0] = _pair_merge_rows(acc_s[...], tq)


def _sb_prompt(qt, k_rows, vt):
    bsz, _, seq = qt.shape
    tq = ATT_BLOCK
    return pl.pallas_call(
        _sb_prompt_kernel,
        grid=_prompt_grid(seq, bsz),
        in_specs=[_qt_spec(), _krows_spec(seq), _vt_spec(seq)],
        out_specs=_orow_spec(),
        out_shape=jax.ShapeDtypeStruct((bsz, seq, GROUP_WIDTH), F32),
        scratch_shapes=[pltpu.VMEM((tq // 2 + ONES_ROWS, tq), BF16),
                        pltpu.VMEM((HEAD_PAIR_WIDTH, seq), BF16),
                        pltpu.VMEM((2, 2 * tq, 2 * tq), BF16),
                        pltpu.VMEM((4, tq, 2 * tq), F32),
                        pltpu.VMEM((2, tq + 2 * ONES_ROWS, 2 * tq), F32),
                        pltpu.VMEM((1, 2 * tq), F32),
                        pltpu.VMEM((HEAD_PAIR_WIDTH, 2 * tq), F32)],
        compiler_params=pltpu.CompilerParams(dimension_semantics=("arbitrary",) * 3,
                                             vmem_limit_bytes=VMEM_LIMIT),
        name="sb_prompt",
    )(qt, k_rows, vt)


def _rank_before_rows(g, n_valid, n_cand):
    blk = lax.broadcasted_iota(jnp.int32, g.shape, 0)
    cnt = jnp.zeros(g.shape, jnp.int32)
    for m in range(n_cand):
        gm = g[m:m + 1, :]
        beats = (gm > g) | ((gm == g) & (m < blk))
        cnt = cnt + jnp.where(beats & (m < n_valid), 1, 0)
    return cnt, blk


def _moba_prompt_kernel(slopes_ref, qt_ref, k_ref, vt_ref, kmean_ref, o_ref,
                        kaug_ref, vaug_ref, p_s, top_s, m_s, acc_s):
    hp, qi = pl.program_id(1), pl.program_id(2)
    tq = ATT_BLOCK
    seq = k_ref.shape[1]
    n_blocks = seq // MOBA_BLOCK

    @pl.when(qi == 0)
    def _():
        kaug_ref[:, 0:HEAD_PAIR_WIDTH] = k_ref[0]
        lane = lax.broadcasted_iota(jnp.int32, (MOBA_BLOCK, LANES), 1)
        key = lax.broadcasted_iota(jnp.int32, (MOBA_BLOCK, LANES), 0).astype(F32)
        base = jnp.where(lane < AUG_ONE + 3, 1.0,
                         jnp.where(lane < AUG_KEYPOS + 3, key, 0.0))
        for n in range(n_blocks):
            kaug_ref[n * MOBA_BLOCK:(n + 1) * MOBA_BLOCK, HEAD_PAIR_WIDTH:] = (
                jnp.where(lane == AUG_BLOCK + n, 1.0, base).astype(BF16))
        vaug_ref[0:HEAD_PAIR_WIDTH, :] = vt_ref[0].astype(BF16)
        vaug_ref[HEAD_PAIR_WIDTH:, :] = jnp.ones((ONES_ROWS, seq), BF16)

    qt = qt_ref[0]
    q_cols = _pair_columns(qt, tq)
    gate = _dot_3pass(kmean_ref[0], q_cols)
    cnt, blk = _rank_before_rows(gate, qi, n_blocks)
    chosen = ((blk < qi) & (cnt < MOBA_TOPK)) | (blk == qi)
    choice_bias = jnp.where(chosen, 0.0, NEG_BIG)

    lane = lax.broadcasted_iota(jnp.int32, (1, 2 * tq), 1)
    slope = jnp.where(lane < tq, slopes_ref[2 * hp], slopes_ref[2 * hp + 1]) * LOG2E
    q_off = (lane & (tq - 1)).astype(F32)
    row8 = lax.broadcasted_iota(jnp.int32, (8, 2 * tq), 0)
    small = jnp.zeros((8, 2 * tq), F32)
    for first, terms in ((AUG_ONE, _split3(-slope * q_off)), (AUG_KEYPOS, _split3(slope))):
        for r, v in enumerate(terms):
            small = jnp.where(row8 == first + r, v.astype(F32), small)
    q_aug = jnp.concatenate(
        [q_cols * (QK_SCALE * LOG2E), small, choice_bias,
         jnp.zeros((LANES - 8 - n_blocks, 2 * tq), F32)], axis=0).astype(BF16)

    def probs(t, own):
        s = _dot(kaug_ref[_blk(qi - t), :], q_aug)
        if own:
            key = lax.broadcasted_iota(jnp.int32, s.shape, 0)
            qry = lax.broadcasted_iota(jnp.int32, s.shape, 1) & (tq - 1)
            s = jnp.where(key <= qry, s, NEG_BIG)
        top = jnp.max(s, axis=0, keepdims=True)
        p_s[t & 1] = jnp.exp2(s - top).astype(BF16)
        top_s[t & 1, 0:1, :] = top - slope * (t * MOBA_BLOCK).astype(F32)

    def gather(t):
        pv = _dot(vaug_ref[:, _blk(qi - t)], p_s[t & 1])
        top = top_s[t & 1, 0:1, :]
        m_run = m_s[...]
        m_new = jnp.maximum(m_run, top)
        acc_s[...] = acc_s[...] * jnp.exp2(m_run - m_new) + pv * jnp.exp2(top - m_new)
        m_s[...] = m_new

    m_s[...] = jnp.full(m_s.shape, NEG_BIG, F32)
    acc_s[...] = jnp.zeros_like(acc_s)
    probs(jnp.int32(0), True)

    def steady(t, carry):
        gather(t - 1)
        probs(t, False)
        return carry

    lax.fori_loop(1, qi + 1, steady, 0)
    gather(qi)
    acc = acc_s[...]
    out_t = acc[0:HEAD_PAIR_WIDTH] / acc[HEAD_PAIR_WIDTH:HEAD_PAIR_WIDTH + 1]
    o_ref[0] = _pair_merge_rows(out_t, tq)


def _moba_prompt(slopes, qt, k_rows, vt, kmean):
    bsz, _, seq = qt.shape
    n_blocks = seq // MOBA_BLOCK
    assert AUG_BLOCK + n_blocks <= LANES
    return pl.pallas_call(
        _moba_prompt_kernel,
        grid=_prompt_grid(seq, bsz),
        in_specs=[pl.BlockSpec(memory_space=pltpu.SMEM), _qt_spec(), _krows_spec(seq), _vt_spec(seq),
                  pl.BlockSpec((1, n_blocks, HEAD_PAIR_WIDTH), lambda b, h, i: (b, 0, h))],
        out_specs=_orow_spec(),
        out_shape=jax.ShapeDtypeStruct((bsz, seq, GROUP_WIDTH), F32),
        scratch_shapes=[pltpu.VMEM((seq, HEAD_PAIR_WIDTH + LANES), BF16),
                        pltpu.VMEM((HEAD_PAIR_WIDTH + ONES_ROWS, seq), BF16),
                        pltpu.VMEM((2, ATT_BLOCK, 2 * ATT_BLOCK), BF16),
                        pltpu.VMEM((2, 8, 2 * ATT_BLOCK), F32),
                        pltpu.VMEM((1, 2 * ATT_BLOCK), F32),
                        pltpu.VMEM((HEAD_PAIR_WIDTH + ONES_ROWS, 2 * ATT_BLOCK), F32)],

        compiler_params=pltpu.CompilerParams(dimension_semantics=("arbitrary",) * 3,
                                             vmem_limit_bytes=VMEM_LIMIT),
        name="moba_prompt",
    )(slopes, qt, k_rows, vt, kmean)


def _suffix_matrix(n):
    j = lax.broadcasted_iota(jnp.int32, (n, n), 0)
    s = lax.broadcasted_iota(jnp.int32, (n, n), 1)
    return jnp.where(j > s, 1.0, 0.0).astype(BF16)


def _suffix_sums(x, t2):
    hi, lo = _split(x)
    return _dot(jnp.concatenate([hi, lo], axis=1), t2)


def _rank_before_lanes(g, n_cand):
    lane = lax.broadcasted_iota(jnp.int32, g.shape, 1)
    cnt = jnp.zeros(g.shape, jnp.int32)
    for m in range(n_cand):
        gm = g[:, m:m + 1]
        beats = (gm > g) | ((gm == g) & (m < lane))
        cnt = cnt + jnp.where(beats, 1, 0)
    return cnt, lane


def _sample_kernel(pt_ref, slope_ref, qa_ref, qb_ref, kan_ref, van_ref, kbn_ref, vbn_ref, *rest):
    del pt_ref
    pps = PAGES_PER_STEP
    ka_refs, va_refs, kb_refs, vb_refs = (rest[i * pps:(i + 1) * pps] for i in range(4))
    (oa_ref, ob_ref, t2_ref, qa_s, qb_s, run_s, acca_s, gate_s, mx_s, l_s, accb_s,
     mo_s, lo_s, acco_s, new_s) = rest[4 * pps:]
    j = pl.program_id(1)
    n_steps = pl.num_programs(1)
    blk = MOBA_BLOCK
    bps = pps * NEW_PAD // blk
    n_cached = accb_s.shape[0]
    n_first = (n_steps - 1 - j) * bps
    n_tok = qa_ref.shape[1]
    n_rows = n_tok * N_HEADS

    row = lax.broadcasted_iota(jnp.int32, (n_rows, GROUP_WIDTH), 0)
    lane_w = lax.broadcasted_iota(jnp.int32, (n_rows, GROUP_WIDTH), 1)
    head_mask = (lane_w // HEAD_DIM) == (row % N_HEADS)
    slope = slope_ref[:, 0:1]
    lane = lax.broadcasted_iota(jnp.int32, (n_rows, LANES), 1)

    @pl.when(j == 0)
    def _():
        t = _suffix_matrix(blk)
        t2_ref[0:blk, :] = t
        t2_ref[blk:, :] = t

        def expand(q):
            rep = jnp.concatenate([jnp.broadcast_to(q[t:t + 1], (N_HEADS, GROUP_WIDTH))
                                   for t in range(n_tok)], axis=0)
            return jnp.where(head_mask, rep, 0.0)

        qa_s[...] = (expand(qa_ref[0]) * QK_SCALE).astype(BF16)
        qh, ql = _split(expand(qb_ref[0]))
        qb_s[...] = jnp.concatenate([qh, ql], axis=0)
        gate_s[...] = jnp.zeros_like(gate_s)
        mx_s[...] = jnp.zeros_like(mx_s)
        l_s[...] = jnp.zeros_like(l_s)

        def padded(ref):
            new_s[...] = jnp.zeros_like(new_s)
            new_s[0:n_tok, :] = ref[0]
            return new_s[...].astype(BF16)

        tok = (lax.broadcasted_iota(jnp.int32, (n_rows, NEW_PAD), 0) // N_HEADS)
        col = lax.broadcasted_iota(jnp.int32, (n_rows, NEW_PAD), 1)
        z = _dot_nt(qa_s[...], padded(kan_ref))
        strict = col < tok
        sp = _softplus(z)
        log_keep = jnp.where(strict, -sp, 0.0)
        hi, lo = _split(log_keep)
        tn = t2_ref[0:NEW_PAD, 0:NEW_PAD]
        between = _dot(hi, tn) + _dot(lo, tn)
        w = jnp.where(strict, jnp.exp((z - sp) + between), 0.0)
        run_s[...] = jnp.sum(log_keep, axis=1, keepdims=True)
        acca_s[...] = _dot(w.astype(BF16), padded(van_ref))
        s = _dot_nt(qb_s[0:n_rows, :], padded(kbn_ref)) * QK_SCALE
        s = s - slope * (tok - col).astype(F32)
        s = jnp.where(col <= tok, s, NEG_BIG)
        m_o = jnp.max(s, axis=1, keepdims=True)
        p = jnp.exp(s - m_o)
        mo_s[...] = m_o
        lo_s[...] = jnp.sum(p, axis=1, keepdims=True)
        acco_s[...] = _dot(p.astype(BF16), padded(vbn_ref))

    def pages(refs):
        return jnp.concatenate([r[0] for r in refs], axis=1)

    z = _dot(qa_s[...], pages(ka_refs).astype(BF16))
    sp = _softplus(z)
    log_keep = -sp
    t2 = t2_ref[...]
    run = run_s[...]
    betweens = [None] * bps
    for i in reversed(range(bps)):
        lk = log_keep[:, i * blk:(i + 1) * blk]
        betweens[i] = _suffix_sums(lk, t2) + run
        run = run + jnp.sum(lk, axis=1, keepdims=True)
    run_s[...] = run
    w = jnp.exp((z - sp) + jnp.concatenate(betweens, axis=1))
    acca_s[...] = acca_s[...] + _dot_nt(w.astype(BF16), pages(va_refs).astype(BF16))

    kb_h, kb_l = _split(pages(kb_refs))
    vb = pages(vb_refs).astype(BF16)
    raw = _dot(qb_s[...], kb_h)
    raw_hh = raw[0:n_rows]
    raw3 = raw_hh + (raw[n_rows:] + _dot(qb_s[0:n_rows, :], kb_l))
    tok = lax.broadcasted_iota(jnp.int32, (n_rows, blk), 0) // N_HEADS
    col = lax.broadcasted_iota(jnp.int32, (n_rows, blk), 1)
    for i in range(bps):
        n = n_first + i
        cols = slice(i * blk, (i + 1) * blk)
        gate_n = jnp.sum(raw3[:, cols], axis=1, keepdims=True) * (1.0 / blk)
        dist = ((n_cached - n) * blk + tok - col).astype(F32)
        s = raw_hh[:, cols] * QK_SCALE - slope * dist
        m_n = jnp.max(s, axis=1, keepdims=True)
        p = jnp.exp(s - m_n)
        l_n = jnp.sum(p, axis=1, keepdims=True)
        here = lane == n
        gate_s[...] = jnp.where(here, gate_n, gate_s[...])
        mx_s[...] = jnp.where(here, m_n, mx_s[...])
        l_s[...] = jnp.where(here, l_n, l_s[...])
        accb_s[n] = _dot_nt(p.astype(BF16), vb[:, cols])

    @pl.when(j == n_steps - 1)
    def _():
        def fold(acc):
            kept = jnp.where(head_mask, acc, 0.0)
            return jnp.sum(kept.reshape(n_tok, N_HEADS, GROUP_WIDTH), axis=1)

        oa_ref[0] = fold(acca_s[...])
        cnt, ln = _rank_before_lanes(gate_s[...], n_cached)
        sel = (ln < n_cached) & (cnt < MOBA_TOPK)
        mx = jnp.where(sel, mx_s[...], NEG_BIG)
        m_o = mo_s[...]
        m_all = jnp.maximum(m_o, jnp.max(mx, axis=1, keepdims=True))
        c = jnp.where(sel, jnp.exp(mx - m_all), 0.0)
        c_o = jnp.exp(m_o - m_all)
        l_tot = c_o * lo_s[...] + jnp.sum(c * l_s[...], axis=1, keepdims=True)
        acc = c_o * acco_s[...]
        for nb in range(n_cached):
            acc = acc + c[:, nb:nb + 1] * accb_s[nb]
        ob_ref[0] = fold(acc / l_tot)


def _sample_attention(page_table, slope_rows, qa, qb, ka_new, va_new, kb_new, vb_new,
                      ck_sb, cv_sb, ck_mb, cv_mb):
    dbs, n_tok, _ = qa.shape
    n_pages = page_table.shape[1]
    page = ck_sb.shape[2]
    pps = PAGES_PER_STEP
    assert page == NEW_PAD and n_pages % pps == 0 and (pps * page) % MOBA_BLOCK == 0
    n_steps = n_pages // pps
    n_cached = n_pages * page // MOBA_BLOCK
    assert n_cached <= LANES
    n_rows = n_tok * N_HEADS

    tok_spec = pl.BlockSpec((1, n_tok, GROUP_WIDTH), lambda b, j, pt: (b, 0, 0))

    def page_spec(which):
        return pl.BlockSpec((1, GROUP_WIDTH, page),
                            lambda b, j, pt: (pt[b, pps * (n_steps - 1 - j) + which], 0, 0))

    grid_spec = pltpu.PrefetchScalarGridSpec(
        num_scalar_prefetch=1,
        grid=(dbs, n_steps),
        in_specs=[pl.BlockSpec((n_rows, LANES), lambda b, j, pt: (0, 0))] + [tok_spec] * 6
                 + [page_spec(p) for p in range(pps)] * 4,
        out_specs=[tok_spec, tok_spec],
        scratch_shapes=[
            pltpu.VMEM((2 * MOBA_BLOCK, MOBA_BLOCK), BF16),
            pltpu.VMEM((n_rows, GROUP_WIDTH), BF16),
            pltpu.VMEM((2 * n_rows, GROUP_WIDTH), BF16),
            pltpu.VMEM((n_rows, 1), F32),
            pltpu.VMEM((n_rows, GROUP_WIDTH), F32),
            pltpu.VMEM((n_rows, LANES), F32),
            pltpu.VMEM((n_rows, LANES), F32),
            pltpu.VMEM((n_rows, LANES), F32),
            pltpu.VMEM((n_cached, n_rows, GROUP_WIDTH), F32),
            pltpu.VMEM((n_rows, 1), F32),
            pltpu.VMEM((n_rows, 1), F32),
            pltpu.VMEM((n_rows, GROUP_WIDTH), F32),
            pltpu.VMEM((NEW_PAD, GROUP_WIDTH), F32),
        ],
    )
    out = jax.ShapeDtypeStruct((dbs, n_tok, GROUP_WIDTH), F32)
    caches = [c for c in (ck_sb, cv_sb, ck_mb, cv_mb) for _ in range(pps)]
    return pl.pallas_call(
        _sample_kernel,
        grid_spec=grid_spec,
        out_shape=[out, out],
        compiler_params=pltpu.CompilerParams(dimension_semantics=("arbitrary", "arbitrary"),
                                             vmem_limit_bytes=VMEM_LIMIT),
        name="sample_attn",
    )(page_table, slope_rows, qa, qb, ka_new, va_new, kb_new, vb_new, *caches)


def _mix_out_kernel(alpha, x_ref, oa_ref, ga_ref, ob_ref, gb_ref, w_ref, gain_ref, bias_ref, y_ref):
    h = jnp.concatenate([oa_ref[...] * jax.nn.silu(ga_ref[...]),
                         ob_ref[...] * jax.nn.silu(gb_ref[...])], axis=-1)
    out = _dot(h.astype(BF16), w_ref[...])
    y = alpha * x_ref[...] + out
    mu = jnp.mean(y, axis=-1, keepdims=True)
    var = jnp.mean(jnp.square(y - mu), axis=-1, keepdims=True)
    y_ref[...] = (y - mu) * lax.rsqrt(var + LN_EPS) * gain_ref[...] + bias_ref[...]


def _mix_out(x2d, oa, ga, ob, gb, w_out, gain, bias, alpha, block_rows):
    rows, d_model = x2d.shape
    g_spec = pl.BlockSpec((block_rows, GROUP_WIDTH), lambda i: (i, 0))
    x_spec = pl.BlockSpec((block_rows, d_model), lambda i: (i, 0))
    return pl.pallas_call(
        functools.partial(_mix_out_kernel, alpha),
        grid=(rows // block_rows,),
        in_specs=[x_spec, g_spec, g_spec, g_spec, g_spec,
                  _full_spec(w_out), _full_spec(gain), _full_spec(bias)],
        out_specs=x_spec,
        out_shape=jax.ShapeDtypeStruct(x2d.shape, F32),
        compiler_params=pltpu.CompilerParams(dimension_semantics=("arbitrary",),
                                             vmem_limit_bytes=VMEM_LIMIT),
        name="mix_out",
    )(x2d, oa, ga, ob, gb, w_out, gain, bias)


def kernel(x_prompt, x_sample, cache_k_sb, cache_v_sb, cache_k_moba, cache_v_moba, page_table,
           w_in, w_out, ln_gain, ln_bias):
    depth = w_in.shape[0]
    assert depth == 1, "single-layer trunk"
    bsz, seq, d_model = x_prompt.shape
    dbs, n_tok, _ = x_sample.shape
    alpha = (2.0 * depth) ** 0.25
    slopes = jnp.asarray([2.0 ** (-8.0 * (i + 1) / N_HEADS) for i in range(N_HEADS)], dtype=F32)
    slope_rows = jnp.broadcast_to(jnp.tile(slopes, n_tok)[:, None], (n_tok * N_HEADS, LANES))

    g = GROUP_WIDTH
    w = w_in[0]
    w_hi = w.astype(BF16)
    w_lo = (w[:, 4 * g:6 * g] - w_hi[:, 4 * g:6 * g].astype(F32)).astype(BF16)

    def col(a, c):
        return a[:, c * g:(c + 1) * g]

    w_row = jnp.concatenate([col(w_hi, c) for c in ROW_GROUPS], axis=1)
    w_row_lo = col(w_lo, 1)
    w_feat = jnp.concatenate([col(w_hi, c) for c in FEAT_GROUPS], axis=1).T
    w_feat_lo = w_lo.T
    w_o = w_out[0].astype(BF16)
    gain = ln_gain[0][None, :]
    bias = ln_bias[0][None, :]

    def pages(c):
        return jnp.transpose(c[0], (0, 2, 3, 1)).reshape(c.shape[1], GROUP_WIDTH, c.shape[2])

    def heads_from_features(a):
        return jnp.transpose(a.reshape(bsz, N_HEADS, HEAD_DIM, seq), (0, 3, 1, 2))[None]

    def heads_from_rows(a):
        return a.reshape(1, dbs, n_tok, N_HEADS, HEAD_DIM)

    xp = x_prompt.reshape(bsz * seq, d_model)
    (ga, gb, ka_rows, kb_rows, kmean, qat, qbt, kat, vat, kbt, vbt) = _project_prompt(
        xp, w_row, w_row_lo, w_feat, w_feat_lo, bsz, 512)
    shp = (bsz, seq, GROUP_WIDTH)
    o_a = _sb_prompt(qat, ka_rows.reshape(shp), vat)
    o_b = _moba_prompt(slopes, qbt, kb_rows.reshape(shp), vbt, kmean.reshape(bsz, seq // MOBA_BLOCK, GROUP_WIDTH))
    y_p = _mix_out(xp, o_a.reshape(xp.shape[0], GROUP_WIDTH), ga, o_b.reshape(xp.shape[0], GROUP_WIDTH), gb,
                   w_o, gain, bias, alpha, 512)

    xs = x_sample.reshape(dbs * n_tok, d_model)
    sqa, ska, sva, sga, sqb, skb, svb, sgb = _project_rows(xs, w_hi, w_lo)
    sshp = (dbs, n_tok, GROUP_WIDTH)
    so_a, so_b = _sample_attention(
        page_table, slope_rows, sqa.reshape(sshp), sqb.reshape(sshp),
        ska.reshape(sshp), sva.reshape(sshp), skb.reshape(sshp), svb.reshape(sshp),
        pages(cache_k_sb), pages(cache_v_sb), pages(cache_k_moba), pages(cache_v_moba))
    y_s = _mix_out(xs, so_a.reshape(xs.shape[0], GROUP_WIDTH), sga, so_b.reshape(xs.shape[0], GROUP_WIDTH), sgb,
                   w_o, gain, bias, alpha, dbs * n_tok)

    return (y_p.reshape(x_prompt.shape), y_s.reshape(x_sample.shape),
            heads_from_features(kat), heads_from_features(vat),
            heads_from_features(kbt), heads_from_features(vbt),
            heads_from_rows(ska), heads_from_rows(sva), heads_from_rows(skb), heads_from_rows(svb))
```

```python
import functools
import math

import jax
import jax.numpy as jnp
from jax import lax
from jax.experimental import pallas as pl
from jax.experimental.pallas import tpu as pltpu

HEAD_DIM = 64
N_HEADS = 8
GROUP_WIDTH = N_HEADS * HEAD_DIM
HEAD_PAIR_WIDTH = 2 * HEAD_DIM
LANES = 128
MOBA_BLOCK = 256
MOBA_TOPK = 3
ATT_BLOCK = 256
QK_SCALE = HEAD_DIM ** -0.5
LOG2E = math.log2(math.e)
LN_EPS = 1e-5
NEG_BIG = -1e30
SOFTPLUS2_CLAMP = 100.0
RUN_DEAD = -160.0
RUN_DEAD_LN = -112.0
NEW_PAD = 128
PAGES_PER_STEP = 4
ONES_ROWS = 16
VMEM_LIMIT = 48 * 1024 * 1024

AUG_ONE = 0
AUG_KEYPOS = 3
AUG_BLOCK = 8

F32 = jnp.float32
BF16 = jnp.bfloat16


def _dot(a, b):
    return jnp.dot(a, b, preferred_element_type=F32)


def _dot_nt(a, b):
    return lax.dot_general(a, b, (((1,), (1,)), ((), ())), preferred_element_type=F32)


def _split(x):
    hi = x.astype(BF16)
    lo = (x - hi.astype(F32)).astype(BF16)
    return hi, lo


def _split3(x):
    hi = x.astype(BF16)
    r = x - hi.astype(F32)
    mid = r.astype(BF16)
    lo = (r - mid.astype(F32)).astype(BF16)
    return hi, mid, lo


def _dot_3pass(a, b):
    ah, al = _split(a)
    bh, bl = _split(b)
    return _dot(ah, bh) + (_dot(ah, bl) + _dot(al, bh))


def _softplus(z):
    return jnp.maximum(z, 0.0) + jnp.log(1.0 + jnp.exp(-jnp.abs(z)))


def _softplus2(z):
    return jnp.maximum(jnp.log2(1.0 + jnp.exp2(jnp.minimum(z, SOFTPLUS2_CLAMP))), z)


def _full_spec(a, single_buffer=False):
    mode = pl.Buffered(1) if single_buffer else None
    return pl.BlockSpec(a.shape, lambda *_: (0,) * a.ndim, pipeline_mode=mode)


def _proj_rows_kernel(x_ref, w_ref, wlo_ref, *out_refs):
    xh, xl = _split(x_ref[...])
    g = GROUP_WIDTH
    for c, o_ref in enumerate(out_refs):
        wc = w_ref[:, c * g:(c + 1) * g]
        out = _dot(xh, wc)
        if c in (4, 5):
            out = out + (_dot(xh, wlo_ref[:, (c - 4) * g:(c - 3) * g]) + _dot(xl, wc))
        o_ref[...] = out


def _project_rows(x2d, w_hi, w_lo):
    rows, _ = x2d.shape
    out = jax.ShapeDtypeStruct((rows, GROUP_WIDTH), F32)
    return pl.pallas_call(
        _proj_rows_kernel,
        grid=(1,),
        in_specs=[_full_spec(x2d), _full_spec(w_hi), _full_spec(w_lo)],
        out_specs=[pl.BlockSpec((rows, GROUP_WIDTH), lambda i: (0, 0))] * 8,
        out_shape=[out] * 8,
        compiler_params=pltpu.CompilerParams(dimension_semantics=("arbitrary",),
                                             vmem_limit_bytes=VMEM_LIMIT),
        name="proj_rows",
    )(x2d, w_hi, w_lo)


ROW_GROUPS = (3, 7, 1, 5)
FEAT_GROUPS = (0, 4, 1, 2, 5, 6)
FEAT_3PASS = (1, 4)


def _proj_prompt_kernel(x_ref, wr_ref, wrlo_ref, wf_ref, wflo_ref,
                        ga_ref, gb_ref, kar_ref, kbr_ref, kmean_ref,
                        qat_ref, qbt_ref, kat_ref, vat_ref, kbt_ref, vbt_ref):
    x = x_ref[...]
    xh, xl = _split(x)
    g = GROUP_WIDTH
    for c, o_ref in enumerate((ga_ref, gb_ref, kar_ref, kbr_ref)):
        o_ref[...] = _dot(xh, wr_ref[:, c * g:(c + 1) * g]).astype(o_ref.dtype)
    lo_slot = 0
    for c, o_ref in enumerate((qat_ref, qbt_ref, kat_ref, vat_ref, kbt_ref, vbt_ref)):
        wc = wf_ref[c * g:(c + 1) * g, :]
        out = _dot_nt(wc, xh)
        if c in FEAT_3PASS:
            out = out + (_dot_nt(wflo_ref[lo_slot * g:(lo_slot + 1) * g, :], xh) + _dot_nt(wc, xl))
            lo_slot += 1
        o_ref[0] = out
    n_blk = x.shape[0] // MOBA_BLOCK
    row8 = lax.broadcasted_iota(jnp.int32, (8, x.shape[1]), 0)
    xbar = jnp.zeros((8, x.shape[1]), F32)
    for i in range(n_blk):
        mean_i = jnp.sum(x[i * MOBA_BLOCK:(i + 1) * MOBA_BLOCK], axis=0, keepdims=True) * (1.0 / MOBA_BLOCK)
        xbar = jnp.where(row8 == i, mean_i, xbar)
    bh, bl = _split(xbar)
    wk = wr_ref[:, 3 * g:4 * g]
    km = _dot(bh, wk) + (_dot(bh, wrlo_ref[...]) + _dot(bl, wk))
    kmean_ref[0] = km[0:n_blk]


def _project_prompt(x2d, w_row, w_row_lo, w_feat, w_feat_lo, batch, block_rows):
    rows, d_model = x2d.shape
    seq = rows // batch
    per_seq = seq // block_rows
    n_blk = block_rows // MOBA_BLOCK
    row_spec = pl.BlockSpec((block_rows, GROUP_WIDTH), lambda i: (i, 0))
    feat_spec = pl.BlockSpec((1, GROUP_WIDTH, block_rows), lambda i: (i // per_seq, 0, i % per_seq))
    row_f32 = jax.ShapeDtypeStruct((rows, GROUP_WIDTH), F32)
    row_bf16 = jax.ShapeDtypeStruct((rows, GROUP_WIDTH), BF16)
    feat = jax.ShapeDtypeStruct((batch, GROUP_WIDTH, seq), F32)
    kmean = jax.ShapeDtypeStruct((rows // block_rows, n_blk, GROUP_WIDTH), F32)
    return pl.pallas_call(
        _proj_prompt_kernel,
        grid=(rows // block_rows,),
        in_specs=[pl.BlockSpec((block_rows, d_model), lambda i: (i, 0)),
                  _full_spec(w_row, True), _full_spec(w_row_lo, True),
                  _full_spec(w_feat, True), _full_spec(w_feat_lo, True)],
        out_specs=[row_spec] * 4 + [pl.BlockSpec((1, n_blk, GROUP_WIDTH), lambda i: (i, 0, 0))] + [feat_spec] * 6,
        out_shape=[row_f32, row_f32, row_bf16, row_bf16, kmean] + [feat] * 6,
        compiler_params=pltpu.CompilerParams(dimension_semantics=("arbitrary",),
                                             vmem_limit_bytes=VMEM_LIMIT),
        name="proj_prompt",
    )(x2d, w_row, w_row_lo, w_feat, w_feat_lo)


def _pair_columns(qt, tq):
    row = lax.broadcasted_iota(jnp.int32, qt.shape, 0)
    zero = jnp.zeros_like(qt)
    return jnp.concatenate([jnp.where(row < HEAD_DIM, qt, zero),
                            jnp.where(row >= HEAD_DIM, qt, zero)], axis=1)


def _pair_merge_rows(acc_t, tq):
    row = lax.broadcasted_iota(jnp.int32, (HEAD_PAIR_WIDTH, tq), 0)
    return jnp.where(row < HEAD_DIM, acc_t[:, :tq], acc_t[:, tq:]).T


def _blk(kj):
    return pl.ds(pl.multiple_of(kj * ATT_BLOCK, ATT_BLOCK), ATT_BLOCK)


def _prompt_grid(seq, bsz):
    return (bsz, GROUP_WIDTH // HEAD_PAIR_WIDTH, seq // ATT_BLOCK)


def _qt_spec():
    return pl.BlockSpec((1, HEAD_PAIR_WIDTH, ATT_BLOCK), lambda b, h, i: (b, h, i))


def _krows_spec(seq):
    return pl.BlockSpec((1, seq, HEAD_PAIR_WIDTH), lambda b, h, i: (b, 0, h))


def _vt_spec(seq):
    return pl.BlockSpec((1, HEAD_PAIR_WIDTH, seq), lambda b, h, i: (b, h, 0))


def _orow_spec():
    return pl.BlockSpec((1, ATT_BLOCK, HEAD_PAIR_WIDTH), lambda b, h, i: (b, i, h))


def _sb_prompt_kernel(qt_ref, k_ref, vt_ref, o_ref, u_ref, vtb_ref, run_s, acc_s):
    b, hp, qi = pl.program_id(0), pl.program_id(1), pl.program_id(2)
    tq = ATT_BLOCK
    half = tq // 2
    u_rows = half + ONES_ROWS

    @pl.when((b == 0) & (hp == 0) & (qi == 0))
    def _():
        s_i = lax.broadcasted_iota(jnp.int32, (u_rows, tq), 0)
        j_i = lax.broadcasted_iota(jnp.int32, (u_rows, tq), 1) & (half - 1)
        u_ref[...] = jnp.where((j_i >= s_i) | (s_i >= half), -1.0, 0.0).astype(BF16)

    @pl.when(qi == 0)
    def _():
        vtb_ref[...] = vt_ref[0].astype(BF16)

    q_cols = _pair_columns((qt_ref[0] * (QK_SCALE * LOG2E)).astype(BF16), tq)

    def scores(t, diagonal):
        z = _dot(k_ref[0, _blk(qi - t), :], q_cols)
        sp = _softplus2(z)
        if diagonal:
            key = lax.broadcasted_iota(jnp.int32, z.shape, 0)
            qry = lax.broadcasted_iota(jnp.int32, z.shape, 1) & (tq - 1)
            strict = key < qry
            sp = jnp.where(strict, sp, 0.0)
            z = jnp.where(strict, z, NEG_BIG)
        hi, lo = _split(sp)
        halves = [jnp.concatenate([hi[h * half:(h + 1) * half], lo[h * half:(h + 1) * half]], axis=0)
                  for h in range(2)]
        return z, halves

    def suffix(halves):
        neg_u = u_ref[...]
        return [_dot(neg_u, hl) for hl in halves]

    def weigh(t, z, locs):
        run = run_s[...]
        tot_first = locs[0][half:half + 1]
        tot_second = locs[1][half:half + 1]
        first = (z[0:half] + locs[0][0:half]) + (run + tot_second)
        second = (z[half:tq] + locs[1][0:half]) + run
        w = jnp.exp2(jnp.concatenate([first, second], axis=0)).astype(BF16)
        run_s[...] = run + (tot_first + tot_second)
        acc_s[...] = acc_s[...] + _dot(vtb_ref[:, _blk(qi - t)], w)

    def single(t, diagonal):
        z, halves = scores(t, diagonal)
        weigh(t, z, suffix(halves))

    def pair(t, diagonal):
        z0, h0 = scores(t, diagonal)
        z1, h1 = scores(t + 1, False)
        l0 = suffix(h0)
        l1 = suffix(h1)
        weigh(t, z0, l0)
        weigh(t + 1, z1, l1)

    def alive():
        return jnp.max(run_s[...]) > RUN_DEAD

    run_s[...] = jnp.zeros_like(run_s)
    acc_s[...] = jnp.zeros_like(acc_s)

    @pl.when(qi == 0)
    def _():
        single(0, True)

    @pl.when(qi >= 1)
    def _():
        pair(0, True)

        def more_pairs(c):
            return (c[0] + 1 <= qi) & c[1]

        def next_pair(c):
            pair(c[0], False)
            return c[0] + 2, alive()

        t, live = lax.while_loop(more_pairs, next_pair, (jnp.int32(2), alive()))

        @pl.when((t == qi) & live)
        def _():
            single(t, False)

    o_ref[0] = _pair_merge_rows(acc_s[...], tq)


def _sb_prompt(qt, k_rows, vt):
    bsz, _, seq = qt.shape
    tq = ATT_BLOCK
    return pl.pallas_call(
        _sb_prompt_kernel,
        grid=_prompt_grid(seq, bsz),
        in_specs=[_qt_spec(), _krows_spec(seq), _vt_spec(seq)],
        out_specs=_orow_spec(),
        out_shape=jax.ShapeDtypeStruct((bsz, seq, GROUP_WIDTH), F32),
        scratch_shapes=[pltpu.VMEM((tq // 2 + ONES_ROWS, tq), BF16),
                        pltpu.VMEM((HEAD_PAIR_WIDTH, seq), BF16),
                        pltpu.VMEM((1, 2 * tq), F32),
                        pltpu.VMEM((HEAD_PAIR_WIDTH, 2 * tq), F32)],
        compiler_params=pltpu.CompilerParams(dimension_semantics=("arbitrary",) * 3,
                                             vmem_limit_bytes=VMEM_LIMIT),
        name="sb_prompt",
    )(qt, k_rows, vt)


def _rank_before_rows(g, n_valid, n_cand):
    blk = lax.broadcasted_iota(jnp.int32, g.shape, 0)
    cnt = jnp.zeros(g.shape, jnp.int32)
    for m in range(n_cand):
        gm = g[m:m + 1, :]
        beats = (gm > g) | ((gm == g) & (m < blk))
        cnt = cnt + jnp.where(beats & (m < n_valid), 1, 0)
    return cnt, blk


def _moba_prompt_kernel(slopes_ref, qt_ref, k_ref, vt_ref, kmean_ref, o_ref,
                        kaug_ref, vaug_ref, p_s, top_s, m_s, acc_s):
    hp, qi = pl.program_id(1), pl.program_id(2)
    tq = ATT_BLOCK
    seq = k_ref.shape[1]
    n_blocks = seq // MOBA_BLOCK

    @pl.when(qi == 0)
    def _():
        kaug_ref[:, 0:HEAD_PAIR_WIDTH] = k_ref[0]
        lane = lax.broadcasted_iota(jnp.int32, (MOBA_BLOCK, LANES), 1)
        key = lax.broadcasted_iota(jnp.int32, (MOBA_BLOCK, LANES), 0).astype(F32)
        base = jnp.where(lane < AUG_ONE + 3, 1.0,
                         jnp.where(lane < AUG_KEYPOS + 3, key, 0.0))
        for n in range(n_blocks):
            kaug_ref[n * MOBA_BLOCK:(n + 1) * MOBA_BLOCK, HEAD_PAIR_WIDTH:] = (
                jnp.where(lane == AUG_BLOCK + n, 1.0, base).astype(BF16))
        vaug_ref[0:HEAD_PAIR_WIDTH, :] = vt_ref[0].astype(BF16)
        vaug_ref[HEAD_PAIR_WIDTH:, :] = jnp.ones((ONES_ROWS, seq), BF16)

    qt = qt_ref[0]
    q_cols = _pair_columns(qt, tq)
    gate = _dot_3pass(kmean_ref[0], q_cols)
    cnt, blk = _rank_before_rows(gate, qi, n_blocks)
    chosen = ((blk < qi) & (cnt < MOBA_TOPK)) | (blk == qi)
    choice_bias = jnp.where(chosen, 0.0, NEG_BIG)

    lane = lax.broadcasted_iota(jnp.int32, (1, 2 * tq), 1)
    slope = jnp.where(lane < tq, slopes_ref[2 * hp], slopes_ref[2 * hp + 1]) * LOG2E
    q_off = (lane & (tq - 1)).astype(F32)
    row8 = lax.broadcasted_iota(jnp.int32, (8, 2 * tq), 0)
    small = jnp.zeros((8, 2 * tq), F32)
    for first, terms in ((AUG_ONE, _split3(-slope * q_off)), (AUG_KEYPOS, _split3(slope))):
        for r, v in enumerate(terms):
            small = jnp.where(row8 == first + r, v.astype(F32), small)
    q_aug = jnp.concatenate(
        [q_cols * (QK_SCALE * LOG2E), small, choice_bias,
         jnp.zeros((LANES - 8 - n_blocks, 2 * tq), F32)], axis=0).astype(BF16)

    def probs(t, own):
        s = _dot(kaug_ref[_blk(qi - t), :], q_aug)
        if own:
            key = lax.broadcasted_iota(jnp.int32, s.shape, 0)
            qry = lax.broadcasted_iota(jnp.int32, s.shape, 1) & (tq - 1)
            s = jnp.where(key <= qry, s, NEG_BIG)
        top = jnp.max(s, axis=0, keepdims=True)
        p_s[t & 1] = jnp.exp2(s - top).astype(BF16)
        top_s[t & 1, 0:1, :] = top - slope * (t * MOBA_BLOCK).astype(F32)

    def gather(t):
        pv = _dot(vaug_ref[:, _blk(qi - t)], p_s[t & 1])
        top = top_s[t & 1, 0:1, :]
        m_run = m_s[...]
        m_new = jnp.maximum(m_run, top)
        acc_s[...] = acc_s[...] * jnp.exp2(m_run - m_new) + pv * jnp.exp2(top - m_new)
        m_s[...] = m_new

    m_s[...] = jnp.full(m_s.shape, NEG_BIG, F32)
    acc_s[...] = jnp.zeros_like(acc_s)
    probs(jnp.int32(0), True)

    def steady(t, carry):
        gather(t - 1)
        probs(t, False)
        return carry

    lax.fori_loop(1, qi + 1, steady, 0)
    gather(qi)
    acc = acc_s[...]
    out_t = acc[0:HEAD_PAIR_WIDTH] / acc[HEAD_PAIR_WIDTH:HEAD_PAIR_WIDTH + 1]
    o_ref[0] = _pair_merge_rows(out_t, tq)


def _moba_prompt(slopes, qt, k_rows, vt, kmean):
    bsz, _, seq = qt.shape
    n_blocks = seq // MOBA_BLOCK
    assert AUG_BLOCK + n_blocks <= LANES
    return pl.pallas_call(
        _moba_prompt_kernel,
        grid=_prompt_grid(seq, bsz),
        in_specs=[pl.BlockSpec(memory_space=pltpu.SMEM), _qt_spec(), _krows_spec(seq), _vt_spec(seq),
                  pl.BlockSpec((1, n_blocks, HEAD_PAIR_WIDTH), lambda b, h, i: (b, 0, h))],
        out_specs=_orow_spec(),
        out_shape=jax.ShapeDtypeStruct((bsz, seq, GROUP_WIDTH), F32),
        scratch_shapes=[pltpu.VMEM((seq, HEAD_PAIR_WIDTH + LANES), BF16),
                        pltpu.VMEM((HEAD_PAIR_WIDTH + ONES_ROWS, seq), BF16),
                        pltpu.VMEM((2, ATT_BLOCK, 2 * ATT_BLOCK), BF16),
                        pltpu.VMEM((2, 8, 2 * ATT_BLOCK), F32),
                        pltpu.VMEM((1, 2 * ATT_BLOCK), F32),
                        pltpu.VMEM((HEAD_PAIR_WIDTH + ONES_ROWS, 2 * ATT_BLOCK), F32)],

        compiler_params=pltpu.CompilerParams(dimension_semantics=("arbitrary",) * 3,
                                             vmem_limit_bytes=VMEM_LIMIT),
        name="moba_prompt",
    )(slopes, qt, k_rows, vt, kmean)


def _suffix_matrix(n):
    j = lax.broadcasted_iota(jnp.int32, (n, n), 0)
    s = lax.broadcasted_iota(jnp.int32, (n, n), 1)
    return jnp.where(j > s, 1.0, 0.0).astype(BF16)


def _suffix_sums(x, t2):
    hi, lo = _split(x)
    return _dot(jnp.concatenate([hi, lo], axis=1), t2)


def _rank_before_lanes(g, n_cand):
    lane = lax.broadcasted_iota(jnp.int32, g.shape, 1)
    cnt = jnp.zeros(g.shape, jnp.int32)
    for m in range(n_cand):
        gm = g[:, m:m + 1]
        beats = (gm > g) | ((gm == g) & (m < lane))
        cnt = cnt + jnp.where(beats, 1, 0)
    return cnt, lane


def _sample_kernel(pt_ref, slope_ref, qa_ref, qb_ref, kan_ref, van_ref, kbn_ref, vbn_ref, *rest):
    del pt_ref
    pps = PAGES_PER_STEP
    ka_refs, va_refs, kb_refs, vb_refs = (rest[i * pps:(i + 1) * pps] for i in range(4))
    (oa_ref, ob_ref, t2_ref, qa_s, qb_s, run_s, acca_s, gate_s, mx_s, l_s, accb_s,
     mo_s, lo_s, acco_s, new_s, live_s) = rest[4 * pps:]
    j = pl.program_id(1)
    n_steps = pl.num_programs(1)
    blk = MOBA_BLOCK
    bps = pps * NEW_PAD // blk
    n_cached = accb_s.shape[0]
    n_first = (n_steps - 1 - j) * bps
    n_tok = qa_ref.shape[1]
    n_rows = n_tok * N_HEADS

    row = lax.broadcasted_iota(jnp.int32, (n_rows, GROUP_WIDTH), 0)
    lane_w = lax.broadcasted_iota(jnp.int32, (n_rows, GROUP_WIDTH), 1)
    head_mask = (lane_w // HEAD_DIM) == (row % N_HEADS)
    slope = slope_ref[:, 0:1]
    lane = lax.broadcasted_iota(jnp.int32, (n_rows, LANES), 1)

    @pl.when(j == 0)
    def _():
        t = _suffix_matrix(blk)
        t2_ref[0:blk, :] = t
        t2_ref[blk:, :] = t
        live_s[0] = 1

        def expand(q):
            rep = jnp.concatenate([jnp.broadcast_to(q[t:t + 1], (N_HEADS, GROUP_WIDTH))
                                   for t in range(n_tok)], axis=0)
            return jnp.where(head_mask, rep, 0.0)

        qa_s[...] = (expand(qa_ref[0]) * QK_SCALE).astype(BF16)
        qh, ql = _split(expand(qb_ref[0]))
        qb_s[...] = jnp.concatenate([qh, ql], axis=0)
        gate_s[...] = jnp.zeros_like(gate_s)
        mx_s[...] = jnp.zeros_like(mx_s)
        l_s[...] = jnp.zeros_like(l_s)

        def padded(ref):
            new_s[...] = jnp.zeros_like(new_s)
            new_s[0:n_tok, :] = ref[0]
            return new_s[...].astype(BF16)

        tok = (lax.broadcasted_iota(jnp.int32, (n_rows, NEW_PAD), 0) // N_HEADS)
        col = lax.broadcasted_iota(jnp.int32, (n_rows, NEW_PAD), 1)
        z = _dot_nt(qa_s[...], padded(kan_ref))
        strict = col < tok
        sp = _softplus(z)
        log_keep = jnp.where(strict, -sp, 0.0)
        hi, lo = _split(log_keep)
        tn = t2_ref[0:NEW_PAD, 0:NEW_PAD]
        between = _dot(hi, tn) + _dot(lo, tn)
        w = jnp.where(strict, jnp.exp((z - sp) + between), 0.0)
        run_s[...] = jnp.sum(log_keep, axis=1, keepdims=True)
        acca_s[...] = _dot(w.astype(BF16), padded(van_ref))
        s = _dot_nt(qb_s[0:n_rows, :], padded(kbn_ref)) * QK_SCALE
        s = s - slope * (tok - col).astype(F32)
        s = jnp.where(col <= tok, s, NEG_BIG)
        m_o = jnp.max(s, axis=1, keepdims=True)
        p = jnp.exp(s - m_o)
        mo_s[...] = m_o
        lo_s[...] = jnp.sum(p, axis=1, keepdims=True)
        acco_s[...] = _dot(p.astype(BF16), padded(vbn_ref))

    def pages(refs):
        return jnp.concatenate([r[0] for r in refs], axis=1)

    @pl.when(live_s[0] == 1)
    def _():
        z = _dot(qa_s[...], pages(ka_refs).astype(BF16))
        sp = _softplus(z)
        log_keep = -sp
        t2 = t2_ref[...]
        run = run_s[...]
        betweens = [None] * bps
        for i in reversed(range(bps)):
            lk = log_keep[:, i * blk:(i + 1) * blk]
            betweens[i] = _suffix_sums(lk, t2) + run
            run = run + jnp.sum(lk, axis=1, keepdims=True)
        run_s[...] = run
        live_s[0] = (jnp.max(run) > RUN_DEAD_LN).astype(jnp.int32)
        w = jnp.exp((z - sp) + jnp.concatenate(betweens, axis=1))
        acca_s[...] = acca_s[...] + _dot_nt(w.astype(BF16), pages(va_refs).astype(BF16))

    kb_h, kb_l = _split(pages(kb_refs))
    vb = pages(vb_refs).astype(BF16)
    raw = _dot(qb_s[...], kb_h)
    raw_hh = raw[0:n_rows]
    raw3 = raw_hh + (raw[n_rows:] + _dot(qb_s[0:n_rows, :], kb_l))
    tok = lax.broadcasted_iota(jnp.int32, (n_rows, blk), 0) // N_HEADS
    col = lax.broadcasted_iota(jnp.int32, (n_rows, blk), 1)
    for i in range(bps):
        n = n_first + i
        cols = slice(i * blk, (i + 1) * blk)
        gate_n = jnp.sum(raw3[:, cols], axis=1, keepdims=True) * (1.0 / blk)
        dist = ((n_cached - n) * blk + tok - col).astype(F32)
        s = raw_hh[:, cols] * QK_SCALE - slope * dist
        m_n = jnp.max(s, axis=1, keepdims=True)
        p = jnp.exp(s - m_n)
        l_n = jnp.sum(p, axis=1, keepdims=True)
        here = lane == n
        gate_s[...] = jnp.where(here, gate_n, gate_s[...])
        mx_s[...] = jnp.where(here, m_n, mx_s[...])
        l_s[...] = jnp.where(here, l_n, l_s[...])
        accb_s[n] = _dot_nt(p.astype(BF16), vb[:, cols])

    @pl.when(j == n_steps - 1)
    def _():
        def fold(acc):
            kept = jnp.where(head_mask, acc, 0.0)
            return jnp.sum(kept.reshape(n_tok, N_HEADS, GROUP_WIDTH), axis=1)

        oa_ref[0] = fold(acca_s[...])
        cnt, ln = _rank_before_lanes(gate_s[...], n_cached)
        sel = (ln < n_cached) & (cnt < MOBA_TOPK)
        mx = jnp.where(sel, mx_s[...], NEG_BIG)
        m_o = mo_s[...]
        m_all = jnp.maximum(m_o, jnp.max(mx, axis=1, keepdims=True))
        c = jnp.where(sel, jnp.exp(mx - m_all), 0.0)
        c_o = jnp.exp(m_o - m_all)
        l_tot = c_o * lo_s[...] + jnp.sum(c * l_s[...], axis=1, keepdims=True)
        acc = c_o * acco_s[...]
        for nb in range(n_cached):
            acc = acc + c[:, nb:nb + 1] * accb_s[nb]
        ob_ref[0] = fold(acc / l_tot)


def _sample_attention(page_table, slope_rows, qa, qb, ka_new, va_new, kb_new, vb_new,
                      ck_sb, cv_sb, ck_mb, cv_mb):
    dbs, n_tok, _ = qa.shape
    n_pages = page_table.shape[1]
    page = ck_sb.shape[2]
    pps = PAGES_PER_STEP
    assert page == NEW_PAD and n_pages % pps == 0 and (pps * page) % MOBA_BLOCK == 0
    n_steps = n_pages // pps
    n_cached = n_pages * page // MOBA_BLOCK
    assert n_cached <= LANES
    n_rows = n_tok * N_HEADS

    tok_spec = pl.BlockSpec((1, n_tok, GROUP_WIDTH), lambda b, j, pt: (b, 0, 0))

    def page_spec(which):
        return pl.BlockSpec((1, GROUP_WIDTH, page),
                            lambda b, j, pt: (pt[b, pps * (n_steps - 1 - j) + which], 0, 0))

    grid_spec = pltpu.PrefetchScalarGridSpec(
        num_scalar_prefetch=1,
        grid=(dbs, n_steps),
        in_specs=[pl.BlockSpec((n_rows, LANES), lambda b, j, pt: (0, 0))] + [tok_spec] * 6
                 + [page_spec(p) for p in range(pps)] * 4,
        out_specs=[tok_spec, tok_spec],
        scratch_shapes=[
            pltpu.VMEM((2 * MOBA_BLOCK, MOBA_BLOCK), BF16),
            pltpu.VMEM((n_rows, GROUP_WIDTH), BF16),
            pltpu.VMEM((2 * n_rows, GROUP_WIDTH), BF16),
            pltpu.VMEM((n_rows, 1), F32),
            pltpu.VMEM((n_rows, GROUP_WIDTH), F32),
            pltpu.VMEM((n_rows, LANES), F32),
            pltpu.VMEM((n_rows, LANES), F32),
            pltpu.VMEM((n_rows, LANES), F32),
            pltpu.VMEM((n_cached, n_rows, GROUP_WIDTH), F32),
            pltpu.VMEM((n_rows, 1), F32),
            pltpu.VMEM((n_rows, 1), F32),
            pltpu.VMEM((n_rows, GROUP_WIDTH), F32),
            pltpu.VMEM((NEW_PAD, GROUP_WIDTH), F32),
            pltpu.SMEM((1,), jnp.int32),
        ],
    )
    out = jax.ShapeDtypeStruct((dbs, n_tok, GROUP_WIDTH), F32)
    caches = [c for c in (ck_sb, cv_sb, ck_mb, cv_mb) for _ in range(pps)]
    return pl.pallas_call(
        _sample_kernel,
        grid_spec=grid_spec,
        out_shape=[out, out],
        compiler_params=pltpu.CompilerParams(dimension_semantics=("arbitrary", "arbitrary"),
                                             vmem_limit_bytes=VMEM_LIMIT),
        name="sample_attn",
    )(page_table, slope_rows, qa, qb, ka_new, va_new, kb_new, vb_new, *caches)


def _mix_out_kernel(alpha, x_ref, oa_ref, ga_ref, ob_ref, gb_ref, w_ref, gain_ref, bias_ref, y_ref):
    h = jnp.concatenate([oa_ref[...] * jax.nn.silu(ga_ref[...]),
                         ob_ref[...] * jax.nn.silu(gb_ref[...])], axis=-1)
    out = _dot(h.astype(BF16), w_ref[...])
    y = alpha * x_ref[...] + out
    mu = jnp.mean(y, axis=-1, keepdims=True)
    var = jnp.mean(jnp.square(y - mu), axis=-1, keepdims=True)
    y_ref[...] = (y - mu) * lax.rsqrt(var + LN_EPS) * gain_ref[...] + bias_ref[...]


def _mix_out(x2d, oa, ga, ob, gb, w_out, gain, bias, alpha, block_rows):
    rows, d_model = x2d.shape
    g_spec = pl.BlockSpec((block_rows, GROUP_WIDTH), lambda i: (i, 0))
    x_spec = pl.BlockSpec((block_rows, d_model), lambda i: (i, 0))
    return pl.pallas_call(
        functools.partial(_mix_out_kernel, alpha),
        grid=(rows // block_rows,),
        in_specs=[x_spec, g_spec, g_spec, g_spec, g_spec,
                  _full_spec(w_out), _full_spec(gain), _full_spec(bias)],
        out_specs=x_spec,
        out_shape=jax.ShapeDtypeStruct(x2d.shape, F32),
        compiler_params=pltpu.CompilerParams(dimension_semantics=("arbitrary",),
                                             vmem_limit_bytes=VMEM_LIMIT),
        name="mix_out",
    )(x2d, oa, ga, ob, gb, w_out, gain, bias)


def kernel(x_prompt, x_sample, cache_k_sb, cache_v_sb, cache_k_moba, cache_v_moba, page_table,
           w_in, w_out, ln_gain, ln_bias):
    depth = w_in.shape[0]
    assert depth == 1, "single-layer trunk"
    bsz, seq, d_model = x_prompt.shape
    dbs, n_tok, _ = x_sample.shape
    alpha = (2.0 * depth) ** 0.25
    slopes = jnp.asarray([2.0 ** (-8.0 * (i + 1) / N_HEADS) for i in range(N_HEADS)], dtype=F32)
    slope_rows = jnp.broadcast_to(jnp.tile(slopes, n_tok)[:, None], (n_tok * N_HEADS, LANES))

    g = GROUP_WIDTH
    w = w_in[0]
    w_hi = w.astype(BF16)
    w_lo = (w[:, 4 * g:6 * g] - w_hi[:, 4 * g:6 * g].astype(F32)).astype(BF16)

    def col(a, c):
        return a[:, c * g:(c + 1) * g]

    w_row = jnp.concatenate([col(w_hi, c) for c in ROW_GROUPS], axis=1)
    w_row_lo = col(w_lo, 1)
    w_feat = jnp.concatenate([col(w_hi, c) for c in FEAT_GROUPS], axis=1).T
    w_feat_lo = w_lo.T
    w_o = w_out[0].astype(BF16)
    gain = ln_gain[0][None, :]
    bias = ln_bias[0][None, :]

    def pages(c):
        return jnp.transpose(c[0], (0, 2, 3, 1)).reshape(c.shape[1], GROUP_WIDTH, c.shape[2])

    def heads_from_features(a):
        return jnp.transpose(a.reshape(bsz, N_HEADS, HEAD_DIM, seq), (0, 3, 1, 2))[None]

    def heads_from_rows(a):
        return a.reshape(1, dbs, n_tok, N_HEADS, HEAD_DIM)

    xp = x_prompt.reshape(bsz * seq, d_model)
    (ga, gb, ka_rows, kb_rows, kmean, qat, qbt, kat, vat, kbt, vbt) = _project_prompt(
        xp, w_row, w_row_lo, w_feat, w_feat_lo, bsz, 512)
    shp = (bsz, seq, GROUP_WIDTH)
    o_a = _sb_prompt(qat, ka_rows.reshape(shp), vat)
    o_b = _moba_prompt(slopes, qbt, kb_rows.reshape(shp), vbt, kmean.reshape(bsz, seq // MOBA_BLOCK, GROUP_WIDTH))
    y_p = _mix_out(xp, o_a.reshape(xp.shape[0], GROUP_WIDTH), ga, o_b.reshape(xp.shape[0], GROUP_WIDTH), gb,
                   w_o, gain, bias, alpha, 512)

    xs = x_sample.reshape(dbs * n_tok, d_model)
    sqa, ska, sva, sga, sqb, skb, svb, sgb = _project_rows(xs, w_hi, w_lo)
    sshp = (dbs, n_tok, GROUP_WIDTH)
    so_a, so_b = _sample_attention(
        page_table, slope_rows, sqa.reshape(sshp), sqb.reshape(sshp),
        ska.reshape(sshp), sva.reshape(sshp), skb.reshape(sshp), svb.reshape(sshp),
        pages(cache_k_sb), pages(cache_v_sb), pages(cache_k_moba), pages(cache_v_moba))
    y_s = _mix_out(xs, so_a.reshape(xs.shape[0], GROUP_WIDTH), sga, so_b.reshape(xs.shape[0], GROUP_WIDTH), sgb,
                   w_o, gain, bias, alpha, dbs * n_tok)

    return (y_p.reshape(x_prompt.shape), y_s.reshape(x_sample.shape),
            heads_from_features(kat), heads_from_features(vat),
            heads_from_features(kbt), heads_from_features(vbt),
            heads_from_rows(ska), heads_from_rows(sva), heads_from_rows(skb), heads_from_rows(svb))
```

```python
import functools
import math

import jax
import jax.numpy as jnp
from jax import lax
from jax.experimental import pallas as pl
from jax.experimental.pallas import tpu as pltpu

HEAD_DIM = 64
N_HEADS = 8
GROUP_WIDTH = N_HEADS * HEAD_DIM
HEAD_PAIR_WIDTH = 2 * HEAD_DIM
LANES = 128
MOBA_BLOCK = 256
MOBA_TOPK = 3
ATT_BLOCK = 256
QK_SCALE = HEAD_DIM ** -0.5
LOG2E = math.log2(math.e)
LN_EPS = 1e-5
NEG_BIG = -1e30
SOFTPLUS2_CLAMP = 100.0
RUN_DEAD = -160.0
RUN_DEAD_LN = -112.0
NEW_PAD = 128
PAGES_PER_STEP = 8
ONES_ROWS = 16
VMEM_LIMIT = 48 * 1024 * 1024

AUG_ONE = 0
AUG_KEYPOS = 3
AUG_BLOCK = 8

F32 = jnp.float32
BF16 = jnp.bfloat16


def _dot(a, b):
    return jnp.dot(a, b, preferred_element_type=F32)


def _dot_nt(a, b):
    return lax.dot_general(a, b, (((1,), (1,)), ((), ())), preferred_element_type=F32)


def _split(x):
    hi = x.astype(BF16)
    lo = (x - hi.astype(F32)).astype(BF16)
    return hi, lo


def _split3(x):
    hi = x.astype(BF16)
    r = x - hi.astype(F32)
    mid = r.astype(BF16)
    lo = (r - mid.astype(F32)).astype(BF16)
    return hi, mid, lo


def _dot_3pass(a, b):
    ah, al = _split(a)
    bh, bl = _split(b)
    return _dot(ah, bh) + (_dot(ah, bl) + _dot(al, bh))


def _softplus(z):
    return jnp.maximum(z, 0.0) + jnp.log(1.0 + jnp.exp(-jnp.abs(z)))


def _softplus2(z):
    return jnp.maximum(jnp.log2(1.0 + jnp.exp2(jnp.minimum(z, SOFTPLUS2_CLAMP))), z)


def _full_spec(a, single_buffer=False):
    mode = pl.Buffered(1) if single_buffer else None
    return pl.BlockSpec(a.shape, lambda *_: (0,) * a.ndim, pipeline_mode=mode)


def _proj_rows_kernel(x_ref, w_ref, wlo_ref, *out_refs):
    xh, xl = _split(x_ref[...])
    g = GROUP_WIDTH
    for c, o_ref in enumerate(out_refs):
        wc = w_ref[:, c * g:(c + 1) * g]
        out = _dot(xh, wc)
        if c in (4, 5):
            out = out + (_dot(xh, wlo_ref[:, (c - 4) * g:(c - 3) * g]) + _dot(xl, wc))
        o_ref[...] = out


def _project_rows(x2d, w_hi, w_lo):
    rows, _ = x2d.shape
    out = jax.ShapeDtypeStruct((rows, GROUP_WIDTH), F32)
    return pl.pallas_call(
        _proj_rows_kernel,
        grid=(1,),
        in_specs=[_full_spec(x2d), _full_spec(w_hi), _full_spec(w_lo)],
        out_specs=[pl.BlockSpec((rows, GROUP_WIDTH), lambda i: (0, 0))] * 8,
        out_shape=[out] * 8,
        compiler_params=pltpu.CompilerParams(dimension_semantics=("arbitrary",),
                                             vmem_limit_bytes=VMEM_LIMIT),
        name="proj_rows",
    )(x2d, w_hi, w_lo)


ROW_GROUPS = (3, 7, 1, 5)
FEAT_GROUPS = (0, 4, 1, 2, 5, 6)
FEAT_3PASS = (1, 4)


def _proj_prompt_kernel(x_ref, wr_ref, wrlo_ref, wf_ref, wflo_ref,
                        ga_ref, gb_ref, kar_ref, kbr_ref, kmean_ref,
                        qat_ref, qbt_ref, kat_ref, vat_ref, kbt_ref, vbt_ref):
    x = x_ref[...]
    xh, xl = _split(x)
    g = GROUP_WIDTH
    for c, o_ref in enumerate((ga_ref, gb_ref, kar_ref, kbr_ref)):
        o_ref[...] = _dot(xh, wr_ref[:, c * g:(c + 1) * g]).astype(o_ref.dtype)
    lo_slot = 0
    for c, o_ref in enumerate((qat_ref, qbt_ref, kat_ref, vat_ref, kbt_ref, vbt_ref)):
        wc = wf_ref[c * g:(c + 1) * g, :]
        out = _dot_nt(wc, xh)
        if c in FEAT_3PASS:
            out = out + (_dot_nt(wflo_ref[lo_slot * g:(lo_slot + 1) * g, :], xh) + _dot_nt(wc, xl))
            lo_slot += 1
        o_ref[0] = out
    n_blk = x.shape[0] // MOBA_BLOCK
    row8 = lax.broadcasted_iota(jnp.int32, (8, x.shape[1]), 0)
    xbar = jnp.zeros((8, x.shape[1]), F32)
    for i in range(n_blk):
        mean_i = jnp.sum(x[i * MOBA_BLOCK:(i + 1) * MOBA_BLOCK], axis=0, keepdims=True) * (1.0 / MOBA_BLOCK)
        xbar = jnp.where(row8 == i, mean_i, xbar)
    bh, bl = _split(xbar)
    wk = wr_ref[:, 3 * g:4 * g]
    km = _dot(bh, wk) + (_dot(bh, wrlo_ref[...]) + _dot(bl, wk))
    kmean_ref[0] = km[0:n_blk]


def _project_prompt(x2d, w_row, w_row_lo, w_feat, w_feat_lo, batch, block_rows):
    rows, d_model = x2d.shape
    seq = rows // batch
    per_seq = seq // block_rows
    n_blk = block_rows // MOBA_BLOCK
    row_spec = pl.BlockSpec((block_rows, GROUP_WIDTH), lambda i: (i, 0))
    feat_spec = pl.BlockSpec((1, GROUP_WIDTH, block_rows), lambda i: (i // per_seq, 0, i % per_seq))
    row_f32 = jax.ShapeDtypeStruct((rows, GROUP_WIDTH), F32)
    row_bf16 = jax.ShapeDtypeStruct((rows, GROUP_WIDTH), BF16)
    feat = jax.ShapeDtypeStruct((batch, GROUP_WIDTH, seq), F32)
    kmean = jax.ShapeDtypeStruct((rows // block_rows, n_blk, GROUP_WIDTH), F32)
    return pl.pallas_call(
        _proj_prompt_kernel,
        grid=(rows // block_rows,),
        in_specs=[pl.BlockSpec((block_rows, d_model), lambda i: (i, 0)),
                  _full_spec(w_row, True), _full_spec(w_row_lo, True),
                  _full_spec(w_feat, True), _full_spec(w_feat_lo, True)],
        out_specs=[row_spec] * 4 + [pl.BlockSpec((1, n_blk, GROUP_WIDTH), lambda i: (i, 0, 0))] + [feat_spec] * 6,
        out_shape=[row_f32, row_f32, row_bf16, row_bf16, kmean] + [feat] * 6,
        compiler_params=pltpu.CompilerParams(dimension_semantics=("arbitrary",),
                                             vmem_limit_bytes=VMEM_LIMIT),
        name="proj_prompt",
    )(x2d, w_row, w_row_lo, w_feat, w_feat_lo)


def _pair_columns(qt, tq):
    row = lax.broadcasted_iota(jnp.int32, qt.shape, 0)
    zero = jnp.zeros_like(qt)
    return jnp.concatenate([jnp.where(row < HEAD_DIM, qt, zero),
                            jnp.where(row >= HEAD_DIM, qt, zero)], axis=1)


def _pair_merge_rows(acc_t, tq):
    row = lax.broadcasted_iota(jnp.int32, (HEAD_PAIR_WIDTH, tq), 0)
    return jnp.where(row < HEAD_DIM, acc_t[:, :tq], acc_t[:, tq:]).T


def _blk(kj):
    return pl.ds(pl.multiple_of(kj * ATT_BLOCK, ATT_BLOCK), ATT_BLOCK)


def _prompt_grid(seq, bsz):
    return (bsz, GROUP_WIDTH // HEAD_PAIR_WIDTH, seq // ATT_BLOCK)


def _qt_spec():
    return pl.BlockSpec((1, HEAD_PAIR_WIDTH, ATT_BLOCK), lambda b, h, i: (b, h, i))


def _krows_spec(seq):
    return pl.BlockSpec((1, seq, HEAD_PAIR_WIDTH), lambda b, h, i: (b, 0, h))


def _vt_spec(seq):
    return pl.BlockSpec((1, HEAD_PAIR_WIDTH, seq), lambda b, h, i: (b, h, 0))


def _orow_spec():
    return pl.BlockSpec((1, ATT_BLOCK, HEAD_PAIR_WIDTH), lambda b, h, i: (b, i, h))


def _sb_prompt_kernel(qt_ref, k_ref, vt_ref, o_ref, u_ref, vtb_ref, run_s, acc_s):
    b, hp, qi = pl.program_id(0), pl.program_id(1), pl.program_id(2)
    tq = ATT_BLOCK
    half = tq // 2
    u_rows = half + ONES_ROWS

    @pl.when((b == 0) & (hp == 0) & (qi == 0))
    def _():
        s_i = lax.broadcasted_iota(jnp.int32, (u_rows, tq), 0)
        j_i = lax.broadcasted_iota(jnp.int32, (u_rows, tq), 1) & (half - 1)
        u_ref[...] = jnp.where((j_i >= s_i) | (s_i >= half), -1.0, 0.0).astype(BF16)

    @pl.when(qi == 0)
    def _():
        vtb_ref[...] = vt_ref[0].astype(BF16)

    q_cols = _pair_columns((qt_ref[0] * (QK_SCALE * LOG2E)).astype(BF16), tq)

    def scores(t, diagonal):
        z = _dot(k_ref[0, _blk(qi - t), :], q_cols)
        sp = _softplus2(z)
        if diagonal:
            key = lax.broadcasted_iota(jnp.int32, z.shape, 0)
            qry = lax.broadcasted_iota(jnp.int32, z.shape, 1) & (tq - 1)
            strict = key < qry
            sp = jnp.where(strict, sp, 0.0)
            z = jnp.where(strict, z, NEG_BIG)
        hi, lo = _split(sp)
        halves = [jnp.concatenate([hi[h * half:(h + 1) * half], lo[h * half:(h + 1) * half]], axis=0)
                  for h in range(2)]
        return z, halves

    def suffix(halves):
        neg_u = u_ref[...]
        return [_dot(neg_u, hl) for hl in halves]

    def weigh(t, z, locs):
        run = run_s[...]
        tot_first = locs[0][half:half + 1]
        tot_second = locs[1][half:half + 1]
        first = (z[0:half] + locs[0][0:half]) + (run + tot_second)
        second = (z[half:tq] + locs[1][0:half]) + run
        w = jnp.exp2(jnp.concatenate([first, second], axis=0)).astype(BF16)
        run_s[...] = run + (tot_first + tot_second)
        acc_s[...] = acc_s[...] + _dot(vtb_ref[:, _blk(qi - t)], w)

    def single(t, diagonal):
        z, halves = scores(t, diagonal)
        weigh(t, z, suffix(halves))

    def pair(t, diagonal):
        z0, h0 = scores(t, diagonal)
        z1, h1 = scores(t + 1, False)
        l0 = suffix(h0)
        l1 = suffix(h1)
        weigh(t, z0, l0)
        weigh(t + 1, z1, l1)

    def alive():
        return jnp.max(run_s[...]) > RUN_DEAD

    run_s[...] = jnp.zeros_like(run_s)
    acc_s[...] = jnp.zeros_like(acc_s)

    @pl.when(qi == 0)
    def _():
        single(0, True)

    @pl.when(qi >= 1)
    def _():
        pair(0, True)

        def more_pairs(c):
            return (c[0] + 1 <= qi) & c[1]

        def next_pair(c):
            pair(c[0], False)
            return c[0] + 2, alive()

        t, live = lax.while_loop(more_pairs, next_pair, (jnp.int32(2), alive()))

        @pl.when((t == qi) & live)
        def _():
            single(t, False)

    o_ref[0] = _pair_merge_rows(acc_s[...], tq)


def _sb_prompt(qt, k_rows, vt):
    bsz, _, seq = qt.shape
    tq = ATT_BLOCK
    return pl.pallas_call(
        _sb_prompt_kernel,
        grid=_prompt_grid(seq, bsz),
        in_specs=[_qt_spec(), _krows_spec(seq), _vt_spec(seq)],
        out_specs=_orow_spec(),
        out_shape=jax.ShapeDtypeStruct((bsz, seq, GROUP_WIDTH), F32),
        scratch_shapes=[pltpu.VMEM((tq // 2 + ONES_ROWS, tq), BF16),
                        pltpu.VMEM((HEAD_PAIR_WIDTH, seq), BF16),
                        pltpu.VMEM((1, 2 * tq), F32),
                        pltpu.VMEM((HEAD_PAIR_WIDTH, 2 * tq), F32)],
        compiler_params=pltpu.CompilerParams(dimension_semantics=("arbitrary",) * 3,
                                             vmem_limit_bytes=VMEM_LIMIT),
        name="sb_prompt",
    )(qt, k_rows, vt)


def _rank_before_rows(g, n_valid, n_cand):
    blk = lax.broadcasted_iota(jnp.int32, g.shape, 0)
    cnt = jnp.zeros(g.shape, jnp.int32)
    for m in range(n_cand):
        gm = g[m:m + 1, :]
        beats = (gm > g) | ((gm == g) & (m < blk))
        cnt = cnt + jnp.where(beats & (m < n_valid), 1, 0)
    return cnt, blk


def _block_pairs(n_blocks):
    return [(q, t) for q in range(n_blocks) for t in range(q + 1)]


def _moba_prompt_kernel(slopes_ref, qt_ref, k_ref, vt_ref, kmean_ref, o_ref,
                        kaug_ref, vaug_ref, qaug_s, mask_s, s_s, p_s, pv_s, top_s, m_all, acc_all,
                        pair_q, pair_t):
    b, hp = pl.program_id(0), pl.program_id(1)
    tq = ATT_BLOCK
    seq = k_ref.shape[1]
    n_blocks = seq // MOBA_BLOCK
    pairs = _block_pairs(n_blocks)
    n_pairs = len(pairs)

    @pl.when((b == 0) & (hp == 0))
    def _():
        for i, (q, t) in enumerate(pairs):
            pair_q[i] = q
            pair_t[i] = t
        key = lax.broadcasted_iota(jnp.int32, (tq, 2 * tq), 0)
        qry = lax.broadcasted_iota(jnp.int32, (tq, 2 * tq), 1) & (tq - 1)
        mask_s[0] = jnp.zeros((tq, 2 * tq), F32)
        mask_s[1] = jnp.where(key <= qry, 0.0, NEG_BIG)

    kaug_ref[:, 0:HEAD_PAIR_WIDTH] = k_ref[0]
    lane = lax.broadcasted_iota(jnp.int32, (MOBA_BLOCK, LANES), 1)
    key = lax.broadcasted_iota(jnp.int32, (MOBA_BLOCK, LANES), 0).astype(F32)
    base = jnp.where(lane < AUG_ONE + 3, 1.0,
                     jnp.where(lane < AUG_KEYPOS + 3, key, 0.0))
    for n in range(n_blocks):
        kaug_ref[n * MOBA_BLOCK:(n + 1) * MOBA_BLOCK, HEAD_PAIR_WIDTH:] = (
            jnp.where(lane == AUG_BLOCK + n, 1.0, base).astype(BF16))
    vaug_ref[0:HEAD_PAIR_WIDTH, :] = vt_ref[0].astype(BF16)
    vaug_ref[HEAD_PAIR_WIDTH:, :] = jnp.ones((ONES_ROWS, seq), BF16)

    lane = lax.broadcasted_iota(jnp.int32, (1, 2 * tq), 1)
    slope = jnp.where(lane < tq, slopes_ref[2 * hp], slopes_ref[2 * hp + 1]) * LOG2E
    q_off = (lane & (tq - 1)).astype(F32)
    row8 = lax.broadcasted_iota(jnp.int32, (8, 2 * tq), 0)
    small = jnp.zeros((8, 2 * tq), F32)
    for first, terms in ((AUG_ONE, _split3(-slope * q_off)), (AUG_KEYPOS, _split3(slope))):
        for r, v in enumerate(terms):
            small = jnp.where(row8 == first + r, v.astype(F32), small)
    kmean = kmean_ref[0]

    def prepare(qb, carry):
        qt = qt_ref[0, :, _blk(qb)]
        q_cols = _pair_columns(qt, tq)
        gate = _dot_3pass(kmean, q_cols)
        cnt, blk = _rank_before_rows(gate, qb, n_blocks)
        chosen = ((blk < qb) & (cnt < MOBA_TOPK)) | (blk == qb)
        choice_bias = jnp.where(chosen, 0.0, NEG_BIG)
        qaug_s[qb] = jnp.concatenate(
            [q_cols * (QK_SCALE * LOG2E), small, choice_bias,
             jnp.zeros((LANES - 8 - n_blocks, 2 * tq), F32)], axis=0).astype(BF16)
        m_all[qb] = jnp.full(m_all.shape[1:], NEG_BIG, F32)
        acc_all[qb] = jnp.zeros(acc_all.shape[1:], F32)
        return carry

    lax.fori_loop(0, n_blocks, prepare, 0)

    def score(i, slot):
        q, t = pair_q[i], pair_t[i]
        s_s[slot & 1] = _dot(kaug_ref[_blk(q - t), :], qaug_s[q])

    def probs(i, slot):
        t = pair_t[i]
        s = s_s[slot & 1] + mask_s[(t == 0).astype(jnp.int32)]
        top = jnp.max(s, axis=0, keepdims=True)
        p_s[slot & 1] = jnp.exp2(s - top).astype(BF16)
        top_s[slot & 3, 0:1, :] = top - slope * (t * MOBA_BLOCK).astype(F32)

    def value(i, slot):
        q, t = pair_q[i], pair_t[i]
        pv_s[slot & 1] = _dot(vaug_ref[:, _blk(q - t)], p_s[slot & 1])

    def merge(i, slot):
        q = pair_q[i]
        top = top_s[slot & 3, 0:1, :]
        m_run = m_all[q, 0:1, :]
        m_new = jnp.maximum(m_run, top)
        acc_all[q] = acc_all[q] * jnp.exp2(m_run - m_new) + pv_s[slot & 1] * jnp.exp2(top - m_new)
        m_all[q, 0:1, :] = m_new

    def tick(i, slot, first=0, last=3):
        stages = (score, probs, value, merge)
        for k in range(last, first - 1, -1):
            stages[k](i - k, slot - k)

    depth = 3
    for i in range(depth):
        tick(i, i, last=i)
    unroll = 4
    n_steady = n_pairs - depth
    n_groups = n_steady // unroll

    def steady(g, carry):
        base = depth + g * unroll
        for u in range(unroll):
            tick(base + u, depth + u)
        return carry

    lax.fori_loop(0, n_groups, steady, 0)
    for i in range(depth + n_groups * unroll, n_pairs):
        tick(i, i)
    for k in range(1, depth + 1):
        tick(n_pairs - 1 + k, n_pairs - 1 + k, first=k)

    def finish(qb, carry):
        acc = acc_all[qb]
        out_t = acc[0:HEAD_PAIR_WIDTH] / acc[HEAD_PAIR_WIDTH:HEAD_PAIR_WIDTH + 1]
        o_ref[0, _blk(qb), :] = _pair_merge_rows(out_t, tq)
        return carry

    lax.fori_loop(0, n_blocks, finish, 0)


def _moba_prompt(slopes, qt, k_rows, vt, kmean):
    bsz, _, seq = qt.shape
    n_blocks = seq // MOBA_BLOCK
    assert AUG_BLOCK + n_blocks <= LANES
    tq = ATT_BLOCK
    n_pairs = len(_block_pairs(n_blocks))
    acc_rows = HEAD_PAIR_WIDTH + ONES_ROWS
    feat_spec = pl.BlockSpec((1, HEAD_PAIR_WIDTH, seq), lambda b, h: (b, h, 0))
    rows_spec = pl.BlockSpec((1, seq, HEAD_PAIR_WIDTH), lambda b, h: (b, 0, h))
    return pl.pallas_call(
        _moba_prompt_kernel,
        grid=(bsz, GROUP_WIDTH // HEAD_PAIR_WIDTH),
        in_specs=[pl.BlockSpec(memory_space=pltpu.SMEM), feat_spec, rows_spec, feat_spec,
                  pl.BlockSpec((1, n_blocks, HEAD_PAIR_WIDTH), lambda b, h: (b, 0, h))],
        out_specs=rows_spec,
        out_shape=jax.ShapeDtypeStruct((bsz, seq, GROUP_WIDTH), F32),
        scratch_shapes=[pltpu.VMEM((seq, HEAD_PAIR_WIDTH + LANES), BF16),
                        pltpu.VMEM((acc_rows, seq), BF16),
                        pltpu.VMEM((n_blocks, 2 * LANES, 2 * tq), BF16),
                        pltpu.VMEM((2, tq, 2 * tq), F32),
                        pltpu.VMEM((2, tq, 2 * tq), F32),
                        pltpu.VMEM((2, tq, 2 * tq), BF16),
                        pltpu.VMEM((2, acc_rows, 2 * tq), F32),
                        pltpu.VMEM((4, 8, 2 * tq), F32),
                        pltpu.VMEM((n_blocks, 8, 2 * tq), F32),
                        pltpu.VMEM((n_blocks, acc_rows, 2 * tq), F32),
                        pltpu.SMEM((n_pairs,), jnp.int32),
                        pltpu.SMEM((n_pairs,), jnp.int32)],
        compiler_params=pltpu.CompilerParams(dimension_semantics=("arbitrary",) * 2,
                                             vmem_limit_bytes=VMEM_LIMIT),
        name="moba_prompt",
    )(slopes, qt, k_rows, vt, kmean)


def _suffix_matrix(n):
    j = lax.broadcasted_iota(jnp.int32, (n, n), 0)
    s = lax.broadcasted_iota(jnp.int32, (n, n), 1)
    return jnp.where(j > s, 1.0, 0.0).astype(BF16)


def _suffix_sums(x, t2):
    hi, lo = _split(x)
    return _dot(jnp.concatenate([hi, lo], axis=1), t2)


def _rank_before_lanes(g, n_cand):
    lane = lax.broadcasted_iota(jnp.int32, g.shape, 1)
    cnt = jnp.zeros(g.shape, jnp.int32)
    for m in range(n_cand):
        gm = g[:, m:m + 1]
        beats = (gm > g) | ((gm == g) & (m < lane))
        cnt = cnt + jnp.where(beats, 1, 0)
    return cnt, lane


def _sample_kernel(pt_ref, slope_ref, qa_ref, qb_ref, kan_ref, van_ref, kbn_ref, vbn_ref, *rest):
    del pt_ref
    pps = PAGES_PER_STEP
    ka_refs, va_refs, kb_refs, vb_refs = (rest[i * pps:(i + 1) * pps] for i in range(4))
    (oa_ref, ob_ref, t2_ref, qa_s, qb_s, run_s, acca_s, gate_s, mx_s, l_s, accb_s,
     mo_s, lo_s, acco_s, new_s, live_s) = rest[4 * pps:]
    j = pl.program_id(1)
    n_steps = pl.num_programs(1)
    blk = MOBA_BLOCK
    bps = pps * NEW_PAD // blk
    n_cached = accb_s.shape[0]
    n_first = (n_steps - 1 - j) * bps
    n_tok = qa_ref.shape[1]
    n_rows = n_tok * N_HEADS

    row = lax.broadcasted_iota(jnp.int32, (n_rows, GROUP_WIDTH), 0)
    lane_w = lax.broadcasted_iota(jnp.int32, (n_rows, GROUP_WIDTH), 1)
    head_mask = (lane_w // HEAD_DIM) == (row % N_HEADS)
    slope = slope_ref[:, 0:1]
    lane = lax.broadcasted_iota(jnp.int32, (n_rows, LANES), 1)

    @pl.when(j == 0)
    def _():
        t = _suffix_matrix(blk)
        t2_ref[0:blk, :] = t
        t2_ref[blk:, :] = t
        live_s[0] = 1

        def expand(q):
            rep = jnp.concatenate([jnp.broadcast_to(q[t:t + 1], (N_HEADS, GROUP_WIDTH))
                                   for t in range(n_tok)], axis=0)
            return jnp.where(head_mask, rep, 0.0)

        qa_s[...] = (expand(qa_ref[0]) * QK_SCALE).astype(BF16)
        qh, ql = _split(expand(qb_ref[0]))
        qb_s[...] = jnp.concatenate([qh, ql], axis=0)
        gate_s[...] = jnp.zeros_like(gate_s)
        mx_s[...] = jnp.zeros_like(mx_s)
        l_s[...] = jnp.zeros_like(l_s)

        def padded(ref):
            new_s[...] = jnp.zeros_like(new_s)
            new_s[0:n_tok, :] = ref[0]
            return new_s[...].astype(BF16)

        tok = (lax.broadcasted_iota(jnp.int32, (n_rows, NEW_PAD), 0) // N_HEADS)
        col = lax.broadcasted_iota(jnp.int32, (n_rows, NEW_PAD), 1)
        z = _dot_nt(qa_s[...], padded(kan_ref))
        strict = col < tok
        sp = _softplus(z)
        log_keep = jnp.where(strict, -sp, 0.0)
        hi, lo = _split(log_keep)
        tn = t2_ref[0:NEW_PAD, 0:NEW_PAD]
        between = _dot(hi, tn) + _dot(lo, tn)
        w = jnp.where(strict, jnp.exp((z - sp) + between), 0.0)
        run_s[...] = jnp.sum(log_keep, axis=1, keepdims=True)
        acca_s[...] = _dot(w.astype(BF16), padded(van_ref))
        s = _dot_nt(qb_s[0:n_rows, :], padded(kbn_ref)) * QK_SCALE
        s = s - slope * (tok - col).astype(F32)
        s = jnp.where(col <= tok, s, NEG_BIG)
        m_o = jnp.max(s, axis=1, keepdims=True)
        p = jnp.exp(s - m_o)
        mo_s[...] = m_o
        lo_s[...] = jnp.sum(p, axis=1, keepdims=True)
        acco_s[...] = _dot(p.astype(BF16), padded(vbn_ref))

    def pages(refs):
        return jnp.concatenate([r[0] for r in refs], axis=1)

    @pl.when(live_s[0] == 1)
    def _():
        z = _dot(qa_s[...], pages(ka_refs).astype(BF16))
        sp = _softplus(z)
        log_keep = -sp
        t2 = t2_ref[...]
        run = run_s[...]
        betweens = [None] * bps
        for i in reversed(range(bps)):
            lk = log_keep[:, i * blk:(i + 1) * blk]
            betweens[i] = _suffix_sums(lk, t2) + run
            run = run + jnp.sum(lk, axis=1, keepdims=True)
        run_s[...] = run
        live_s[0] = (jnp.max(run) > RUN_DEAD_LN).astype(jnp.int32)
        w = jnp.exp((z - sp) + jnp.concatenate(betweens, axis=1))
        acca_s[...] = acca_s[...] + _dot_nt(w.astype(BF16), pages(va_refs).astype(BF16))

    kb_h, kb_l = _split(pages(kb_refs))
    vb = pages(vb_refs).astype(BF16)
    raw = _dot(qb_s[...], kb_h)
    raw_hh = raw[0:n_rows]
    raw3 = raw_hh + (raw[n_rows:] + _dot(qb_s[0:n_rows, :], kb_l))
    tok = lax.broadcasted_iota(jnp.int32, (n_rows, blk), 0) // N_HEADS
    col = lax.broadcasted_iota(jnp.int32, (n_rows, blk), 1)
    for i in range(bps):
        n = n_first + i
        cols = slice(i * blk, (i + 1) * blk)
        gate_n = jnp.sum(raw3[:, cols], axis=1, keepdims=True) * (1.0 / blk)
        dist = ((n_cached - n) * blk + tok - col).astype(F32)
        s = raw_hh[:, cols] * QK_SCALE - slope * dist
        m_n = jnp.max(s, axis=1, keepdims=True)
        p = jnp.exp(s - m_n)
        l_n = jnp.sum(p, axis=1, keepdims=True)
        here = lane == n
        gate_s[...] = jnp.where(here, gate_n, gate_s[...])
        mx_s[...] = jnp.where(here, m_n, mx_s[...])
        l_s[...] = jnp.where(here, l_n, l_s[...])
        accb_s[n] = _dot_nt(p.astype(BF16), vb[:, cols])

    @pl.when(j == n_steps - 1)
    def _():
        def fold(acc):
            kept = jnp.where(head_mask, acc, 0.0)
            return jnp.sum(kept.reshape(n_tok, N_HEADS, GROUP_WIDTH), axis=1)

        oa_ref[0] = fold(acca_s[...])
        cnt, ln = _rank_before_lanes(gate_s[...], n_cached)
        sel = (ln < n_cached) & (cnt < MOBA_TOPK)
        mx = jnp.where(sel, mx_s[...], NEG_BIG)
        m_o = mo_s[...]
        m_all = jnp.maximum(m_o, jnp.max(mx, axis=1, keepdims=True))
        c = jnp.where(sel, jnp.exp(mx - m_all), 0.0)
        c_o = jnp.exp(m_o - m_all)
        l_tot = c_o * lo_s[...] + jnp.sum(c * l_s[...], axis=1, keepdims=True)
        acc = c_o * acco_s[...]
        for nb in range(n_cached):
            acc = acc + c[:, nb:nb + 1] * accb_s[nb]
        ob_ref[0] = fold(acc / l_tot)


def _sample_attention(page_table, slope_rows, qa, qb, ka_new, va_new, kb_new, vb_new,
                      ck_sb, cv_sb, ck_mb, cv_mb):
    dbs, n_tok, _ = qa.shape
    n_pages = page_table.shape[1]
    page = ck_sb.shape[2]
    pps = PAGES_PER_STEP
    assert page == NEW_PAD and n_pages % pps == 0 and (pps * page) % MOBA_BLOCK == 0
    n_steps = n_pages // pps
    n_cached = n_pages * page // MOBA_BLOCK
    assert n_cached <= LANES
    n_rows = n_tok * N_HEADS

    tok_spec = pl.BlockSpec((1, n_tok, GROUP_WIDTH), lambda b, j, pt: (b, 0, 0))

    def page_spec(which):
        return pl.BlockSpec((1, GROUP_WIDTH, page),
                            lambda b, j, pt: (pt[b, pps * (n_steps - 1 - j) + which], 0, 0))

    grid_spec = pltpu.PrefetchScalarGridSpec(
        num_scalar_prefetch=1,
        grid=(dbs, n_steps),
        in_specs=[pl.BlockSpec((n_rows, LANES), lambda b, j, pt: (0, 0))] + [tok_spec] * 6
                 + [page_spec(p) for p in range(pps)] * 4,
        out_specs=[tok_spec, tok_spec],
        scratch_shapes=[
            pltpu.VMEM((2 * MOBA_BLOCK, MOBA_BLOCK), BF16),
            pltpu.VMEM((n_rows, GROUP_WIDTH), BF16),
            pltpu.VMEM((2 * n_rows, GROUP_WIDTH), BF16),
            pltpu.VMEM((n_rows, 1), F32),
            pltpu.VMEM((n_rows, GROUP_WIDTH), F32),
            pltpu.VMEM((n_rows, LANES), F32),
            pltpu.VMEM((n_rows, LANES), F32),
            pltpu.VMEM((n_rows, LANES), F32),
            pltpu.VMEM((n_cached, n_rows, GROUP_WIDTH), F32),
            pltpu.VMEM((n_rows, 1), F32),
            pltpu.VMEM((n_rows, 1), F32),
            pltpu.VMEM((n_rows, GROUP_WIDTH), F32),
            pltpu.VMEM((NEW_PAD, GROUP_WIDTH), F32),
            pltpu.SMEM((1,), jnp.int32),
        ],
    )
    out = jax.ShapeDtypeStruct((dbs, n_tok, GROUP_WIDTH), F32)
    caches = [c for c in (ck_sb, cv_sb, ck_mb, cv_mb) for _ in range(pps)]
    return pl.pallas_call(
        _sample_kernel,
        grid_spec=grid_spec,
        out_shape=[out, out],
        compiler_params=pltpu.CompilerParams(dimension_semantics=("arbitrary", "arbitrary"),
                                             vmem_limit_bytes=VMEM_LIMIT),
        name="sample_attn",
    )(page_table, slope_rows, qa, qb, ka_new, va_new, kb_new, vb_new, *caches)


def _mix_out_kernel(alpha, x_ref, oa_ref, ga_ref, ob_ref, gb_ref, w_ref, gain_ref, bias_ref, y_ref):
    h = jnp.concatenate([oa_ref[...] * jax.nn.silu(ga_ref[...]),
                         ob_ref[...] * jax.nn.silu(gb_ref[...])], axis=-1)
    out = _dot(h.astype(BF16), w_ref[...])
    y = alpha * x_ref[...] + out
    mu = jnp.mean(y, axis=-1, keepdims=True)
    var = jnp.mean(jnp.square(y - mu), axis=-1, keepdims=True)
    y_ref[...] = (y - mu) * lax.rsqrt(var + LN_EPS) * gain_ref[...] + bias_ref[...]


def _mix_out(x2d, oa, ga, ob, gb, w_out, gain, bias, alpha, block_rows):
    rows, d_model = x2d.shape
    g_spec = pl.BlockSpec((block_rows, GROUP_WIDTH), lambda i: (i, 0))
    x_spec = pl.BlockSpec((block_rows, d_model), lambda i: (i, 0))
    return pl.pallas_call(
        functools.partial(_mix_out_kernel, alpha),
        grid=(rows // block_rows,),
        in_specs=[x_spec, g_spec, g_spec, g_spec, g_spec,
                  _full_spec(w_out), _full_spec(gain), _full_spec(bias)],
        out_specs=x_spec,
        out_shape=jax.ShapeDtypeStruct(x2d.shape, F32),
        compiler_params=pltpu.CompilerParams(dimension_semantics=("arbitrary",),
                                             vmem_limit_bytes=VMEM_LIMIT),
        name="mix_out",
    )(x2d, oa, ga, ob, gb, w_out, gain, bias)


def kernel(x_prompt, x_sample, cache_k_sb, cache_v_sb, cache_k_moba, cache_v_moba, page_table,
           w_in, w_out, ln_gain, ln_bias):
    depth = w_in.shape[0]
    assert depth == 1, "single-layer trunk"
    bsz, seq, d_model = x_prompt.shape
    dbs, n_tok, _ = x_sample.shape
    alpha = (2.0 * depth) ** 0.25
    slopes = jnp.asarray([2.0 ** (-8.0 * (i + 1) / N_HEADS) for i in range(N_HEADS)], dtype=F32)
    slope_rows = jnp.broadcast_to(jnp.tile(slopes, n_tok)[:, None], (n_tok * N_HEADS, LANES))

    g = GROUP_WIDTH
    w = w_in[0]
    w_hi = w.astype(BF16)
    w_lo = (w[:, 4 * g:6 * g] - w_hi[:, 4 * g:6 * g].astype(F32)).astype(BF16)

    def col(a, c):
        return a[:, c * g:(c + 1) * g]

    w_row = jnp.concatenate([col(w_hi, c) for c in ROW_GROUPS], axis=1)
    w_row_lo = col(w_lo, 1)
    w_feat = jnp.concatenate([col(w_hi, c) for c in FEAT_GROUPS], axis=1).T
    w_feat_lo = w_lo.T
    w_o = w_out[0].astype(BF16)
    gain = ln_gain[0][None, :]
    bias = ln_bias[0][None, :]

    def pages(c):
        return jnp.transpose(c[0], (0, 2, 3, 1)).reshape(c.shape[1], GROUP_WIDTH, c.shape[2])

    def heads_from_features(a):
        return jnp.transpose(a.reshape(bsz, N_HEADS, HEAD_DIM, seq), (0, 3, 1, 2))[None]

    def heads_from_rows(a):
        return a.reshape(1, dbs, n_tok, N_HEADS, HEAD_DIM)

    xp = x_prompt.reshape(bsz * seq, d_model)
    (ga, gb, ka_rows, kb_rows, kmean, qat, qbt, kat, vat, kbt, vbt) = _project_prompt(
        xp, w_row, w_row_lo, w_feat, w_feat_lo, bsz, 512)
    shp = (bsz, seq, GROUP_WIDTH)
    o_a = _sb_prompt(qat, ka_rows.reshape(shp), vat)
    o_b = _moba_prompt(slopes, qbt, kb_rows.reshape(shp), vbt, kmean.reshape(bsz, seq // MOBA_BLOCK, GROUP_WIDTH))
    y_p = _mix_out(xp, o_a.reshape(xp.shape[0], GROUP_WIDTH), ga, o_b.reshape(xp.shape[0], GROUP_WIDTH), gb,
                   w_o, gain, bias, alpha, 512)

    xs = x_sample.reshape(dbs * n_tok, d_model)
    sqa, ska, sva, sga, sqb, skb, svb, sgb = _project_rows(xs, w_hi, w_lo)
    sshp = (dbs, n_tok, GROUP_WIDTH)
    so_a, so_b = _sample_attention(
        page_table, slope_rows, sqa.reshape(sshp), sqb.reshape(sshp),
        ska.reshape(sshp), sva.reshape(sshp), skb.reshape(sshp), svb.reshape(sshp),
        pages(cache_k_sb), pages(cache_v_sb), pages(cache_k_moba), pages(cache_v_moba))
    y_s = _mix_out(xs, so_a.reshape(xs.shape[0], GROUP_WIDTH), sga, so_b.reshape(xs.shape[0], GROUP_WIDTH), sgb,
                   w_o, gain, bias, alpha, dbs * n_tok)

    return (y_p.reshape(x_prompt.shape), y_s.reshape(x_sample.shape),
            heads_from_features(kat), heads_from_features(vat),
            heads_from_features(kbt), heads_from_features(vbt),
            heads_from_rows(ska), heads_from_rows(sva), heads_from_rows(skb), heads_from_rows(svb))
```

```python
import functools
import math

import jax
import jax.numpy as jnp
from jax import lax
from jax.experimental import pallas as pl
from jax.experimental.pallas import tpu as pltpu

HEAD_DIM = 64
N_HEADS = 8
GROUP_WIDTH = N_HEADS * HEAD_DIM
HEAD_PAIR_WIDTH = 2 * HEAD_DIM
LANES = 128
MOBA_BLOCK = 256
MOBA_TOPK = 3
ATT_BLOCK = 256
QK_SCALE = HEAD_DIM ** -0.5
LOG2E = math.log2(math.e)
LN_EPS = 1e-5
NEG_BIG = -1e30
SOFTPLUS2_CLAMP = 100.0
RUN_DEAD = -160.0
RUN_DEAD_LN = -112.0
NEW_PAD = 128
PAGES_PER_STEP = 8
ONES_ROWS = 16
VMEM_LIMIT = 48 * 1024 * 1024

AUG_ONE = 0
AUG_KEYPOS = 3
AUG_BLOCK = 8

F32 = jnp.float32
BF16 = jnp.bfloat16


def _dot(a, b):
    return jnp.dot(a, b, preferred_element_type=F32)


def _dot_nt(a, b):
    return lax.dot_general(a, b, (((1,), (1,)), ((), ())), preferred_element_type=F32)


def _split(x):
    hi = x.astype(BF16)
    lo = (x - hi.astype(F32)).astype(BF16)
    return hi, lo


def _split3(x):
    hi = x.astype(BF16)
    r = x - hi.astype(F32)
    mid = r.astype(BF16)
    lo = (r - mid.astype(F32)).astype(BF16)
    return hi, mid, lo


def _dot_3pass(a, b):
    ah, al = _split(a)
    bh, bl = _split(b)
    return _dot(ah, bh) + (_dot(ah, bl) + _dot(al, bh))


def _softplus(z):
    return jnp.maximum(z, 0.0) + jnp.log(1.0 + jnp.exp(-jnp.abs(z)))


def _softplus2(z):
    return jnp.maximum(jnp.log2(1.0 + jnp.exp2(jnp.minimum(z, SOFTPLUS2_CLAMP))), z)


def _full_spec(a, single_buffer=False):
    mode = pl.Buffered(1) if single_buffer else None
    return pl.BlockSpec(a.shape, lambda *_: (0,) * a.ndim, pipeline_mode=mode)


def _proj_rows_kernel(x_ref, w_ref, wlo_ref, *out_refs):
    xh, xl = _split(x_ref[...])
    g = GROUP_WIDTH
    for c, o_ref in enumerate(out_refs):
        wc = w_ref[:, c * g:(c + 1) * g]
        out = _dot(xh, wc)
        if c in (4, 5):
            out = out + (_dot(xh, wlo_ref[:, (c - 4) * g:(c - 3) * g]) + _dot(xl, wc))
        o_ref[...] = out


def _project_rows(x2d, w_hi, w_lo):
    rows, _ = x2d.shape
    out = jax.ShapeDtypeStruct((rows, GROUP_WIDTH), F32)
    return pl.pallas_call(
        _proj_rows_kernel,
        grid=(1,),
        in_specs=[_full_spec(x2d), _full_spec(w_hi), _full_spec(w_lo)],
        out_specs=[pl.BlockSpec((rows, GROUP_WIDTH), lambda i: (0, 0))] * 8,
        out_shape=[out] * 8,
        compiler_params=pltpu.CompilerParams(dimension_semantics=("arbitrary",),
                                             vmem_limit_bytes=VMEM_LIMIT),
        name="proj_rows",
    )(x2d, w_hi, w_lo)


ROW_GROUPS = (3, 7, 1, 5)
FEAT_GROUPS = (0, 4, 1, 2, 5, 6)
FEAT_3PASS = (1,)


def _proj_prompt_kernel(x_ref, wr_ref, wrlo_ref, wf_ref, wflo_ref,
                        ga_ref, gb_ref, kar_ref, kbr_ref, kmean_ref,
                        qat_ref, qbt_ref, kat_ref, vat_ref, kbt_ref, vbt_ref):
    x = x_ref[...]
    xh, xl = _split(x)
    g = GROUP_WIDTH
    for c, o_ref in enumerate((ga_ref, gb_ref, kar_ref, kbr_ref)):
        o_ref[...] = _dot(xh, wr_ref[:, c * g:(c + 1) * g]).astype(o_ref.dtype)
    lo_slot = 0
    for c, o_ref in enumerate((qat_ref, qbt_ref, kat_ref, vat_ref, kbt_ref, vbt_ref)):
        wc = wf_ref[c * g:(c + 1) * g, :]
        out = _dot_nt(wc, xh)
        if c in FEAT_3PASS:
            out = out + (_dot_nt(wflo_ref[lo_slot * g:(lo_slot + 1) * g, :], xh) + _dot_nt(wc, xl))
            lo_slot += 1
        o_ref[0] = out
    n_blk = x.shape[0] // MOBA_BLOCK
    row8 = lax.broadcasted_iota(jnp.int32, (8, x.shape[1]), 0)
    xbar = jnp.zeros((8, x.shape[1]), F32)
    for i in range(n_blk):
        mean_i = jnp.sum(x[i * MOBA_BLOCK:(i + 1) * MOBA_BLOCK], axis=0, keepdims=True) * (1.0 / MOBA_BLOCK)
        xbar = jnp.where(row8 == i, mean_i, xbar)
    bh, bl = _split(xbar)
    wk = wr_ref[:, 3 * g:4 * g]
    km = _dot(bh, wk) + (_dot(bh, wrlo_ref[...]) + _dot(bl, wk))
    kmean_ref[0] = km[0:n_blk]


def _project_prompt(x2d, w_row, w_row_lo, w_feat, w_feat_lo, batch, block_rows):
    rows, d_model = x2d.shape
    seq = rows // batch
    per_seq = seq // block_rows
    n_blk = block_rows // MOBA_BLOCK
    row_spec = pl.BlockSpec((block_rows, GROUP_WIDTH), lambda i: (i, 0))
    feat_spec = pl.BlockSpec((1, GROUP_WIDTH, block_rows), lambda i: (i // per_seq, 0, i % per_seq))
    row_f32 = jax.ShapeDtypeStruct((rows, GROUP_WIDTH), F32)
    row_bf16 = jax.ShapeDtypeStruct((rows, GROUP_WIDTH), BF16)
    feat = jax.ShapeDtypeStruct((batch, GROUP_WIDTH, seq), F32)
    kmean = jax.ShapeDtypeStruct((rows // block_rows, n_blk, GROUP_WIDTH), F32)
    return pl.pallas_call(
        _proj_prompt_kernel,
        grid=(rows // block_rows,),
        in_specs=[pl.BlockSpec((block_rows, d_model), lambda i: (i, 0)),
                  _full_spec(w_row, True), _full_spec(w_row_lo, True),
                  _full_spec(w_feat, True), _full_spec(w_feat_lo, True)],
        out_specs=[row_spec] * 4 + [pl.BlockSpec((1, n_blk, GROUP_WIDTH), lambda i: (i, 0, 0))] + [feat_spec] * 6,
        out_shape=[row_f32, row_f32, row_bf16, row_bf16, kmean] + [feat] * 6,
        compiler_params=pltpu.CompilerParams(dimension_semantics=("arbitrary",),
                                             vmem_limit_bytes=VMEM_LIMIT),
        name="proj_prompt",
    )(x2d, w_row, w_row_lo, w_feat, w_feat_lo)


def _pair_columns(qt, tq):
    row = lax.broadcasted_iota(jnp.int32, qt.shape, 0)
    zero = jnp.zeros_like(qt)
    return jnp.concatenate([jnp.where(row < HEAD_DIM, qt, zero),
                            jnp.where(row >= HEAD_DIM, qt, zero)], axis=1)


def _pair_merge_rows(acc_t, tq):
    row = lax.broadcasted_iota(jnp.int32, (HEAD_PAIR_WIDTH, tq), 0)
    return jnp.where(row < HEAD_DIM, acc_t[:, :tq], acc_t[:, tq:]).T


def _blk(kj):
    return pl.ds(pl.multiple_of(kj * ATT_BLOCK, ATT_BLOCK), ATT_BLOCK)


def _prompt_grid(seq, bsz):
    return (bsz, GROUP_WIDTH // HEAD_PAIR_WIDTH, seq // ATT_BLOCK)


def _qt_spec():
    return pl.BlockSpec((1, HEAD_PAIR_WIDTH, ATT_BLOCK), lambda b, h, i: (b, h, i))


def _krows_spec(seq):
    return pl.BlockSpec((1, seq, HEAD_PAIR_WIDTH), lambda b, h, i: (b, 0, h))


def _vt_spec(seq):
    return pl.BlockSpec((1, HEAD_PAIR_WIDTH, seq), lambda b, h, i: (b, h, 0))


def _orow_spec():
    return pl.BlockSpec((1, ATT_BLOCK, HEAD_PAIR_WIDTH), lambda b, h, i: (b, i, h))


def _sb_prompt_kernel(qt_ref, k_ref, vt_ref, o_ref, u_ref, vtb_ref, run_s, acc_s):
    b, hp, qi = pl.program_id(0), pl.program_id(1), pl.program_id(2)
    tq = ATT_BLOCK
    half = tq // 2
    u_rows = half + ONES_ROWS

    @pl.when((b == 0) & (hp == 0) & (qi == 0))
    def _():
        s_i = lax.broadcasted_iota(jnp.int32, (u_rows, tq), 0)
        j_i = lax.broadcasted_iota(jnp.int32, (u_rows, tq), 1) & (half - 1)
        u_ref[...] = jnp.where((j_i >= s_i) | (s_i >= half), -1.0, 0.0).astype(BF16)

    @pl.when(qi == 0)
    def _():
        vtb_ref[...] = vt_ref[0].astype(BF16)

    q_cols = _pair_columns((qt_ref[0] * (QK_SCALE * LOG2E)).astype(BF16), tq)

    def scores(t, diagonal):
        z = _dot(k_ref[0, _blk(qi - t), :], q_cols)
        sp = _softplus2(z)
        if diagonal:
            key = lax.broadcasted_iota(jnp.int32, z.shape, 0)
            qry = lax.broadcasted_iota(jnp.int32, z.shape, 1) & (tq - 1)
            strict = key < qry
            sp = jnp.where(strict, sp, 0.0)
            z = jnp.where(strict, z, NEG_BIG)
        hi, lo = _split(sp)
        halves = [jnp.concatenate([hi[h * half:(h + 1) * half], lo[h * half:(h + 1) * half]], axis=0)
                  for h in range(2)]
        return z, halves

    def suffix(halves):
        neg_u = u_ref[...]
        return [_dot(neg_u, hl) for hl in halves]

    def weigh(t, z, locs):
        run = run_s[...]
        tot_first = locs[0][half:half + 1]
        tot_second = locs[1][half:half + 1]
        first = (z[0:half] + locs[0][0:half]) + (run + tot_second)
        second = (z[half:tq] + locs[1][0:half]) + run
        w = jnp.exp2(jnp.concatenate([first, second], axis=0)).astype(BF16)
        run_s[...] = run + (tot_first + tot_second)
        acc_s[...] = acc_s[...] + _dot(vtb_ref[:, _blk(qi - t)], w)

    def single(t, diagonal):
        z, halves = scores(t, diagonal)
        weigh(t, z, suffix(halves))

    def pair(t, diagonal):
        z0, h0 = scores(t, diagonal)
        z1, h1 = scores(t + 1, False)
        l0 = suffix(h0)
        l1 = suffix(h1)
        weigh(t, z0, l0)
        weigh(t + 1, z1, l1)

    def alive():
        return jnp.max(run_s[...]) > RUN_DEAD

    run_s[...] = jnp.zeros_like(run_s)
    acc_s[...] = jnp.zeros_like(acc_s)

    @pl.when(qi == 0)
    def _():
        single(0, True)

    @pl.when(qi >= 1)
    def _():
        pair(0, True)

        def more_pairs(c):
            return (c[0] + 1 <= qi) & c[1]

        def next_pair(c):
            pair(c[0], False)
            return c[0] + 2, alive()

        t, live = lax.while_loop(more_pairs, next_pair, (jnp.int32(2), alive()))

        @pl.when((t == qi) & live)
        def _():
            single(t, False)

    o_ref[0] = _pair_merge_rows(acc_s[...], tq)


def _sb_prompt(qt, k_rows, vt):
    bsz, _, seq = qt.shape
    tq = ATT_BLOCK
    return pl.pallas_call(
        _sb_prompt_kernel,
        grid=_prompt_grid(seq, bsz),
        in_specs=[_qt_spec(), _krows_spec(seq), _vt_spec(seq)],
        out_specs=_orow_spec(),
        out_shape=jax.ShapeDtypeStruct((bsz, seq, GROUP_WIDTH), F32),
        scratch_shapes=[pltpu.VMEM((tq // 2 + ONES_ROWS, tq), BF16),
                        pltpu.VMEM((HEAD_PAIR_WIDTH, seq), BF16),
                        pltpu.VMEM((1, 2 * tq), F32),
                        pltpu.VMEM((HEAD_PAIR_WIDTH, 2 * tq), F32)],
        compiler_params=pltpu.CompilerParams(dimension_semantics=("arbitrary",) * 3,
                                             vmem_limit_bytes=VMEM_LIMIT),
        name="sb_prompt",
    )(qt, k_rows, vt)


def _rank_before_rows(g, n_valid, n_cand):
    blk = lax.broadcasted_iota(jnp.int32, g.shape, 0)
    cnt = jnp.zeros(g.shape, jnp.int32)
    for m in range(n_cand):
        gm = g[m:m + 1, :]
        beats = (gm > g) | ((gm == g) & (m < blk))
        cnt = cnt + jnp.where(beats & (m < n_valid), 1, 0)
    return cnt, blk


def _block_pairs(n_blocks):
    return [(q, 0) for q in range(n_blocks)] + [(q, t) for q in range(n_blocks) for t in range(1, q + 1)]


def _moba_prompt_kernel(slopes_ref, qt_ref, k_ref, vt_ref, kmean_ref, o_ref,
                        kaug_ref, vaug_ref, qaug_s, mask_s, s_s, p_s, pv_s, top_s, m_all, acc_all,
                        pair_q, pair_t):
    b, hp = pl.program_id(0), pl.program_id(1)
    tq = ATT_BLOCK
    seq = k_ref.shape[1]
    n_blocks = seq // MOBA_BLOCK
    pairs = _block_pairs(n_blocks)
    n_pairs = len(pairs)

    @pl.when((b == 0) & (hp == 0))
    def _():
        for i, (q, t) in enumerate(pairs):
            pair_q[i] = q
            pair_t[i] = t
        key = lax.broadcasted_iota(jnp.int32, (tq, 2 * tq), 0)
        qry = lax.broadcasted_iota(jnp.int32, (tq, 2 * tq), 1) & (tq - 1)
        mask_s[...] = jnp.where(key <= qry, 0.0, NEG_BIG)

    kaug_ref[:, 0:HEAD_PAIR_WIDTH] = k_ref[0]
    lane = lax.broadcasted_iota(jnp.int32, (MOBA_BLOCK, LANES), 1)
    key = lax.broadcasted_iota(jnp.int32, (MOBA_BLOCK, LANES), 0).astype(F32)
    base = jnp.where(lane < AUG_ONE + 3, 1.0,
                     jnp.where(lane < AUG_KEYPOS + 3, key, 0.0))
    for n in range(n_blocks):
        kaug_ref[n * MOBA_BLOCK:(n + 1) * MOBA_BLOCK, HEAD_PAIR_WIDTH:] = (
            jnp.where(lane == AUG_BLOCK + n, 1.0, base).astype(BF16))
    vaug_ref[0:HEAD_PAIR_WIDTH, :] = vt_ref[0].astype(BF16)
    vaug_ref[HEAD_PAIR_WIDTH:, :] = jnp.ones((ONES_ROWS, seq), BF16)

    lane = lax.broadcasted_iota(jnp.int32, (1, 2 * tq), 1)
    slope = jnp.where(lane < tq, slopes_ref[2 * hp], slopes_ref[2 * hp + 1]) * LOG2E
    q_off = (lane & (tq - 1)).astype(F32)
    row8 = lax.broadcasted_iota(jnp.int32, (8, 2 * tq), 0)
    small = jnp.zeros((8, 2 * tq), F32)
    for first, terms in ((AUG_ONE, _split3(-slope * q_off)), (AUG_KEYPOS, _split3(slope))):
        for r, v in enumerate(terms):
            small = jnp.where(row8 == first + r, v.astype(F32), small)
    kmean = kmean_ref[0]

    def prepare(qb, carry):
        qt = qt_ref[0, :, _blk(qb)]
        q_cols = _pair_columns(qt, tq)
        gate = _dot_3pass(kmean, q_cols)
        cnt, blk = _rank_before_rows(gate, qb, n_blocks)
        chosen = ((blk < qb) & (cnt < MOBA_TOPK)) | (blk == qb)
        choice_bias = jnp.where(chosen, 0.0, NEG_BIG)
        qaug_s[qb] = jnp.concatenate(
            [q_cols * (QK_SCALE * LOG2E), small, choice_bias,
             jnp.zeros((LANES - 8 - n_blocks, 2 * tq), F32)], axis=0).astype(BF16)
        m_all[qb] = jnp.full(m_all.shape[1:], NEG_BIG, F32)
        acc_all[qb] = jnp.zeros(acc_all.shape[1:], F32)
        return carry

    lax.fori_loop(0, n_blocks, prepare, 0)

    def score(i, slot):
        q, t = pair_q[i], pair_t[i]
        s_s[slot & 1] = _dot(kaug_ref[_blk(q - t), :], qaug_s[q])

    def probs(i, slot):
        t = pair_t[i]
        s = s_s[slot & 1]
        if isinstance(i, int) and i < n_blocks:
            s = s + mask_s[...]
        top = jnp.max(s, axis=0, keepdims=True)
        p_s[slot & 1] = jnp.exp2(s - top).astype(BF16)
        top_s[slot & 3, 0:1, :] = top - slope * (t * MOBA_BLOCK).astype(F32)

    def value(i, slot):
        q, t = pair_q[i], pair_t[i]
        pv_s[slot & 1] = _dot(vaug_ref[:, _blk(q - t)], p_s[slot & 1])

    def merge(i, slot):
        q = pair_q[i]
        top = top_s[slot & 3, 0:1, :]
        m_run = m_all[q, 0:1, :]
        m_new = jnp.maximum(m_run, top)
        acc_all[q] = acc_all[q] * jnp.exp2(m_run - m_new) + pv_s[slot & 1] * jnp.exp2(top - m_new)
        m_all[q, 0:1, :] = m_new

    def tick(i, slot, first=0, last=3):
        stages = (score, probs, value, merge)
        for k in range(last, first - 1, -1):
            stages[k](i - k, slot - k)

    depth = 3
    unroll = 4
    n_static = depth + unroll * -(-(n_blocks + 1 - depth) // unroll)
    for i in range(n_static):
        tick(i, i, last=min(i, depth))
    n_groups = (n_pairs - n_static) // unroll

    def steady(g, carry):
        base = n_static + g * unroll
        for u in range(unroll):
            tick(base + u, n_static + u)
        return carry

    lax.fori_loop(0, n_groups, steady, 0)
    for i in range(n_static + n_groups * unroll, n_pairs):
        tick(i, i)
    for k in range(1, depth + 1):
        tick(n_pairs - 1 + k, n_pairs - 1 + k, first=k)

    def finish(qb, carry):
        acc = acc_all[qb]
        out_t = acc[0:HEAD_PAIR_WIDTH] / acc[HEAD_PAIR_WIDTH:HEAD_PAIR_WIDTH + 1]
        o_ref[0, _blk(qb), :] = _pair_merge_rows(out_t, tq)
        return carry

    lax.fori_loop(0, n_blocks, finish, 0)


def _moba_prompt(slopes, qt, k_rows, vt, kmean):
    bsz, _, seq = qt.shape
    n_blocks = seq // MOBA_BLOCK
    assert AUG_BLOCK + n_blocks <= LANES
    tq = ATT_BLOCK
    n_pairs = len(_block_pairs(n_blocks))
    acc_rows = HEAD_PAIR_WIDTH + ONES_ROWS
    feat_spec = pl.BlockSpec((1, HEAD_PAIR_WIDTH, seq), lambda b, h: (b, h, 0))
    rows_spec = pl.BlockSpec((1, seq, HEAD_PAIR_WIDTH), lambda b, h: (b, 0, h))
    return pl.pallas_call(
        _moba_prompt_kernel,
        grid=(bsz, GROUP_WIDTH // HEAD_PAIR_WIDTH),
        in_specs=[pl.BlockSpec(memory_space=pltpu.SMEM), feat_spec, rows_spec, feat_spec,
                  pl.BlockSpec((1, n_blocks, HEAD_PAIR_WIDTH), lambda b, h: (b, 0, h))],
        out_specs=rows_spec,
        out_shape=jax.ShapeDtypeStruct((bsz, seq, GROUP_WIDTH), F32),
        scratch_shapes=[pltpu.VMEM((seq, HEAD_PAIR_WIDTH + LANES), BF16),
                        pltpu.VMEM((acc_rows, seq), BF16),
                        pltpu.VMEM((n_blocks, 2 * LANES, 2 * tq), BF16),
                        pltpu.VMEM((tq, 2 * tq), F32),
                        pltpu.VMEM((2, tq, 2 * tq), F32),
                        pltpu.VMEM((2, tq, 2 * tq), BF16),
                        pltpu.VMEM((2, acc_rows, 2 * tq), F32),
                        pltpu.VMEM((4, 8, 2 * tq), F32),
                        pltpu.VMEM((n_blocks, 8, 2 * tq), F32),
                        pltpu.VMEM((n_blocks, acc_rows, 2 * tq), F32),
                        pltpu.SMEM((n_pairs,), jnp.int32),
                        pltpu.SMEM((n_pairs,), jnp.int32)],
        compiler_params=pltpu.CompilerParams(dimension_semantics=("arbitrary",) * 2,
                                             vmem_limit_bytes=VMEM_LIMIT),
        name="moba_prompt",
    )(slopes, qt, k_rows, vt, kmean)


def _suffix_matrix(n):
    j = lax.broadcasted_iota(jnp.int32, (n, n), 0)
    s = lax.broadcasted_iota(jnp.int32, (n, n), 1)
    return jnp.where(j > s, 1.0, 0.0).astype(BF16)


def _suffix_sums(x, t2):
    hi, lo = _split(x)
    return _dot(jnp.concatenate([hi, lo], axis=1), t2)


def _rank_before_lanes(g, n_cand):
    lane = lax.broadcasted_iota(jnp.int32, g.shape, 1)
    cnt = jnp.zeros(g.shape, jnp.int32)
    for m in range(n_cand):
        gm = g[:, m:m + 1]
        beats = (gm > g) | ((gm == g) & (m < lane))
        cnt = cnt + jnp.where(beats, 1, 0)
    return cnt, lane


def _sample_kernel(pt_ref, slope_ref, qa_ref, qb_ref, kan_ref, van_ref, kbn_ref, vbn_ref,
                   cka_hbm, cva_hbm, *rest):
    pps = PAGES_PER_STEP
    kb_refs, vb_refs = (rest[i * pps:(i + 1) * pps] for i in range(2))
    (oa_ref, ob_ref, t2_ref, qa_s, qb_s, run_s, acca_s, gate_s, mx_s, l_s, accb_s,
     mo_s, lo_s, acco_s, new_s, live_s, ka_ring, va_ring, ring_sem, inflight_s) = rest[2 * pps:]
    b, j = pl.program_id(0), pl.program_id(1)
    n_batch = pl.num_programs(0)
    n_steps = pl.num_programs(1)
    blk = MOBA_BLOCK
    bps = pps * NEW_PAD // blk
    n_cached = accb_s.shape[0]
    n_first = (n_steps - 1 - j) * bps
    n_tok = qa_ref.shape[1]
    n_rows = n_tok * N_HEADS

    row = lax.broadcasted_iota(jnp.int32, (n_rows, GROUP_WIDTH), 0)
    lane_w = lax.broadcasted_iota(jnp.int32, (n_rows, GROUP_WIDTH), 1)
    head_mask = (lane_w // HEAD_DIM) == (row % N_HEADS)
    slope = slope_ref[:, 0:1]
    lane = lax.broadcasted_iota(jnp.int32, (n_rows, LANES), 1)

    @pl.when(j == 0)
    def _():
        t = _suffix_matrix(blk)
        t2_ref[0:blk, :] = t
        t2_ref[blk:, :] = t
        live_s[0] = 1

        def expand(q):
            rep = jnp.concatenate([jnp.broadcast_to(q[t:t + 1], (N_HEADS, GROUP_WIDTH))
                                   for t in range(n_tok)], axis=0)
            return jnp.where(head_mask, rep, 0.0)

        qa_s[...] = (expand(qa_ref[0]) * QK_SCALE).astype(BF16)
        qh, ql = _split(expand(qb_ref[0]))
        qb_s[...] = jnp.concatenate([qh, ql], axis=0)
        gate_s[...] = jnp.zeros_like(gate_s)
        mx_s[...] = jnp.zeros_like(mx_s)
        l_s[...] = jnp.zeros_like(l_s)

        def padded(ref):
            new_s[...] = jnp.zeros_like(new_s)
            new_s[0:n_tok, :] = ref[0]
            return new_s[...].astype(BF16)

        tok = (lax.broadcasted_iota(jnp.int32, (n_rows, NEW_PAD), 0) // N_HEADS)
        col = lax.broadcasted_iota(jnp.int32, (n_rows, NEW_PAD), 1)
        z = _dot_nt(qa_s[...], padded(kan_ref))
        strict = col < tok
        sp = _softplus(z)
        log_keep = jnp.where(strict, -sp, 0.0)
        hi, lo = _split(log_keep)
        tn = t2_ref[0:NEW_PAD, 0:NEW_PAD]
        between = _dot(hi, tn) + _dot(lo, tn)
        w = jnp.where(strict, jnp.exp((z - sp) + between), 0.0)
        run_s[...] = jnp.sum(log_keep, axis=1, keepdims=True)
        acca_s[...] = _dot(w.astype(BF16), padded(van_ref))
        s = _dot_nt(qb_s[0:n_rows, :], padded(kbn_ref)) * QK_SCALE
        s = s - slope * (tok - col).astype(F32)
        s = jnp.where(col <= tok, s, NEG_BIG)
        m_o = jnp.max(s, axis=1, keepdims=True)
        p = jnp.exp(s - m_o)
        mo_s[...] = m_o
        lo_s[...] = jnp.sum(p, axis=1, keepdims=True)
        acco_s[...] = _dot(p.astype(BF16), padded(vbn_ref))

    def pages(refs):
        return jnp.concatenate([r[0] for r in refs], axis=1)

    def ring_pages(ring, slot):
        return jnp.concatenate([ring[slot, p] for p in range(pps)], axis=1)

    def ring_copies(bb, jj, slot):
        first = pps * (n_steps - 1 - jj)
        copies = []
        for p in range(pps):
            pg = pt_ref[bb, first + p]
            copies.append(pltpu.make_async_copy(cka_hbm.at[pg], ka_ring.at[slot, p], ring_sem.at[0, slot]))
            copies.append(pltpu.make_async_copy(cva_hbm.at[pg], va_ring.at[slot, p], ring_sem.at[1, slot]))
        return copies

    slot = j & 1

    @pl.when((b == 0) & (j == 0))
    def _():
        for c in ring_copies(0, 0, 0):
            c.start()
        inflight_s[0] = 1

    last = j == n_steps - 1
    want_next = jnp.where(last, b + 1 < n_batch, live_s[0] == 1)

    @pl.when(want_next)
    def _():
        for c in ring_copies(jnp.where(last, b + 1, b), jnp.where(last, 0, j + 1), 1 - slot):
            c.start()

    inflight_s[1 - slot] = want_next.astype(jnp.int32)

    @pl.when(inflight_s[slot] == 1)
    def _():
        for c in ring_copies(b, j, slot):
            c.wait()

    @pl.when(live_s[0] == 1)
    def _():
        z = _dot(qa_s[...], ring_pages(ka_ring, slot).astype(BF16))
        sp = _softplus(z)
        log_keep = -sp
        t2 = t2_ref[...]
        run = run_s[...]
        betweens = [None] * bps
        for i in reversed(range(bps)):
            lk = log_keep[:, i * blk:(i + 1) * blk]
            betweens[i] = _suffix_sums(lk, t2) + run
            run = run + jnp.sum(lk, axis=1, keepdims=True)
        run_s[...] = run
        live_s[0] = (jnp.max(run) > RUN_DEAD_LN).astype(jnp.int32)
        w = jnp.exp((z - sp) + jnp.concatenate(betweens, axis=1))
        acca_s[...] = acca_s[...] + _dot_nt(w.astype(BF16), ring_pages(va_ring, slot).astype(BF16))

    kb_h, kb_l = _split(pages(kb_refs))
    vb = pages(vb_refs).astype(BF16)
    raw = _dot(qb_s[...], kb_h)
    raw_hh = raw[0:n_rows]
    raw3 = raw_hh + (raw[n_rows:] + _dot(qb_s[0:n_rows, :], kb_l))
    tok = lax.broadcasted_iota(jnp.int32, (n_rows, blk), 0) // N_HEADS
    col = lax.broadcasted_iota(jnp.int32, (n_rows, blk), 1)
    for i in range(bps):
        n = n_first + i
        cols = slice(i * blk, (i + 1) * blk)
        gate_n = jnp.sum(raw3[:, cols], axis=1, keepdims=True) * (1.0 / blk)
        dist = ((n_cached - n) * blk + tok - col).astype(F32)
        s = raw_hh[:, cols] * QK_SCALE - slope * dist
        m_n = jnp.max(s, axis=1, keepdims=True)
        p = jnp.exp(s - m_n)
        l_n = jnp.sum(p, axis=1, keepdims=True)
        here = lane == n
        gate_s[...] = jnp.where(here, gate_n, gate_s[...])
        mx_s[...] = jnp.where(here, m_n, mx_s[...])
        l_s[...] = jnp.where(here, l_n, l_s[...])
        accb_s[n] = _dot_nt(p.astype(BF16), vb[:, cols])

    @pl.when(j == n_steps - 1)
    def _():
        def fold(acc):
            kept = jnp.where(head_mask, acc, 0.0)
            return jnp.sum(kept.reshape(n_tok, N_HEADS, GROUP_WIDTH), axis=1)

        oa_ref[0] = fold(acca_s[...])
        cnt, ln = _rank_before_lanes(gate_s[...], n_cached)
        sel = (ln < n_cached) & (cnt < MOBA_TOPK)
        mx = jnp.where(sel, mx_s[...], NEG_BIG)
        m_o = mo_s[...]
        m_all = jnp.maximum(m_o, jnp.max(mx, axis=1, keepdims=True))
        c = jnp.where(sel, jnp.exp(mx - m_all), 0.0)
        c_o = jnp.exp(m_o - m_all)
        l_tot = c_o * lo_s[...] + jnp.sum(c * l_s[...], axis=1, keepdims=True)
        acc = c_o * acco_s[...]
        for nb in range(n_cached):
            acc = acc + c[:, nb:nb + 1] * accb_s[nb]
        ob_ref[0] = fold(acc / l_tot)


def _sample_attention(page_table, slope_rows, qa, qb, ka_new, va_new, kb_new, vb_new,
                      ck_sb, cv_sb, ck_mb, cv_mb):
    dbs, n_tok, _ = qa.shape
    n_pages = page_table.shape[1]
    page = ck_sb.shape[2]
    pps = PAGES_PER_STEP
    assert page == NEW_PAD and n_pages % pps == 0 and (pps * page) % MOBA_BLOCK == 0
    n_steps = n_pages // pps
    assert n_steps % 2 == 0, "the two-slot page ring alternates slots across consecutive grid steps"
    n_cached = n_pages * page // MOBA_BLOCK
    assert n_cached <= LANES
    n_rows = n_tok * N_HEADS

    tok_spec = pl.BlockSpec((1, n_tok, GROUP_WIDTH), lambda b, j, pt: (b, 0, 0))

    def page_spec(which):
        return pl.BlockSpec((1, GROUP_WIDTH, page),
                            lambda b, j, pt: (pt[b, pps * (n_steps - 1 - j) + which], 0, 0))

    grid_spec = pltpu.PrefetchScalarGridSpec(
        num_scalar_prefetch=1,
        grid=(dbs, n_steps),
        in_specs=[pl.BlockSpec((n_rows, LANES), lambda b, j, pt: (0, 0))] + [tok_spec] * 6
                 + [pl.BlockSpec(memory_space=pl.ANY)] * 2 + [page_spec(p) for p in range(pps)] * 2,
        out_specs=[tok_spec, tok_spec],
        scratch_shapes=[
            pltpu.VMEM((2 * MOBA_BLOCK, MOBA_BLOCK), BF16),
            pltpu.VMEM((n_rows, GROUP_WIDTH), BF16),
            pltpu.VMEM((2 * n_rows, GROUP_WIDTH), BF16),
            pltpu.VMEM((n_rows, 1), F32),
            pltpu.VMEM((n_rows, GROUP_WIDTH), F32),
            pltpu.VMEM((n_rows, LANES), F32),
            pltpu.VMEM((n_rows, LANES), F32),
            pltpu.VMEM((n_rows, LANES), F32),
            pltpu.VMEM((n_cached, n_rows, GROUP_WIDTH), F32),
            pltpu.VMEM((n_rows, 1), F32),
            pltpu.VMEM((n_rows, 1), F32),
            pltpu.VMEM((n_rows, GROUP_WIDTH), F32),
            pltpu.VMEM((NEW_PAD, GROUP_WIDTH), F32),
            pltpu.SMEM((1,), jnp.int32),
            pltpu.VMEM((2, pps, GROUP_WIDTH, page), F32),
            pltpu.VMEM((2, pps, GROUP_WIDTH, page), F32),
            pltpu.SemaphoreType.DMA((2, 2)),
            pltpu.SMEM((2,), jnp.int32),
        ],
    )
    out = jax.ShapeDtypeStruct((dbs, n_tok, GROUP_WIDTH), F32)
    caches = [ck_sb, cv_sb] + [c for c in (ck_mb, cv_mb) for _ in range(pps)]
    return pl.pallas_call(
        _sample_kernel,
        grid_spec=grid_spec,
        out_shape=[out, out],
        compiler_params=pltpu.CompilerParams(dimension_semantics=("arbitrary", "arbitrary"),
                                             vmem_limit_bytes=VMEM_LIMIT),
        name="sample_attn",
    )(page_table, slope_rows, qa, qb, ka_new, va_new, kb_new, vb_new, *caches)


def _mix_out_kernel(alpha, x_ref, oa_ref, ga_ref, ob_ref, gb_ref, w_ref, gain_ref, bias_ref, y_ref):
    h = jnp.concatenate([oa_ref[...] * jax.nn.silu(ga_ref[...]),
                         ob_ref[...] * jax.nn.silu(gb_ref[...])], axis=-1)
    out = _dot(h.astype(BF16), w_ref[...])
    y = alpha * x_ref[...] + out
    mu = jnp.mean(y, axis=-1, keepdims=True)
    var = jnp.mean(jnp.square(y - mu), axis=-1, keepdims=True)
    y_ref[...] = (y - mu) * lax.rsqrt(var + LN_EPS) * gain_ref[...] + bias_ref[...]


def _mix_out(x2d, oa, ga, ob, gb, w_out, gain, bias, alpha, block_rows):
    rows, d_model = x2d.shape
    g_spec = pl.BlockSpec((block_rows, GROUP_WIDTH), lambda i: (i, 0))
    x_spec = pl.BlockSpec((block_rows, d_model), lambda i: (i, 0))
    return pl.pallas_call(
        functools.partial(_mix_out_kernel, alpha),
        grid=(rows // block_rows,),
        in_specs=[x_spec, g_spec, g_spec, g_spec, g_spec,
                  _full_spec(w_out), _full_spec(gain), _full_spec(bias)],
        out_specs=x_spec,
        out_shape=jax.ShapeDtypeStruct(x2d.shape, F32),
        compiler_params=pltpu.CompilerParams(dimension_semantics=("arbitrary",),
                                             vmem_limit_bytes=VMEM_LIMIT),
        name="mix_out",
    )(x2d, oa, ga, ob, gb, w_out, gain, bias)


def kernel(x_prompt, x_sample, cache_k_sb, cache_v_sb, cache_k_moba, cache_v_moba, page_table,
           w_in, w_out, ln_gain, ln_bias):
    depth = w_in.shape[0]
    assert depth == 1, "single-layer trunk"
    bsz, seq, d_model = x_prompt.shape
    dbs, n_tok, _ = x_sample.shape
    alpha = (2.0 * depth) ** 0.25
    slopes = jnp.asarray([2.0 ** (-8.0 * (i + 1) / N_HEADS) for i in range(N_HEADS)], dtype=F32)
    slope_rows = jnp.broadcast_to(jnp.tile(slopes, n_tok)[:, None], (n_tok * N_HEADS, LANES))

    g = GROUP_WIDTH
    w = w_in[0]
    w_hi = w.astype(BF16)
    w_lo = (w[:, 4 * g:6 * g] - w_hi[:, 4 * g:6 * g].astype(F32)).astype(BF16)

    def col(a, c):
        return a[:, c * g:(c + 1) * g]

    w_row = jnp.concatenate([col(w_hi, c) for c in ROW_GROUPS], axis=1)
    w_row_lo = col(w_lo, 1)
    w_feat = jnp.concatenate([col(w_hi, c) for c in FEAT_GROUPS], axis=1).T
    w_feat_lo = col(w_lo, 0).T
    w_o = w_out[0].astype(BF16)
    gain = ln_gain[0][None, :]
    bias = ln_bias[0][None, :]

    def pages(c):
        return jnp.transpose(c[0], (0, 2, 3, 1)).reshape(c.shape[1], GROUP_WIDTH, c.shape[2])

    def heads_from_features(a):
        return jnp.transpose(a.reshape(bsz, N_HEADS, HEAD_DIM, seq), (0, 3, 1, 2))[None]

    def heads_from_rows(a):
        return a.reshape(1, dbs, n_tok, N_HEADS, HEAD_DIM)

    xp = x_prompt.reshape(bsz * seq, d_model)
    (ga, gb, ka_rows, kb_rows, kmean, qat, qbt, kat, vat, kbt, vbt) = _project_prompt(
        xp, w_row, w_row_lo, w_feat, w_feat_lo, bsz, 512)
    shp = (bsz, seq, GROUP_WIDTH)
    o_a = _sb_prompt(qat, ka_rows.reshape(shp), vat)
    o_b = _moba_prompt(slopes, qbt, kb_rows.reshape(shp), vbt, kmean.reshape(bsz, seq // MOBA_BLOCK, GROUP_WIDTH))
    y_p = _mix_out(xp, o_a.reshape(xp.shape[0], GROUP_WIDTH), ga, o_b.reshape(xp.shape[0], GROUP_WIDTH), gb,
                   w_o, gain, bias, alpha, 512)

    xs = x_sample.reshape(dbs * n_tok, d_model)
    sqa, ska, sva, sga, sqb, skb, svb, sgb = _project_rows(xs, w_hi, w_lo)
    sshp = (dbs, n_tok, GROUP_WIDTH)
    so_a, so_b = _sample_attention(
        page_table, slope_rows, sqa.reshape(sshp), sqb.reshape(sshp),
        ska.reshape(sshp), sva.reshape(sshp), skb.reshape(sshp), svb.reshape(sshp),
        pages(cache_k_sb), pages(cache_v_sb), pages(cache_k_moba), pages(cache_v_moba))
    y_s = _mix_out(xs, so_a.reshape(xs.shape[0], GROUP_WIDTH), sga, so_b.reshape(xs.shape[0], GROUP_WIDTH), sgb,
                   w_o, gain, bias, alpha, dbs * n_tok)

    return (y_p.reshape(x_prompt.shape), y_s.reshape(x_sample.shape),
            heads_from_features(kat), heads_from_features(vat),
            heads_from_features(kbt), heads_from_features(vbt),
            heads_from_rows(ska), heads_from_rows(sva), heads_from_rows(skb), heads_from_rows(svb))
```

```python
import functools
import math

import jax
import jax.numpy as jnp
from jax import lax
from jax.experimental import pallas as pl
from jax.experimental.pallas import tpu as pltpu

HEAD_DIM = 64
N_HEADS = 8
GROUP_WIDTH = N_HEADS * HEAD_DIM
HEAD_PAIR_WIDTH = 2 * HEAD_DIM
LANES = 128
MOBA_BLOCK = 256
MOBA_TOPK = 3
ATT_BLOCK = 256
QK_SCALE = HEAD_DIM ** -0.5
LOG2E = math.log2(math.e)
LN_EPS = 1e-5
NEG_BIG = -1e30
SOFTPLUS2_CLAMP = 100.0
RUN_DEAD = -160.0
RUN_DEAD_LN = -112.0
NEW_PAD = 128
PAGES_PER_STEP = 8
ONES_ROWS = 16
VMEM_LIMIT = 48 * 1024 * 1024

AUG_ONE = 0
AUG_KEYPOS = 3
AUG_BLOCK = 8

F32 = jnp.float32
BF16 = jnp.bfloat16


def _dot(a, b):
    return jnp.dot(a, b, preferred_element_type=F32)


def _dot_nt(a, b):
    return lax.dot_general(a, b, (((1,), (1,)), ((), ())), preferred_element_type=F32)


def _split(x):
    hi = x.astype(BF16)
    lo = (x - hi.astype(F32)).astype(BF16)
    return hi, lo


def _split3(x):
    hi = x.astype(BF16)
    r = x - hi.astype(F32)
    mid = r.astype(BF16)
    lo = (r - mid.astype(F32)).astype(BF16)
    return hi, mid, lo


def _dot_3pass(a, b):
    ah, al = _split(a)
    bh, bl = _split(b)
    return _dot(ah, bh) + (_dot(ah, bl) + _dot(al, bh))


def _softplus(z):
    return jnp.maximum(z, 0.0) + jnp.log(1.0 + jnp.exp(-jnp.abs(z)))


def _softplus2(z):
    return jnp.maximum(jnp.log2(1.0 + jnp.exp2(jnp.minimum(z, SOFTPLUS2_CLAMP))), z)


def _full_spec(a, single_buffer=False):
    mode = pl.Buffered(1) if single_buffer else None
    return pl.BlockSpec(a.shape, lambda *_: (0,) * a.ndim, pipeline_mode=mode)


def _proj_rows_kernel(x_ref, w_ref, wlo_ref, *out_refs):
    xh, xl = _split(x_ref[...])
    g = GROUP_WIDTH
    for c, o_ref in enumerate(out_refs):
        wc = w_ref[:, c * g:(c + 1) * g]
        out = _dot(xh, wc)
        if c in (4, 5):
            out = out + (_dot(xh, wlo_ref[:, (c - 4) * g:(c - 3) * g]) + _dot(xl, wc))
        o_ref[...] = out


def _project_rows(x2d, w_hi, w_lo):
    rows, _ = x2d.shape
    out = jax.ShapeDtypeStruct((rows, GROUP_WIDTH), F32)
    return pl.pallas_call(
        _proj_rows_kernel,
        grid=(1,),
        in_specs=[_full_spec(x2d), _full_spec(w_hi), _full_spec(w_lo)],
        out_specs=[pl.BlockSpec((rows, GROUP_WIDTH), lambda i: (0, 0))] * 8,
        out_shape=[out] * 8,
        compiler_params=pltpu.CompilerParams(dimension_semantics=("arbitrary",),
                                             vmem_limit_bytes=VMEM_LIMIT),
        name="proj_rows",
    )(x2d, w_hi, w_lo)


ROW_GROUPS = (3, 7, 1, 5)
FEAT_GROUPS = (0, 4, 1, 2, 5, 6)
FEAT_3PASS = (1,)


def _proj_prompt_kernel(x_ref, wr_ref, wrlo_ref, wf_ref, wflo_ref,
                        ga_ref, gb_ref, kar_ref, kbr_ref, kmean_ref,
                        qat_ref, qbt_ref, kat_ref, vat_ref, kbt_ref, vbt_ref):
    x = x_ref[...]
    xh, xl = _split(x)
    g = GROUP_WIDTH
    for c, o_ref in enumerate((ga_ref, gb_ref, kar_ref, kbr_ref)):
        o_ref[...] = _dot(xh, wr_ref[:, c * g:(c + 1) * g]).astype(o_ref.dtype)
    lo_slot = 0
    for c, o_ref in enumerate((qat_ref, qbt_ref, kat_ref, vat_ref, kbt_ref, vbt_ref)):
        wc = wf_ref[c * g:(c + 1) * g, :]
        out = _dot_nt(wc, xh)
        if c in FEAT_3PASS:
            out = out + (_dot_nt(wflo_ref[lo_slot * g:(lo_slot + 1) * g, :], xh) + _dot_nt(wc, xl))
            lo_slot += 1
        o_ref[0] = out
    n_blk = x.shape[0] // MOBA_BLOCK
    row8 = lax.broadcasted_iota(jnp.int32, (8, x.shape[1]), 0)
    xbar = jnp.zeros((8, x.shape[1]), F32)
    for i in range(n_blk):
        mean_i = jnp.sum(x[i * MOBA_BLOCK:(i + 1) * MOBA_BLOCK], axis=0, keepdims=True) * (1.0 / MOBA_BLOCK)
        xbar = jnp.where(row8 == i, mean_i, xbar)
    bh, bl = _split(xbar)
    wk = wr_ref[:, 3 * g:4 * g]
    km = _dot(bh, wk) + (_dot(bh, wrlo_ref[...]) + _dot(bl, wk))
    kmean_ref[0] = km[0:n_blk]


def _project_prompt(x2d, w_row, w_row_lo, w_feat, w_feat_lo, batch, block_rows):
    rows, d_model = x2d.shape
    seq = rows // batch
    per_seq = seq // block_rows
    n_blk = block_rows // MOBA_BLOCK
    row_spec = pl.BlockSpec((block_rows, GROUP_WIDTH), lambda i: (i, 0))
    feat_spec = pl.BlockSpec((1, GROUP_WIDTH, block_rows), lambda i: (i // per_seq, 0, i % per_seq))
    row_bf16 = jax.ShapeDtypeStruct((rows, GROUP_WIDTH), BF16)
    feat = jax.ShapeDtypeStruct((batch, GROUP_WIDTH, seq), F32)
    kmean = jax.ShapeDtypeStruct((rows // block_rows, n_blk, GROUP_WIDTH), F32)
    return pl.pallas_call(
        _proj_prompt_kernel,
        grid=(rows // block_rows,),
        in_specs=[pl.BlockSpec((block_rows, d_model), lambda i: (i, 0)),
                  _full_spec(w_row, True), _full_spec(w_row_lo, True),
                  _full_spec(w_feat, True), _full_spec(w_feat_lo, True)],
        out_specs=[row_spec] * 4 + [pl.BlockSpec((1, n_blk, GROUP_WIDTH), lambda i: (i, 0, 0))] + [feat_spec] * 6,
        out_shape=[row_bf16] * 4 + [kmean] + [feat] * 6,
        compiler_params=pltpu.CompilerParams(dimension_semantics=("arbitrary",),
                                             vmem_limit_bytes=VMEM_LIMIT),
        name="proj_prompt",
    )(x2d, w_row, w_row_lo, w_feat, w_feat_lo)


def _pair_columns(qt, tq):
    row = lax.broadcasted_iota(jnp.int32, qt.shape, 0)
    zero = jnp.zeros_like(qt)
    return jnp.concatenate([jnp.where(row < HEAD_DIM, qt, zero),
                            jnp.where(row >= HEAD_DIM, qt, zero)], axis=1)


def _pair_merge_rows(acc_t, tq):
    row = lax.broadcasted_iota(jnp.int32, (HEAD_PAIR_WIDTH, tq), 0)
    return jnp.where(row < HEAD_DIM, acc_t[:, :tq], acc_t[:, tq:]).T


def _blk(kj):
    return pl.ds(pl.multiple_of(kj * ATT_BLOCK, ATT_BLOCK), ATT_BLOCK)


def _head_pair_specs(seq):
    feat_spec = pl.BlockSpec((1, HEAD_PAIR_WIDTH, seq), lambda b, h: (b, h, 0))
    rows_spec = pl.BlockSpec((1, seq, HEAD_PAIR_WIDTH), lambda b, h: (b, 0, h))
    return feat_spec, rows_spec


def _sb_prompt_kernel(qt_ref, k_ref, vt_ref, o_ref, u_ref, vtb_ref, run_s, acc_s):
    b, hp = pl.program_id(0), pl.program_id(1)
    tq = ATT_BLOCK
    half = tq // 2
    u_rows = half + ONES_ROWS
    n_blocks = k_ref.shape[1] // tq

    @pl.when((b == 0) & (hp == 0))
    def _():
        s_i = lax.broadcasted_iota(jnp.int32, (u_rows, tq), 0)
        j_i = lax.broadcasted_iota(jnp.int32, (u_rows, tq), 1) & (half - 1)
        u_ref[...] = jnp.where((j_i >= s_i) | (s_i >= half), -1.0, 0.0).astype(BF16)

    vtb_ref[...] = vt_ref[0].astype(BF16)
    refs = (qt_ref, k_ref, o_ref, u_ref, vtb_ref, run_s, acc_s)

    first, second = _SbQueryBlock(refs, 0, 0), _SbQueryBlock(refs, 1, 1)
    first.reset()
    second.reset()
    first.single(0, True)
    second.pair(0, True)
    first.store()
    second.store()

    def two_blocks(i, carry):
        blocks = (_SbQueryBlock(refs, 2 * i, 0), _SbQueryBlock(refs, 2 * i + 1, 1))
        for blk in blocks:
            blk.reset()
        work = [(blk, t) for blk in blocks for t in (0, 1)]
        scored = [blk.scores(t, t == 0) for blk, t in work]
        locs = [blk.suffix(halves) for (blk, _), (_, halves) in zip(work, scored)]
        for (blk, t), (z, _), loc in zip(work, scored, locs):
            blk.weigh(t, z, loc)
        for blk in blocks:
            blk.earlier_blocks()
            blk.store()
        return carry

    lax.fori_loop(1, n_blocks // 2, two_blocks, 0)


class _SbQueryBlock:
    def __init__(self, refs, qi, slot):
        (self.qt_ref, self.k_ref, self.o_ref, self.u_ref, self.vtb_ref, self.run_s, self.acc_s) = refs
        self.qi, self.slot = qi, slot
        q = self.qt_ref[0, :, _blk(qi)] * (QK_SCALE * LOG2E)
        self.q_cols = _pair_columns(q.astype(BF16), ATT_BLOCK)

    def reset(self):
        self.run_s[self.slot] = jnp.zeros(self.run_s.shape[1:], F32)
        self.acc_s[self.slot] = jnp.zeros(self.acc_s.shape[1:], F32)

    def scores(self, t, diagonal):
        tq, half = ATT_BLOCK, ATT_BLOCK // 2
        z = _dot(self.k_ref[0, _blk(self.qi - t), :], self.q_cols)
        sp = _softplus2(z)
        if diagonal:
            key = lax.broadcasted_iota(jnp.int32, z.shape, 0)
            qry = lax.broadcasted_iota(jnp.int32, z.shape, 1) & (tq - 1)
            strict = key < qry
            sp = jnp.where(strict, sp, 0.0)
            z = jnp.where(strict, z, NEG_BIG)
        hi, lo = _split(sp)
        halves = [jnp.concatenate([hi[h * half:(h + 1) * half], lo[h * half:(h + 1) * half]], axis=0)
                  for h in range(2)]
        return z, halves

    def suffix(self, halves):
        neg_u = self.u_ref[...]
        return [_dot(neg_u, hl) for hl in halves]

    def weigh(self, t, z, locs):
        tq, half = ATT_BLOCK, ATT_BLOCK // 2
        run = self.run_s[self.slot]
        tot_first = locs[0][half:half + 1]
        tot_second = locs[1][half:half + 1]
        first = (z[0:half] + locs[0][0:half]) + (run + tot_second)
        second = (z[half:tq] + locs[1][0:half]) + run
        w = jnp.exp2(jnp.concatenate([first, second], axis=0)).astype(BF16)
        self.run_s[self.slot] = run + (tot_first + tot_second)
        self.acc_s[self.slot] = self.acc_s[self.slot] + _dot(self.vtb_ref[:, _blk(self.qi - t)], w)

    def single(self, t, diagonal):
        z, halves = self.scores(t, diagonal)
        self.weigh(t, z, self.suffix(halves))

    def pair(self, t, diagonal):
        z0, h0 = self.scores(t, diagonal)
        z1, h1 = self.scores(t + 1, False)
        l0 = self.suffix(h0)
        l1 = self.suffix(h1)
        self.weigh(t, z0, l0)
        self.weigh(t + 1, z1, l1)

    def alive(self):
        return jnp.max(self.run_s[self.slot]) > RUN_DEAD

    def earlier_blocks(self):
        qi = self.qi

        def more_pairs(c):
            return (c[0] + 1 <= qi) & c[1]

        def next_pair(c):
            self.pair(c[0], False)
            return c[0] + 2, self.alive()

        t, live = lax.while_loop(more_pairs, next_pair, (jnp.int32(2), self.alive()))

        @pl.when((t == qi) & live)
        def _():
            self.single(t, False)

    def store(self):
        out = _pair_merge_rows(self.acc_s[self.slot], ATT_BLOCK)
        self.o_ref[0, _blk(self.qi), :] = out.astype(self.o_ref.dtype)


def _sb_prompt(qt, k_rows, vt):
    bsz, _, seq = qt.shape
    tq = ATT_BLOCK
    assert seq % (2 * tq) == 0, "query blocks are processed two at a time"
    feat_spec, rows_spec = _head_pair_specs(seq)
    return pl.pallas_call(
        _sb_prompt_kernel,
        grid=(bsz, GROUP_WIDTH // HEAD_PAIR_WIDTH),
        in_specs=[feat_spec, rows_spec, feat_spec],
        out_specs=rows_spec,
        out_shape=jax.ShapeDtypeStruct((bsz, seq, GROUP_WIDTH), BF16),
        scratch_shapes=[pltpu.VMEM((tq // 2 + ONES_ROWS, tq), BF16),
                        pltpu.VMEM((HEAD_PAIR_WIDTH, seq), BF16),
                        pltpu.VMEM((2, 1, 2 * tq), F32),
                        pltpu.VMEM((2, HEAD_PAIR_WIDTH, 2 * tq), F32)],
        compiler_params=pltpu.CompilerParams(dimension_semantics=("arbitrary",) * 2,
                                             vmem_limit_bytes=VMEM_LIMIT),
        name="sb_prompt",
    )(qt, k_rows, vt)


def _rank_before_rows(g, n_valid, n_cand):
    blk = lax.broadcasted_iota(jnp.int32, g.shape, 0)
    cnt = jnp.zeros(g.shape, jnp.int32)
    for m in range(n_cand):
        gm = g[m:m + 1, :]
        beats = (gm > g) | ((gm == g) & (m < blk))
        cnt = cnt + jnp.where(beats & (m < n_valid), 1, 0)
    return cnt, blk


def _block_pairs(n_blocks):
    return [(q, 0) for q in range(n_blocks)] + [(q, t) for q in range(n_blocks) for t in range(1, q + 1)]


def _moba_prompt_kernel(slopes_ref, qt_ref, k_ref, vt_ref, kmean_ref, o_ref,
                        kaug_ref, vaug_ref, qaug_s, mask_s, s_s, p_s, pv_s, top_s, m_all, acc_all,
                        pair_q, pair_t):
    b, hp = pl.program_id(0), pl.program_id(1)
    tq = ATT_BLOCK
    seq = k_ref.shape[1]
    n_blocks = seq // MOBA_BLOCK
    pairs = _block_pairs(n_blocks)
    n_pairs = len(pairs)

    @pl.when((b == 0) & (hp == 0))
    def _():
        for i, (q, t) in enumerate(pairs):
            pair_q[i] = q
            pair_t[i] = t
        key = lax.broadcasted_iota(jnp.int32, (tq, 2 * tq), 0)
        qry = lax.broadcasted_iota(jnp.int32, (tq, 2 * tq), 1) & (tq - 1)
        mask_s[...] = jnp.where(key <= qry, 0.0, NEG_BIG)

    kaug_ref[:, 0:HEAD_PAIR_WIDTH] = k_ref[0]
    lane = lax.broadcasted_iota(jnp.int32, (MOBA_BLOCK, LANES), 1)
    key = lax.broadcasted_iota(jnp.int32, (MOBA_BLOCK, LANES), 0).astype(F32)
    base = jnp.where(lane < AUG_ONE + 3, 1.0,
                     jnp.where(lane < AUG_KEYPOS + 3, key, 0.0))
    for n in range(n_blocks):
        kaug_ref[n * MOBA_BLOCK:(n + 1) * MOBA_BLOCK, HEAD_PAIR_WIDTH:] = (
            jnp.where(lane == AUG_BLOCK + n, 1.0, base).astype(BF16))
    vaug_ref[0:HEAD_PAIR_WIDTH, :] = vt_ref[0].astype(BF16)
    vaug_ref[HEAD_PAIR_WIDTH:, :] = jnp.ones((ONES_ROWS, seq), BF16)

    lane = lax.broadcasted_iota(jnp.int32, (1, 2 * tq), 1)
    slope = jnp.where(lane < tq, slopes_ref[2 * hp], slopes_ref[2 * hp + 1]) * LOG2E
    q_off = (lane & (tq - 1)).astype(F32)
    row8 = lax.broadcasted_iota(jnp.int32, (8, 2 * tq), 0)
    small = jnp.zeros((8, 2 * tq), F32)
    for first, terms in ((AUG_ONE, _split3(-slope * q_off)), (AUG_KEYPOS, _split3(slope))):
        for r, v in enumerate(terms):
            small = jnp.where(row8 == first + r, v.astype(F32), small)
    kmean = kmean_ref[0]

    def prepare(qb, carry):
        qt = qt_ref[0, :, _blk(qb)]
        q_cols = _pair_columns(qt, tq)
        gate = _dot_3pass(kmean, q_cols)
        cnt, blk = _rank_before_rows(gate, qb, n_blocks)
        chosen = ((blk < qb) & (cnt < MOBA_TOPK)) | (blk == qb)
        choice_bias = jnp.where(chosen, 0.0, NEG_BIG)
        qaug_s[qb] = jnp.concatenate(
            [q_cols * (QK_SCALE * LOG2E), small, choice_bias,
             jnp.zeros((LANES - 8 - n_blocks, 2 * tq), F32)], axis=0).astype(BF16)
        m_all[qb] = jnp.full(m_all.shape[1:], NEG_BIG, F32)
        acc_all[qb] = jnp.zeros(acc_all.shape[1:], F32)
        return carry

    lax.fori_loop(0, n_blocks, prepare, 0)

    def score(i, slot):
        q, t = pair_q[i], pair_t[i]
        s_s[slot & 1] = _dot(kaug_ref[_blk(q - t), :], qaug_s[q])

    def probs(i, slot):
        t = pair_t[i]
        s = s_s[slot & 1]
        if isinstance(i, int) and i < n_blocks:
            s = s + mask_s[...]
        top = jnp.max(s, axis=0, keepdims=True)
        p_s[slot & 1] = jnp.exp2(s - top).astype(BF16)
        top_s[slot & 3, 0:1, :] = top - slope * (t * MOBA_BLOCK).astype(F32)

    def value(i, slot):
        q, t = pair_q[i], pair_t[i]
        pv_s[slot & 1] = _dot(vaug_ref[:, _blk(q - t)], p_s[slot & 1])

    def merge(i, slot):
        q = pair_q[i]
        top = top_s[slot & 3, 0:1, :]
        m_run = m_all[q, 0:1, :]
        m_new = jnp.maximum(m_run, top)
        acc_all[q] = acc_all[q] * jnp.exp2(m_run - m_new) + pv_s[slot & 1] * jnp.exp2(top - m_new)
        m_all[q, 0:1, :] = m_new

    def tick(i, slot, first=0, last=3):
        stages = (score, probs, value, merge)
        for k in range(last, first - 1, -1):
            stages[k](i - k, slot - k)

    depth = 3
    unroll = 4
    n_static = depth + unroll * -(-(n_blocks + 1 - depth) // unroll)
    for i in range(n_static):
        tick(i, i, last=min(i, depth))
    n_groups = (n_pairs - n_static) // unroll

    def steady(g, carry):
        base = n_static + g * unroll
        for u in range(unroll):
            tick(base + u, n_static + u)
        return carry

    lax.fori_loop(0, n_groups, steady, 0)
    for i in range(n_static + n_groups * unroll, n_pairs):
        tick(i, i)
    for k in range(1, depth + 1):
        tick(n_pairs - 1 + k, n_pairs - 1 + k, first=k)

    def finish(qb, carry):
        acc = acc_all[qb]
        out_t = acc[0:HEAD_PAIR_WIDTH] / acc[HEAD_PAIR_WIDTH:HEAD_PAIR_WIDTH + 1]
        o_ref[0, _blk(qb), :] = _pair_merge_rows(out_t, tq).astype(o_ref.dtype)
        return carry

    lax.fori_loop(0, n_blocks, finish, 0)


def _moba_prompt(slopes, qt, k_rows, vt, kmean):
    bsz, _, seq = qt.shape
    n_blocks = seq // MOBA_BLOCK
    assert AUG_BLOCK + n_blocks <= LANES
    tq = ATT_BLOCK
    n_pairs = len(_block_pairs(n_blocks))
    acc_rows = HEAD_PAIR_WIDTH + ONES_ROWS
    feat_spec, rows_spec = _head_pair_specs(seq)
    return pl.pallas_call(
        _moba_prompt_kernel,
        grid=(bsz, GROUP_WIDTH // HEAD_PAIR_WIDTH),
        in_specs=[pl.BlockSpec(memory_space=pltpu.SMEM), feat_spec, rows_spec, feat_spec,
                  pl.BlockSpec((1, n_blocks, HEAD_PAIR_WIDTH), lambda b, h: (b, 0, h))],
        out_specs=rows_spec,
        out_shape=jax.ShapeDtypeStruct((bsz, seq, GROUP_WIDTH), BF16),
        scratch_shapes=[pltpu.VMEM((seq, HEAD_PAIR_WIDTH + LANES), BF16),
                        pltpu.VMEM((acc_rows, seq), BF16),
                        pltpu.VMEM((n_blocks, 2 * LANES, 2 * tq), BF16),
                        pltpu.VMEM((tq, 2 * tq), F32),
                        pltpu.VMEM((2, tq, 2 * tq), F32),
                        pltpu.VMEM((2, tq, 2 * tq), BF16),
                        pltpu.VMEM((2, acc_rows, 2 * tq), F32),
                        pltpu.VMEM((4, 8, 2 * tq), F32),
                        pltpu.VMEM((n_blocks, 8, 2 * tq), F32),
                        pltpu.VMEM((n_blocks, acc_rows, 2 * tq), F32),
                        pltpu.SMEM((n_pairs,), jnp.int32),
                        pltpu.SMEM((n_pairs,), jnp.int32)],
        compiler_params=pltpu.CompilerParams(dimension_semantics=("arbitrary",) * 2,
                                             vmem_limit_bytes=VMEM_LIMIT),
        name="moba_prompt",
    )(slopes, qt, k_rows, vt, kmean)


def _suffix_matrix(n):
    j = lax.broadcasted_iota(jnp.int32, (n, n), 0)
    s = lax.broadcasted_iota(jnp.int32, (n, n), 1)
    return jnp.where(j > s, 1.0, 0.0).astype(BF16)


def _suffix_sums(x, t2):
    hi, lo = _split(x)
    return _dot(jnp.concatenate([hi, lo], axis=1), t2)


def _rank_before_lanes(g, n_cand):
    lane = lax.broadcasted_iota(jnp.int32, g.shape, 1)
    cnt = jnp.zeros(g.shape, jnp.int32)
    for m in range(n_cand):
        gm = g[:, m:m + 1]
        beats = (gm > g) | ((gm == g) & (m < lane))
        cnt = cnt + jnp.where(beats, 1, 0)
    return cnt, lane


def _sample_kernel(pt_ref, slope_ref, qa_ref, qb_ref, kan_ref, van_ref, kbn_ref, vbn_ref,
                   cka_hbm, cva_hbm, *rest):
    pps = PAGES_PER_STEP
    kb_refs, vb_refs = (rest[i * pps:(i + 1) * pps] for i in range(2))
    (oa_ref, ob_ref, t2_ref, qa_s, qb_s, run_s, acca_s, gate_s, mx_s, l_s, accb_s,
     mo_s, lo_s, acco_s, new_s, live_s, ka_ring, va_ring, ring_sem, inflight_s) = rest[2 * pps:]
    b, j = pl.program_id(0), pl.program_id(1)
    n_batch = pl.num_programs(0)
    n_steps = pl.num_programs(1)
    blk = MOBA_BLOCK
    bps = pps * NEW_PAD // blk
    n_cached = accb_s.shape[0]
    n_first = (n_steps - 1 - j) * bps
    n_tok = qa_ref.shape[1]
    n_rows = n_tok * N_HEADS

    row = lax.broadcasted_iota(jnp.int32, (n_rows, GROUP_WIDTH), 0)
    lane_w = lax.broadcasted_iota(jnp.int32, (n_rows, GROUP_WIDTH), 1)
    head_mask = (lane_w // HEAD_DIM) == (row % N_HEADS)
    slope = slope_ref[:, 0:1]
    lane = lax.broadcasted_iota(jnp.int32, (n_rows, LANES), 1)

    @pl.when(j == 0)
    def _():
        t = _suffix_matrix(blk)
        t2_ref[0:blk, :] = t
        t2_ref[blk:, :] = t
        live_s[0] = 1

        def expand(q):
            rep = jnp.concatenate([jnp.broadcast_to(q[t:t + 1], (N_HEADS, GROUP_WIDTH))
                                   for t in range(n_tok)], axis=0)
            return jnp.where(head_mask, rep, 0.0)

        qa_s[...] = (expand(qa_ref[0]) * QK_SCALE).astype(BF16)
        qh, ql = _split(expand(qb_ref[0]))
        qb_s[...] = jnp.concatenate([qh, ql], axis=0)
        gate_s[...] = jnp.zeros_like(gate_s)
        mx_s[...] = jnp.zeros_like(mx_s)
        l_s[...] = jnp.zeros_like(l_s)

        def padded(ref):
            new_s[...] = jnp.zeros_like(new_s)
            new_s[0:n_tok, :] = ref[0]
            return new_s[...].astype(BF16)

        tok = (lax.broadcasted_iota(jnp.int32, (n_rows, NEW_PAD), 0) // N_HEADS)
        col = lax.broadcasted_iota(jnp.int32, (n_rows, NEW_PAD), 1)
        z = _dot_nt(qa_s[...], padded(kan_ref))
        strict = col < tok
        sp = _softplus(z)
        log_keep = jnp.where(strict, -sp, 0.0)
        hi, lo = _split(log_keep)
        tn = t2_ref[0:NEW_PAD, 0:NEW_PAD]
        between = _dot(hi, tn) + _dot(lo, tn)
        w = jnp.where(strict, jnp.exp((z - sp) + between), 0.0)
        run_s[...] = jnp.sum(log_keep, axis=1, keepdims=True)
        acca_s[...] = _dot(w.astype(BF16), padded(van_ref))
        s = _dot_nt(qb_s[0:n_rows, :], padded(kbn_ref)) * QK_SCALE
        s = s - slope * (tok - col).astype(F32)
        s = jnp.where(col <= tok, s, NEG_BIG)
        m_o = jnp.max(s, axis=1, keepdims=True)
        p = jnp.exp(s - m_o)
        mo_s[...] = m_o
        lo_s[...] = jnp.sum(p, axis=1, keepdims=True)
        acco_s[...] = _dot(p.astype(BF16), padded(vbn_ref))

    def pages(refs):
        return jnp.concatenate([r[0] for r in refs], axis=1)

    def ring_pages(ring, slot):
        return jnp.concatenate([ring[slot, p] for p in range(pps)], axis=1)

    def ring_copies(bb, jj, slot):
        first = pps * (n_steps - 1 - jj)
        copies = []
        for p in range(pps):
            pg = pt_ref[bb, first + p]
            copies.append(pltpu.make_async_copy(cka_hbm.at[pg], ka_ring.at[slot, p], ring_sem.at[0, slot]))
            copies.append(pltpu.make_async_copy(cva_hbm.at[pg], va_ring.at[slot, p], ring_sem.at[1, slot]))
        return copies

    slot = j & 1

    @pl.when((b == 0) & (j == 0))
    def _():
        for c in ring_copies(0, 0, 0):
            c.start()
        inflight_s[0] = 1

    last = j == n_steps - 1
    want_next = jnp.where(last, b + 1 < n_batch, live_s[0] == 1)

    @pl.when(want_next)
    def _():
        for c in ring_copies(jnp.where(last, b + 1, b), jnp.where(last, 0, j + 1), 1 - slot):
            c.start()

    inflight_s[1 - slot] = want_next.astype(jnp.int32)

    @pl.when(inflight_s[slot] == 1)
    def _():
        for c in ring_copies(b, j, slot):
            c.wait()

    @pl.when(live_s[0] == 1)
    def _():
        z = _dot(qa_s[...], ring_pages(ka_ring, slot).astype(BF16))
        sp = _softplus(z)
        log_keep = -sp
        t2 = t2_ref[...]
        run = run_s[...]
        betweens = [None] * bps
        for i in reversed(range(bps)):
            lk = log_keep[:, i * blk:(i + 1) * blk]
            betweens[i] = _suffix_sums(lk, t2) + run
            run = run + jnp.sum(lk, axis=1, keepdims=True)
        run_s[...] = run
        live_s[0] = (jnp.max(run) > RUN_DEAD_LN).astype(jnp.int32)
        w = jnp.exp((z - sp) + jnp.concatenate(betweens, axis=1))
        acca_s[...] = acca_s[...] + _dot_nt(w.astype(BF16), ring_pages(va_ring, slot).astype(BF16))

    kb_h, kb_l = _split(pages(kb_refs))
    vb = pages(vb_refs).astype(BF16)
    raw = _dot(qb_s[...], kb_h)
    raw_hh = raw[0:n_rows]
    raw3 = raw_hh + (raw[n_rows:] + _dot(qb_s[0:n_rows, :], kb_l))
    tok = lax.broadcasted_iota(jnp.int32, (n_rows, blk), 0) // N_HEADS
    col = lax.broadcasted_iota(jnp.int32, (n_rows, blk), 1)
    for i in range(bps):
        n = n_first + i
        cols = slice(i * blk, (i + 1) * blk)
        gate_n = jnp.sum(raw3[:, cols], axis=1, keepdims=True) * (1.0 / blk)
        dist = ((n_cached - n) * blk + tok - col).astype(F32)
        s = raw_hh[:, cols] * QK_SCALE - slope * dist
        m_n = jnp.max(s, axis=1, keepdims=True)
        p = jnp.exp(s - m_n)
        l_n = jnp.sum(p, axis=1, keepdims=True)
        here = lane == n
        gate_s[...] = jnp.where(here, gate_n, gate_s[...])
        mx_s[...] = jnp.where(here, m_n, mx_s[...])
        l_s[...] = jnp.where(here, l_n, l_s[...])
        accb_s[n] = _dot_nt(p.astype(BF16), vb[:, cols])

    @pl.when(j == n_steps - 1)
    def _():
        def fold(acc):
            kept = jnp.where(head_mask, acc, 0.0)
            return jnp.sum(kept.reshape(n_tok, N_HEADS, GROUP_WIDTH), axis=1)

        oa_ref[0] = fold(acca_s[...])
        cnt, ln = _rank_before_lanes(gate_s[...], n_cached)
        sel = (ln < n_cached) & (cnt < MOBA_TOPK)
        mx = jnp.where(sel, mx_s[...], NEG_BIG)
        m_o = mo_s[...]
        m_all = jnp.maximum(m_o, jnp.max(mx, axis=1, keepdims=True))
        c = jnp.where(sel, jnp.exp(mx - m_all), 0.0)
        c_o = jnp.exp(m_o - m_all)
        l_tot = c_o * lo_s[...] + jnp.sum(c * l_s[...], axis=1, keepdims=True)
        acc = c_o * acco_s[...]
        for nb in range(n_cached):
            acc = acc + c[:, nb:nb + 1] * accb_s[nb]
        ob_ref[0] = fold(acc / l_tot)


def _sample_attention(page_table, slope_rows, qa, qb, ka_new, va_new, kb_new, vb_new,
                      ck_sb, cv_sb, ck_mb, cv_mb):
    dbs, n_tok, _ = qa.shape
    n_pages = page_table.shape[1]
    page = ck_sb.shape[2]
    pps = PAGES_PER_STEP
    assert page == NEW_PAD and n_pages % pps == 0 and (pps * page) % MOBA_BLOCK == 0
    n_steps = n_pages // pps
    assert n_steps % 2 == 0, "the two-slot page ring alternates slots across consecutive grid steps"
    n_cached = n_pages * page // MOBA_BLOCK
    assert n_cached <= LANES
    n_rows = n_tok * N_HEADS

    tok_spec = pl.BlockSpec((1, n_tok, GROUP_WIDTH), lambda b, j, pt: (b, 0, 0))

    def page_spec(which):
        return pl.BlockSpec((1, GROUP_WIDTH, page),
                            lambda b, j, pt: (pt[b, pps * (n_steps - 1 - j) + which], 0, 0))

    grid_spec = pltpu.PrefetchScalarGridSpec(
        num_scalar_prefetch=1,
        grid=(dbs, n_steps),
        in_specs=[pl.BlockSpec((n_rows, LANES), lambda b, j, pt: (0, 0))] + [tok_spec] * 6
                 + [pl.BlockSpec(memory_space=pl.ANY)] * 2 + [page_spec(p) for p in range(pps)] * 2,
        out_specs=[tok_spec, tok_spec],
        scratch_shapes=[
            pltpu.VMEM((2 * MOBA_BLOCK, MOBA_BLOCK), BF16),
            pltpu.VMEM((n_rows, GROUP_WIDTH), BF16),
            pltpu.VMEM((2 * n_rows, GROUP_WIDTH), BF16),
            pltpu.VMEM((n_rows, 1), F32),
            pltpu.VMEM((n_rows, GROUP_WIDTH), F32),
            pltpu.VMEM((n_rows, LANES), F32),
            pltpu.VMEM((n_rows, LANES), F32),
            pltpu.VMEM((n_rows, LANES), F32),
            pltpu.VMEM((n_cached, n_rows, GROUP_WIDTH), F32),
            pltpu.VMEM((n_rows, 1), F32),
            pltpu.VMEM((n_rows, 1), F32),
            pltpu.VMEM((n_rows, GROUP_WIDTH), F32),
            pltpu.VMEM((NEW_PAD, GROUP_WIDTH), F32),
            pltpu.SMEM((1,), jnp.int32),
            pltpu.VMEM((2, pps, GROUP_WIDTH, page), F32),
            pltpu.VMEM((2, pps, GROUP_WIDTH, page), F32),
            pltpu.SemaphoreType.DMA((2, 2)),
            pltpu.SMEM((2,), jnp.int32),
        ],
    )
    out = jax.ShapeDtypeStruct((dbs, n_tok, GROUP_WIDTH), F32)
    caches = [ck_sb, cv_sb] + [c for c in (ck_mb, cv_mb) for _ in range(pps)]
    return pl.pallas_call(
        _sample_kernel,
        grid_spec=grid_spec,
        out_shape=[out, out],
        compiler_params=pltpu.CompilerParams(dimension_semantics=("arbitrary", "arbitrary"),
                                             vmem_limit_bytes=VMEM_LIMIT),
        name="sample_attn",
    )(page_table, slope_rows, qa, qb, ka_new, va_new, kb_new, vb_new, *caches)


def _mix_out_kernel(alpha, x_ref, oa_ref, ga_ref, ob_ref, gb_ref, w_ref, gain_ref, bias_ref, y_ref):
    def gated(o_ref, g_ref):
        return o_ref[...].astype(F32) * jax.nn.silu(g_ref[...].astype(F32))

    h = jnp.concatenate([gated(oa_ref, ga_ref), gated(ob_ref, gb_ref)], axis=-1)
    out = _dot(h.astype(BF16), w_ref[...])
    y = alpha * x_ref[...] + out
    mu = jnp.mean(y, axis=-1, keepdims=True)
    var = jnp.mean(jnp.square(y - mu), axis=-1, keepdims=True)
    y_ref[...] = (y - mu) * lax.rsqrt(var + LN_EPS) * gain_ref[...] + bias_ref[...]


def _mix_out(x2d, oa, ga, ob, gb, w_out, gain, bias, alpha, block_rows):
    rows, d_model = x2d.shape
    g_spec = pl.BlockSpec((block_rows, GROUP_WIDTH), lambda i: (i, 0))
    x_spec = pl.BlockSpec((block_rows, d_model), lambda i: (i, 0))
    return pl.pallas_call(
        functools.partial(_mix_out_kernel, alpha),
        grid=(rows // block_rows,),
        in_specs=[x_spec, g_spec, g_spec, g_spec, g_spec,
                  _full_spec(w_out), _full_spec(gain), _full_spec(bias)],
        out_specs=x_spec,
        out_shape=jax.ShapeDtypeStruct(x2d.shape, F32),
        compiler_params=pltpu.CompilerParams(dimension_semantics=("arbitrary",),
                                             vmem_limit_bytes=VMEM_LIMIT),
        name="mix_out",
    )(x2d, oa, ga, ob, gb, w_out, gain, bias)


def kernel(x_prompt, x_sample, cache_k_sb, cache_v_sb, cache_k_moba, cache_v_moba, page_table,
           w_in, w_out, ln_gain, ln_bias):
    depth = w_in.shape[0]
    assert depth == 1, "single-layer trunk"
    bsz, seq, d_model = x_prompt.shape
    dbs, n_tok, _ = x_sample.shape
    alpha = (2.0 * depth) ** 0.25
    slopes = jnp.asarray([2.0 ** (-8.0 * (i + 1) / N_HEADS) for i in range(N_HEADS)], dtype=F32)
    slope_rows = jnp.broadcast_to(jnp.tile(slopes, n_tok)[:, None], (n_tok * N_HEADS, LANES))

    g = GROUP_WIDTH
    w = w_in[0]
    w_hi = w.astype(BF16)
    w_lo = (w[:, 4 * g:6 * g] - w_hi[:, 4 * g:6 * g].astype(F32)).astype(BF16)

    def col(a, c):
        return a[:, c * g:(c + 1) * g]

    w_row = jnp.concatenate([col(w_hi, c) for c in ROW_GROUPS], axis=1)
    w_row_lo = col(w_lo, 1)
    w_feat = jnp.concatenate([col(w_hi, c) for c in FEAT_GROUPS], axis=1).T
    w_feat_lo = col(w_lo, 0).T
    w_o = w_out[0].astype(BF16)
    gain = ln_gain[0][None, :]
    bias = ln_bias[0][None, :]

    def pages(c):
        return jnp.transpose(c[0], (0, 2, 3, 1)).reshape(c.shape[1], GROUP_WIDTH, c.shape[2])

    def heads_from_features(a):
        return jnp.transpose(a.reshape(bsz, N_HEADS, HEAD_DIM, seq), (0, 3, 1, 2))[None]

    def heads_from_rows(a):
        return a.reshape(1, dbs, n_tok, N_HEADS, HEAD_DIM)

    xp = x_prompt.reshape(bsz * seq, d_model)
    (ga, gb, ka_rows, kb_rows, kmean, qat, qbt, kat, vat, kbt, vbt) = _project_prompt(
        xp, w_row, w_row_lo, w_feat, w_feat_lo, bsz, 512)
    shp = (bsz, seq, GROUP_WIDTH)
    o_a = _sb_prompt(qat, ka_rows.reshape(shp), vat)
    o_b = _moba_prompt(slopes, qbt, kb_rows.reshape(shp), vbt, kmean.reshape(bsz, seq // MOBA_BLOCK, GROUP_WIDTH))
    y_p = _mix_out(xp, o_a.reshape(xp.shape[0], GROUP_WIDTH), ga, o_b.reshape(xp.shape[0], GROUP_WIDTH), gb,
                   w_o, gain, bias, alpha, 512)

    xs = x_sample.reshape(dbs * n_tok, d_model)
    sqa, ska, sva, sga, sqb, skb, svb, sgb = _project_rows(xs, w_hi, w_lo)
    sshp = (dbs, n_tok, GROUP_WIDTH)
    so_a, so_b = _sample_attention(
        page_table, slope_rows, sqa.reshape(sshp), sqb.reshape(sshp),
        ska.reshape(sshp), sva.reshape(sshp), skb.reshape(sshp), svb.reshape(sshp),
        pages(cache_k_sb), pages(cache_v_sb), pages(cache_k_moba), pages(cache_v_moba))
    y_s = _mix_out(xs, so_a.reshape(xs.shape[0], GROUP_WIDTH), sga, so_b.reshape(xs.shape[0], GROUP_WIDTH), sgb,
                   w_o, gain, bias, alpha, dbs * n_tok)

    return (y_p.reshape(x_prompt.shape), y_s.reshape(x_sample.shape),
            heads_from_features(kat), heads_from_features(vat),
            heads_from_features(kbt), heads_from_features(vbt),
            heads_from_rows(ska), heads_from_rows(sva), heads_from_rows(skb), heads_from_rows(svb))
```

```python
import functools
import math

import jax
import jax.numpy as jnp
from jax import lax
from jax.experimental import pallas as pl
from jax.experimental.pallas import tpu as pltpu

HEAD_DIM = 64
N_HEADS = 8
GROUP_WIDTH = N_HEADS * HEAD_DIM
HEAD_PAIR_WIDTH = 2 * HEAD_DIM
LANES = 128
MOBA_BLOCK = 256
MOBA_TOPK = 3
ATT_BLOCK = 256
QK_SCALE = HEAD_DIM ** -0.5
LOG2E = math.log2(math.e)
LN_EPS = 1e-5
NEG_BIG = -1e30
SOFTPLUS2_CLAMP = 100.0
RUN_DEAD = -160.0
RUN_DEAD_LN = -112.0
NEW_PAD = 128
PAGES_PER_STEP = 8
ONES_ROWS = 16
VMEM_LIMIT = 48 * 1024 * 1024

AUG_ONE = 0
AUG_KEYPOS = 3
AUG_BLOCK = 8

F32 = jnp.float32
BF16 = jnp.bfloat16


def _dot(a, b):
    return jnp.dot(a, b, preferred_element_type=F32)


def _dot_nt(a, b):
    return lax.dot_general(a, b, (((1,), (1,)), ((), ())), preferred_element_type=F32)


def _split(x):
    hi = x.astype(BF16)
    lo = (x - hi.astype(F32)).astype(BF16)
    return hi, lo


def _split3(x):
    hi = x.astype(BF16)
    r = x - hi.astype(F32)
    mid = r.astype(BF16)
    lo = (r - mid.astype(F32)).astype(BF16)
    return hi, mid, lo


def _dot_3pass(a, b):
    ah, al = _split(a)
    bh, bl = _split(b)
    return _dot(ah, bh) + (_dot(ah, bl) + _dot(al, bh))


def _softplus(z):
    return jnp.maximum(z, 0.0) + jnp.log(1.0 + jnp.exp(-jnp.abs(z)))


def _softplus2(z):
    return jnp.maximum(jnp.log2(1.0 + jnp.exp2(jnp.minimum(z, SOFTPLUS2_CLAMP))), z)


def _full_spec(a, single_buffer=False):
    mode = pl.Buffered(1) if single_buffer else None
    return pl.BlockSpec(a.shape, lambda *_: (0,) * a.ndim, pipeline_mode=mode)


def _proj_rows_kernel(x_ref, w_ref, wlo_ref, *out_refs):
    xh, xl = _split(x_ref[...])
    g = GROUP_WIDTH
    for c, o_ref in enumerate(out_refs):
        wc = w_ref[:, c * g:(c + 1) * g]
        out = _dot(xh, wc)
        if c in (4, 5):
            out = out + (_dot(xh, wlo_ref[:, (c - 4) * g:(c - 3) * g]) + _dot(xl, wc))
        o_ref[...] = out


def _project_rows(x2d, w_hi, w_lo):
    rows, _ = x2d.shape
    out = jax.ShapeDtypeStruct((rows, GROUP_WIDTH), F32)
    return pl.pallas_call(
        _proj_rows_kernel,
        grid=(1,),
        in_specs=[_full_spec(x2d), _full_spec(w_hi), _full_spec(w_lo)],
        out_specs=[pl.BlockSpec((rows, GROUP_WIDTH), lambda i: (0, 0))] * 8,
        out_shape=[out] * 8,
        compiler_params=pltpu.CompilerParams(dimension_semantics=("arbitrary",),
                                             vmem_limit_bytes=VMEM_LIMIT),
        name="proj_rows",
    )(x2d, w_hi, w_lo)


ROW_GROUPS = (3, 7, 1, 5)
FEAT_GROUPS = (0, 4, 1, 2, 5, 6)
FEAT_3PASS = (1,)


def _proj_prompt_kernel(x_ref, wr_ref, wrlo_ref, wf_ref, wflo_ref,
                        ga_ref, gb_ref, kar_ref, kbr_ref, kmean_ref,
                        qat_ref, qbt_ref, kat_ref, vat_ref, kbt_ref, vbt_ref):
    x = x_ref[...]
    xh, xl = _split(x)
    g = GROUP_WIDTH
    for c, o_ref in enumerate((ga_ref, gb_ref, kar_ref, kbr_ref)):
        o_ref[...] = _dot(xh, wr_ref[:, c * g:(c + 1) * g]).astype(o_ref.dtype)
    lo_slot = 0
    for c, o_ref in enumerate((qat_ref, qbt_ref, kat_ref, vat_ref, kbt_ref, vbt_ref)):
        wc = wf_ref[c * g:(c + 1) * g, :]
        out = _dot_nt(wc, xh)
        if c in FEAT_3PASS:
            out = out + (_dot_nt(wflo_ref[lo_slot * g:(lo_slot + 1) * g, :], xh) + _dot_nt(wc, xl))
            lo_slot += 1
        o_ref[0] = out
    n_blk = x.shape[0] // MOBA_BLOCK
    row8 = lax.broadcasted_iota(jnp.int32, (8, x.shape[1]), 0)
    xbar = jnp.zeros((8, x.shape[1]), F32)
    for i in range(n_blk):
        mean_i = jnp.sum(x[i * MOBA_BLOCK:(i + 1) * MOBA_BLOCK], axis=0, keepdims=True) * (1.0 / MOBA_BLOCK)
        xbar = jnp.where(row8 == i, mean_i, xbar)
    bh, bl = _split(xbar)
    wk = wr_ref[:, 3 * g:4 * g]
    km = _dot(bh, wk) + (_dot(bh, wrlo_ref[...]) + _dot(bl, wk))
    kmean_ref[0] = km[0:n_blk]


def _project_prompt(x2d, w_row, w_row_lo, w_feat, w_feat_lo, batch, block_rows):
    rows, d_model = x2d.shape
    seq = rows // batch
    per_seq = seq // block_rows
    n_blk = block_rows // MOBA_BLOCK
    row_spec = pl.BlockSpec((block_rows, GROUP_WIDTH), lambda i: (i, 0))
    feat_spec = pl.BlockSpec((1, GROUP_WIDTH, block_rows), lambda i: (i // per_seq, 0, i % per_seq))
    row_bf16 = jax.ShapeDtypeStruct((rows, GROUP_WIDTH), BF16)
    feat = jax.ShapeDtypeStruct((batch, GROUP_WIDTH, seq), F32)
    kmean = jax.ShapeDtypeStruct((rows // block_rows, n_blk, GROUP_WIDTH), F32)
    return pl.pallas_call(
        _proj_prompt_kernel,
        grid=(rows // block_rows,),
        in_specs=[pl.BlockSpec((block_rows, d_model), lambda i: (i, 0)),
                  _full_spec(w_row, True), _full_spec(w_row_lo, True),
                  _full_spec(w_feat, True), _full_spec(w_feat_lo, True)],
        out_specs=[row_spec] * 4 + [pl.BlockSpec((1, n_blk, GROUP_WIDTH), lambda i: (i, 0, 0))] + [feat_spec] * 6,
        out_shape=[row_bf16] * 4 + [kmean] + [feat] * 6,
        compiler_params=pltpu.CompilerParams(dimension_semantics=("arbitrary",),
                                             vmem_limit_bytes=VMEM_LIMIT),
        name="proj_prompt",
    )(x2d, w_row, w_row_lo, w_feat, w_feat_lo)


def _pair_columns(qt, tq):
    row = lax.broadcasted_iota(jnp.int32, qt.shape, 0)
    zero = jnp.zeros_like(qt)
    return jnp.concatenate([jnp.where(row < HEAD_DIM, qt, zero),
                            jnp.where(row >= HEAD_DIM, qt, zero)], axis=1)


def _pair_merge_rows(acc_t, tq):
    row = lax.broadcasted_iota(jnp.int32, (HEAD_PAIR_WIDTH, tq), 0)
    return jnp.where(row < HEAD_DIM, acc_t[:, :tq], acc_t[:, tq:]).T


def _blk(kj):
    return pl.ds(pl.multiple_of(kj * ATT_BLOCK, ATT_BLOCK), ATT_BLOCK)


def _head_pair_specs(seq):
    feat_spec = pl.BlockSpec((1, HEAD_PAIR_WIDTH, seq), lambda b, h: (b, h, 0))
    rows_spec = pl.BlockSpec((1, seq, HEAD_PAIR_WIDTH), lambda b, h: (b, 0, h))
    return feat_spec, rows_spec


def _sb_prompt_kernel(qt_ref, k_ref, vt_ref, o_ref, u_ref, vtb_ref, run_s, acc_s):
    b, hp = pl.program_id(0), pl.program_id(1)
    tq = ATT_BLOCK
    half = tq // 2
    u_rows = half + ONES_ROWS
    n_blocks = k_ref.shape[1] // tq

    @pl.when((b == 0) & (hp == 0))
    def _():
        s_i = lax.broadcasted_iota(jnp.int32, (u_rows, tq), 0)
        j_i = lax.broadcasted_iota(jnp.int32, (u_rows, tq), 1) & (half - 1)
        u_ref[...] = jnp.where((j_i >= s_i) | (s_i >= half), -1.0, 0.0).astype(BF16)

    vtb_ref[...] = vt_ref[0].astype(BF16)
    refs = (qt_ref, k_ref, o_ref, u_ref, vtb_ref, run_s, acc_s)

    first, second = _SbQueryBlock(refs, 0, 0), _SbQueryBlock(refs, 1, 1)
    first.reset()
    second.reset()
    first.single(0, True)
    second.pair(0, True)
    first.store()
    second.store()

    def two_blocks(i, carry):
        blocks = (_SbQueryBlock(refs, 2 * i, 0), _SbQueryBlock(refs, 2 * i + 1, 1))
        for blk in blocks:
            blk.reset()
        work = [(blk, t) for blk in blocks for t in (0, 1)]
        scored = [blk.scores(t, t == 0) for blk, t in work]
        locs = [blk.suffix(halves) for (blk, _), (_, halves) in zip(work, scored)]
        for (blk, t), (z, _), loc in zip(work, scored, locs):
            blk.weigh(t, z, loc)
        for blk in blocks:
            blk.earlier_blocks()
            blk.store()
        return carry

    lax.fori_loop(1, n_blocks // 2, two_blocks, 0)


class _SbQueryBlock:
    def __init__(self, refs, qi, slot):
        (self.qt_ref, self.k_ref, self.o_ref, self.u_ref, self.vtb_ref, self.run_s, self.acc_s) = refs
        self.qi, self.slot = qi, slot
        q = self.qt_ref[0, :, _blk(qi)] * (QK_SCALE * LOG2E)
        self.q_cols = _pair_columns(q.astype(BF16), ATT_BLOCK)

    def reset(self):
        self.run_s[self.slot] = jnp.zeros(self.run_s.shape[1:], F32)
        self.acc_s[self.slot] = jnp.zeros(self.acc_s.shape[1:], F32)

    def scores(self, t, diagonal):
        tq, half = ATT_BLOCK, ATT_BLOCK // 2
        z = _dot(self.k_ref[0, _blk(self.qi - t), :], self.q_cols)
        sp = _softplus2(z)
        if diagonal:
            key = lax.broadcasted_iota(jnp.int32, z.shape, 0)
            qry = lax.broadcasted_iota(jnp.int32, z.shape, 1) & (tq - 1)
            strict = key < qry
            sp = jnp.where(strict, sp, 0.0)
            z = jnp.where(strict, z, NEG_BIG)
        hi, lo = _split(sp)
        halves = [jnp.concatenate([hi[h * half:(h + 1) * half], lo[h * half:(h + 1) * half]], axis=0)
                  for h in range(2)]
        return z, halves

    def suffix(self, halves):
        neg_u = self.u_ref[...]
        return [_dot(neg_u, hl) for hl in halves]

    def weigh(self, t, z, locs):
        tq, half = ATT_BLOCK, ATT_BLOCK // 2
        run = self.run_s[self.slot]
        tot_first = locs[0][half:half + 1]
        tot_second = locs[1][half:half + 1]
        first = (z[0:half] + locs[0][0:half]) + (run + tot_second)
        second = (z[half:tq] + locs[1][0:half]) + run
        w = jnp.exp2(jnp.concatenate([first, second], axis=0)).astype(BF16)
        self.run_s[self.slot] = run + (tot_first + tot_second)
        self.acc_s[self.slot] = self.acc_s[self.slot] + _dot(self.vtb_ref[:, _blk(self.qi - t)], w)

    def single(self, t, diagonal):
        z, halves = self.scores(t, diagonal)
        self.weigh(t, z, self.suffix(halves))

    def pair(self, t, diagonal):
        z0, h0 = self.scores(t, diagonal)
        z1, h1 = self.scores(t + 1, False)
        l0 = self.suffix(h0)
        l1 = self.suffix(h1)
        self.weigh(t, z0, l0)
        self.weigh(t + 1, z1, l1)

    def alive(self):
        return jnp.max(self.run_s[self.slot]) > RUN_DEAD

    def earlier_blocks(self):
        qi = self.qi

        def more_pairs(c):
            return (c[0] + 1 <= qi) & c[1]

        def next_pair(c):
            self.pair(c[0], False)
            return c[0] + 2, self.alive()

        t, live = lax.while_loop(more_pairs, next_pair, (jnp.int32(2), self.alive()))

        @pl.when((t == qi) & live)
        def _():
            self.single(t, False)

    def store(self):
        out = _pair_merge_rows(self.acc_s[self.slot], ATT_BLOCK)
        self.o_ref[0, _blk(self.qi), :] = out.astype(self.o_ref.dtype)


def _sb_prompt(qt, k_rows, vt):
    bsz, _, seq = qt.shape
    tq = ATT_BLOCK
    assert seq % (2 * tq) == 0, "query blocks are processed two at a time"
    feat_spec, rows_spec = _head_pair_specs(seq)
    return pl.pallas_call(
        _sb_prompt_kernel,
        grid=(bsz, GROUP_WIDTH // HEAD_PAIR_WIDTH),
        in_specs=[feat_spec, rows_spec, feat_spec],
        out_specs=rows_spec,
        out_shape=jax.ShapeDtypeStruct((bsz, seq, GROUP_WIDTH), BF16),
        scratch_shapes=[pltpu.VMEM((tq // 2 + ONES_ROWS, tq), BF16),
                        pltpu.VMEM((HEAD_PAIR_WIDTH, seq), BF16),
                        pltpu.VMEM((2, 1, 2 * tq), F32),
                        pltpu.VMEM((2, HEAD_PAIR_WIDTH, 2 * tq), F32)],
        compiler_params=pltpu.CompilerParams(dimension_semantics=("arbitrary",) * 2,
                                             vmem_limit_bytes=VMEM_LIMIT),
        name="sb_prompt",
    )(qt, k_rows, vt)


def _rank_before_rows(g, n_valid, n_cand):
    blk = lax.broadcasted_iota(jnp.int32, g.shape, 0)
    cnt = jnp.zeros(g.shape, jnp.int32)
    for m in range(n_cand):
        gm = g[m:m + 1, :]
        beats = (gm > g) | ((gm == g) & (m < blk))
        cnt = cnt + jnp.where(beats & (m < n_valid), 1, 0)
    return cnt, blk


def _block_pairs(n_blocks):
    return [(q, 0) for q in range(n_blocks)] + [(q, t) for q in range(n_blocks) for t in range(1, q + 1)]


def _moba_prompt_kernel(slopes_ref, qt_ref, k_ref, vt_ref, kmean_ref, o_ref,
                        kaug_ref, vaug_ref, qaug_s, mask_s, s_s, p_s, pv_s, top_s, m_all, acc_all,
                        pair_q, pair_t):
    b, hp = pl.program_id(0), pl.program_id(1)
    tq = ATT_BLOCK
    seq = k_ref.shape[1]
    n_blocks = seq // MOBA_BLOCK
    pairs = _block_pairs(n_blocks)
    n_pairs = len(pairs)

    @pl.when((b == 0) & (hp == 0))
    def _():
        for i, (q, t) in enumerate(pairs):
            pair_q[i] = q
            pair_t[i] = t
        key = lax.broadcasted_iota(jnp.int32, (tq, 2 * tq), 0)
        qry = lax.broadcasted_iota(jnp.int32, (tq, 2 * tq), 1) & (tq - 1)
        mask_s[...] = jnp.where(key <= qry, 0.0, NEG_BIG)

    kaug_ref[:, 0:HEAD_PAIR_WIDTH] = k_ref[0]
    lane = lax.broadcasted_iota(jnp.int32, (MOBA_BLOCK, LANES), 1)
    key = lax.broadcasted_iota(jnp.int32, (MOBA_BLOCK, LANES), 0).astype(F32)
    base = jnp.where(lane < AUG_ONE + 3, 1.0,
                     jnp.where(lane < AUG_KEYPOS + 3, key, 0.0))
    for n in range(n_blocks):
        kaug_ref[n * MOBA_BLOCK:(n + 1) * MOBA_BLOCK, HEAD_PAIR_WIDTH:] = (
            jnp.where(lane == AUG_BLOCK + n, 1.0, base).astype(BF16))
    vaug_ref[0:HEAD_PAIR_WIDTH, :] = vt_ref[0].astype(BF16)
    vaug_ref[HEAD_PAIR_WIDTH:, :] = jnp.ones((ONES_ROWS, seq), BF16)

    lane = lax.broadcasted_iota(jnp.int32, (1, 2 * tq), 1)
    slope = jnp.where(lane < tq, slopes_ref[2 * hp], slopes_ref[2 * hp + 1]) * LOG2E
    q_off = (lane & (tq - 1)).astype(F32)
    row8 = lax.broadcasted_iota(jnp.int32, (8, 2 * tq), 0)
    small = jnp.zeros((8, 2 * tq), F32)
    for first, terms in ((AUG_ONE, _split3(-slope * q_off)), (AUG_KEYPOS, _split3(slope))):
        for r, v in enumerate(terms):
            small = jnp.where(row8 == first + r, v.astype(F32), small)
    kmean = kmean_ref[0]

    def prepare(qb, carry):
        qt = qt_ref[0, :, _blk(qb)]
        q_cols = _pair_columns(qt, tq)
        gate = _dot_3pass(kmean, q_cols)
        cnt, blk = _rank_before_rows(gate, qb, n_blocks)
        chosen = ((blk < qb) & (cnt < MOBA_TOPK)) | (blk == qb)
        choice_bias = jnp.where(chosen, 0.0, NEG_BIG)
        qaug_s[qb] = jnp.concatenate(
            [q_cols * (QK_SCALE * LOG2E), small, choice_bias,
             jnp.zeros((LANES - 8 - n_blocks, 2 * tq), F32)], axis=0).astype(BF16)
        m_all[qb] = jnp.full(m_all.shape[1:], NEG_BIG, F32)
        acc_all[qb] = jnp.zeros(acc_all.shape[1:], F32)
        return carry

    lax.fori_loop(0, n_blocks, prepare, 0, unroll=4)

    def score(i, slot):
        q, t = pair_q[i], pair_t[i]
        s_s[slot & 1] = _dot(kaug_ref[_blk(q - t), :], qaug_s[q])

    def probs(i, slot):
        t = pair_t[i]
        s = s_s[slot & 1]
        if isinstance(i, int) and i < n_blocks:
            s = s + mask_s[...]
        top = jnp.max(s, axis=0, keepdims=True)
        p_s[slot & 1] = jnp.exp2(s - top).astype(BF16)
        top_s[slot & 3, 0:1, :] = top - slope * (t * MOBA_BLOCK).astype(F32)

    def value(i, slot):
        q, t = pair_q[i], pair_t[i]
        pv_s[slot & 1] = _dot(vaug_ref[:, _blk(q - t)], p_s[slot & 1])

    def merge(i, slot):
        q = pair_q[i]
        top = top_s[slot & 3, 0:1, :]
        m_run = m_all[q, 0:1, :]
        m_new = jnp.maximum(m_run, top)
        acc_all[q] = acc_all[q] * jnp.exp2(m_run - m_new) + pv_s[slot & 1] * jnp.exp2(top - m_new)
        m_all[q, 0:1, :] = m_new

    def tick(i, slot, first=0, last=3):
        stages = (score, probs, value, merge)
        for k in range(first, last + 1):
            stages[k](i - k, slot - k)

    depth = 3
    unroll = 8
    n_static = depth + unroll * -(-(n_blocks + 1 - depth) // unroll)
    for i in range(n_static):
        tick(i, i, last=min(i, depth))
    n_groups = (n_pairs - n_static) // unroll

    def steady(g, carry):
        base = n_static + g * unroll
        for u in range(unroll):
            tick(base + u, n_static + u)
        return carry

    lax.fori_loop(0, n_groups, steady, 0)
    for i in range(n_static + n_groups * unroll, n_pairs):
        tick(i, i)
    for k in range(1, depth + 1):
        tick(n_pairs - 1 + k, n_pairs - 1 + k, first=k)

    def finish(qb, carry):
        acc = acc_all[qb]
        out_t = acc[0:HEAD_PAIR_WIDTH] / acc[HEAD_PAIR_WIDTH:HEAD_PAIR_WIDTH + 1]
        o_ref[0, _blk(qb), :] = _pair_merge_rows(out_t, tq).astype(o_ref.dtype)
        return carry

    lax.fori_loop(0, n_blocks, finish, 0, unroll=2)


def _moba_prompt(slopes, qt, k_rows, vt, kmean):
    bsz, _, seq = qt.shape
    n_blocks = seq // MOBA_BLOCK
    assert AUG_BLOCK + n_blocks <= LANES
    tq = ATT_BLOCK
    n_pairs = len(_block_pairs(n_blocks))
    acc_rows = HEAD_PAIR_WIDTH + ONES_ROWS
    feat_spec, rows_spec = _head_pair_specs(seq)
    return pl.pallas_call(
        _moba_prompt_kernel,
        grid=(bsz, GROUP_WIDTH // HEAD_PAIR_WIDTH),
        in_specs=[pl.BlockSpec(memory_space=pltpu.SMEM), feat_spec, rows_spec, feat_spec,
                  pl.BlockSpec((1, n_blocks, HEAD_PAIR_WIDTH), lambda b, h: (b, 0, h))],
        out_specs=rows_spec,
        out_shape=jax.ShapeDtypeStruct((bsz, seq, GROUP_WIDTH), BF16),
        scratch_shapes=[pltpu.VMEM((seq, HEAD_PAIR_WIDTH + LANES), BF16),
                        pltpu.VMEM((acc_rows, seq), BF16),
                        pltpu.VMEM((n_blocks, 2 * LANES, 2 * tq), BF16),
                        pltpu.VMEM((tq, 2 * tq), F32),
                        pltpu.VMEM((2, tq, 2 * tq), F32),
                        pltpu.VMEM((2, tq, 2 * tq), BF16),
                        pltpu.VMEM((2, acc_rows, 2 * tq), F32),
                        pltpu.VMEM((4, 8, 2 * tq), F32),
                        pltpu.VMEM((n_blocks, 8, 2 * tq), F32),
                        pltpu.VMEM((n_blocks, acc_rows, 2 * tq), F32),
                        pltpu.SMEM((n_pairs,), jnp.int32),
                        pltpu.SMEM((n_pairs,), jnp.int32)],
        compiler_params=pltpu.CompilerParams(dimension_semantics=("arbitrary",) * 2,
                                             vmem_limit_bytes=VMEM_LIMIT),
        name="moba_prompt",
    )(slopes, qt, k_rows, vt, kmean)


def _suffix_matrix(n):
    j = lax.broadcasted_iota(jnp.int32, (n, n), 0)
    s = lax.broadcasted_iota(jnp.int32, (n, n), 1)
    return jnp.where(j > s, 1.0, 0.0).astype(BF16)


def _suffix_sums(x, t2):
    hi, lo = _split(x)
    return _dot(jnp.concatenate([hi, lo], axis=1), t2)


def _rank_before_lanes(g, n_cand):
    lane = lax.broadcasted_iota(jnp.int32, g.shape, 1)
    cnt = jnp.zeros(g.shape, jnp.int32)
    for m in range(n_cand):
        gm = g[:, m:m + 1]
        beats = (gm > g) | ((gm == g) & (m < lane))
        cnt = cnt + jnp.where(beats, 1, 0)
    return cnt, lane


def _sample_kernel(pt_ref, slope_ref, qa_ref, qb_ref, kan_ref, van_ref, kbn_ref, vbn_ref,
                   cka_hbm, cva_hbm, *rest):
    pps = PAGES_PER_STEP
    kb_refs, vb_refs = (rest[i * pps:(i + 1) * pps] for i in range(2))
    (oa_ref, ob_ref, t2_ref, qa_s, qb_s, run_s, acca_s, gate_s, mx_s, l_s, accb_s,
     mo_s, lo_s, acco_s, new_s, live_s, ka_ring, va_ring, ring_sem, inflight_s) = rest[2 * pps:]
    b, j = pl.program_id(0), pl.program_id(1)
    n_batch = pl.num_programs(0)
    n_steps = pl.num_programs(1)
    blk = MOBA_BLOCK
    bps = pps * NEW_PAD // blk
    n_cached = accb_s.shape[0]
    n_first = (n_steps - 1 - j) * bps
    n_tok = qa_ref.shape[1]
    n_rows = n_tok * N_HEADS

    row = lax.broadcasted_iota(jnp.int32, (n_rows, GROUP_WIDTH), 0)
    lane_w = lax.broadcasted_iota(jnp.int32, (n_rows, GROUP_WIDTH), 1)
    head_mask = (lane_w // HEAD_DIM) == (row % N_HEADS)
    slope = slope_ref[:, 0:1]
    lane = lax.broadcasted_iota(jnp.int32, (n_rows, LANES), 1)

    @pl.when(j == 0)
    def _():
        t = _suffix_matrix(blk)
        t2_ref[0:blk, :] = t
        t2_ref[blk:, :] = t
        live_s[0] = 1

        def expand(q):
            rep = jnp.concatenate([jnp.broadcast_to(q[t:t + 1], (N_HEADS, GROUP_WIDTH))
                                   for t in range(n_tok)], axis=0)
            return jnp.where(head_mask, rep, 0.0)

        qa_s[...] = (expand(qa_ref[0]) * QK_SCALE).astype(BF16)
        qh, ql = _split(expand(qb_ref[0]))
        qb_s[...] = jnp.concatenate([qh, ql], axis=0)
        gate_s[...] = jnp.zeros_like(gate_s)
        mx_s[...] = jnp.zeros_like(mx_s)
        l_s[...] = jnp.zeros_like(l_s)

        def padded(ref):
            new_s[...] = jnp.zeros_like(new_s)
            new_s[0:n_tok, :] = ref[0]
            return new_s[...].astype(BF16)

        tok = (lax.broadcasted_iota(jnp.int32, (n_rows, NEW_PAD), 0) // N_HEADS)
        col = lax.broadcasted_iota(jnp.int32, (n_rows, NEW_PAD), 1)
        z = _dot_nt(qa_s[...], padded(kan_ref))
        strict = col < tok
        sp = _softplus(z)
        log_keep = jnp.where(strict, -sp, 0.0)
        hi, lo = _split(log_keep)
        tn = t2_ref[0:NEW_PAD, 0:NEW_PAD]
        between = _dot(hi, tn) + _dot(lo, tn)
        w = jnp.where(strict, jnp.exp((z - sp) + between), 0.0)
        run_s[...] = jnp.sum(log_keep, axis=1, keepdims=True)
        acca_s[...] = _dot(w.astype(BF16), padded(van_ref))
        s = _dot_nt(qb_s[0:n_rows, :], padded(kbn_ref)) * QK_SCALE
        s = s - slope * (tok - col).astype(F32)
        s = jnp.where(col <= tok, s, NEG_BIG)
        m_o = jnp.max(s, axis=1, keepdims=True)
        p = jnp.exp(s - m_o)
        mo_s[...] = m_o
        lo_s[...] = jnp.sum(p, axis=1, keepdims=True)
        acco_s[...] = _dot(p.astype(BF16), padded(vbn_ref))

    def pages(refs):
        return jnp.concatenate([r[0] for r in refs], axis=1)

    def ring_pages(ring, slot):
        return jnp.concatenate([ring[slot, p] for p in range(pps)], axis=1)

    def ring_copies(bb, jj, slot):
        first = pps * (n_steps - 1 - jj)
        copies = []
        for p in range(pps):
            pg = pt_ref[bb, first + p]
            copies.append(pltpu.make_async_copy(cka_hbm.at[pg], ka_ring.at[slot, p], ring_sem.at[0, slot]))
            copies.append(pltpu.make_async_copy(cva_hbm.at[pg], va_ring.at[slot, p], ring_sem.at[1, slot]))
        return copies

    slot = j & 1

    @pl.when((b == 0) & (j == 0))
    def _():
        for c in ring_copies(0, 0, 0):
            c.start()
        inflight_s[0] = 1

    last = j == n_steps - 1
    want_next = jnp.where(last, b + 1 < n_batch, live_s[0] == 1)

    @pl.when(want_next)
    def _():
        for c in ring_copies(jnp.where(last, b + 1, b), jnp.where(last, 0, j + 1), 1 - slot):
            c.start()

    inflight_s[1 - slot] = want_next.astype(jnp.int32)

    @pl.when(inflight_s[slot] == 1)
    def _():
        for c in ring_copies(b, j, slot):
            c.wait()

    @pl.when(live_s[0] == 1)
    def _():
        z = _dot(qa_s[...], ring_pages(ka_ring, slot).astype(BF16))
        sp = _softplus(z)
        log_keep = -sp
        t2 = t2_ref[...]
        run = run_s[...]
        betweens = [None] * bps
        for i in reversed(range(bps)):
            lk = log_keep[:, i * blk:(i + 1) * blk]
            betweens[i] = _suffix_sums(lk, t2) + run
            run = run + jnp.sum(lk, axis=1, keepdims=True)
        run_s[...] = run
        live_s[0] = (jnp.max(run) > RUN_DEAD_LN).astype(jnp.int32)
        w = jnp.exp((z - sp) + jnp.concatenate(betweens, axis=1))
        acca_s[...] = acca_s[...] + _dot_nt(w.astype(BF16), ring_pages(va_ring, slot).astype(BF16))

    kb_h, kb_l = _split(pages(kb_refs))
    vb = pages(vb_refs).astype(BF16)
    raw = _dot(qb_s[...], kb_h)
    raw_hh = raw[0:n_rows]
    raw3 = raw_hh + (raw[n_rows:] + _dot(qb_s[0:n_rows, :], kb_l))
    tok = lax.broadcasted_iota(jnp.int32, (n_rows, blk), 0) // N_HEADS
    col = lax.broadcasted_iota(jnp.int32, (n_rows, blk), 1)
    for i in range(bps):
        n = n_first + i
        cols = slice(i * blk, (i + 1) * blk)
        gate_n = jnp.sum(raw3[:, cols], axis=1, keepdims=True) * (1.0 / blk)
        dist = ((n_cached - n) * blk + tok - col).astype(F32)
        s = raw_hh[:, cols] * QK_SCALE - slope * dist
        m_n = jnp.max(s, axis=1, keepdims=True)
        p = jnp.exp(s - m_n)
        l_n = jnp.sum(p, axis=1, keepdims=True)
        here = lane == n
        gate_s[...] = jnp.where(here, gate_n, gate_s[...])
        mx_s[...] = jnp.where(here, m_n, mx_s[...])
        l_s[...] = jnp.where(here, l_n, l_s[...])
        accb_s[n] = _dot_nt(p.astype(BF16), vb[:, cols])

    @pl.when(j == n_steps - 1)
    def _():
        def fold(acc):
            kept = jnp.where(head_mask, acc, 0.0)
            return jnp.sum(kept.reshape(n_tok, N_HEADS, GROUP_WIDTH), axis=1)

        oa_ref[0] = fold(acca_s[...])
        cnt, ln = _rank_before_lanes(gate_s[...], n_cached)
        sel = (ln < n_cached) & (cnt < MOBA_TOPK)
        mx = jnp.where(sel, mx_s[...], NEG_BIG)
        m_o = mo_s[...]
        m_all = jnp.maximum(m_o, jnp.max(mx, axis=1, keepdims=True))
        c = jnp.where(sel, jnp.exp(mx - m_all), 0.0)
        c_o = jnp.exp(m_o - m_all)
        l_tot = c_o * lo_s[...] + jnp.sum(c * l_s[...], axis=1, keepdims=True)
        acc = c_o * acco_s[...]
        for nb in range(n_cached):
            acc = acc + c[:, nb:nb + 1] * accb_s[nb]
        ob_ref[0] = fold(acc / l_tot)


def _sample_attention(page_table, slope_rows, qa, qb, ka_new, va_new, kb_new, vb_new,
                      ck_sb, cv_sb, ck_mb, cv_mb):
    dbs, n_tok, _ = qa.shape
    n_pages = page_table.shape[1]
    page = ck_sb.shape[2]
    pps = PAGES_PER_STEP
    assert page == NEW_PAD and n_pages % pps == 0 and (pps * page) % MOBA_BLOCK == 0
    n_steps = n_pages // pps
    assert n_steps % 2 == 0, "the two-slot page ring alternates slots across consecutive grid steps"
    n_cached = n_pages * page // MOBA_BLOCK
    assert n_cached <= LANES
    n_rows = n_tok * N_HEADS

    tok_spec = pl.BlockSpec((1, n_tok, GROUP_WIDTH), lambda b, j, pt: (b, 0, 0))

    def page_spec(which):
        return pl.BlockSpec((1, GROUP_WIDTH, page),
                            lambda b, j, pt: (pt[b, pps * (n_steps - 1 - j) + which], 0, 0))

    grid_spec = pltpu.PrefetchScalarGridSpec(
        num_scalar_prefetch=1,
        grid=(dbs, n_steps),
        in_specs=[pl.BlockSpec((n_rows, LANES), lambda b, j, pt: (0, 0))] + [tok_spec] * 6
                 + [pl.BlockSpec(memory_space=pl.ANY)] * 2 + [page_spec(p) for p in range(pps)] * 2,
        out_specs=[tok_spec, tok_spec],
        scratch_shapes=[
            pltpu.VMEM((2 * MOBA_BLOCK, MOBA_BLOCK), BF16),
            pltpu.VMEM((n_rows, GROUP_WIDTH), BF16),
            pltpu.VMEM((2 * n_rows, GROUP_WIDTH), BF16),
            pltpu.VMEM((n_rows, 1), F32),
            pltpu.VMEM((n_rows, GROUP_WIDTH), F32),
            pltpu.VMEM((n_rows, LANES), F32),
            pltpu.VMEM((n_rows, LANES), F32),
            pltpu.VMEM((n_rows, LANES), F32),
            pltpu.VMEM((n_cached, n_rows, GROUP_WIDTH), F32),
            pltpu.VMEM((n_rows, 1), F32),
            pltpu.VMEM((n_rows, 1), F32),
            pltpu.VMEM((n_rows, GROUP_WIDTH), F32),
            pltpu.VMEM((NEW_PAD, GROUP_WIDTH), F32),
            pltpu.SMEM((1,), jnp.int32),
            pltpu.VMEM((2, pps, GROUP_WIDTH, page), F32),
            pltpu.VMEM((2, pps, GROUP_WIDTH, page), F32),
            pltpu.SemaphoreType.DMA((2, 2)),
            pltpu.SMEM((2,), jnp.int32),
        ],
    )
    out = jax.ShapeDtypeStruct((dbs, n_tok, GROUP_WIDTH), F32)
    caches = [ck_sb, cv_sb] + [c for c in (ck_mb, cv_mb) for _ in range(pps)]
    return pl.pallas_call(
        _sample_kernel,
        grid_spec=grid_spec,
        out_shape=[out, out],
        compiler_params=pltpu.CompilerParams(dimension_semantics=("arbitrary", "arbitrary"),
                                             vmem_limit_bytes=VMEM_LIMIT),
        name="sample_attn",
    )(page_table, slope_rows, qa, qb, ka_new, va_new, kb_new, vb_new, *caches)


def _mix_out_kernel(alpha, x_ref, oa_ref, ga_ref, ob_ref, gb_ref, w_ref, gain_ref, bias_ref, y_ref):
    def gated(o_ref, g_ref):
        return o_ref[...].astype(F32) * jax.nn.silu(g_ref[...].astype(F32))

    h = jnp.concatenate([gated(oa_ref, ga_ref), gated(ob_ref, gb_ref)], axis=-1)
    out = _dot(h.astype(BF16), w_ref[...])
    y = alpha * x_ref[...] + out
    mu = jnp.mean(y, axis=-1, keepdims=True)
    var = jnp.mean(jnp.square(y - mu), axis=-1, keepdims=True)
    y_ref[...] = (y - mu) * lax.rsqrt(var + LN_EPS) * gain_ref[...] + bias_ref[...]


def _mix_out(x2d, oa, ga, ob, gb, w_out, gain, bias, alpha, block_rows):
    rows, d_model = x2d.shape
    g_spec = pl.BlockSpec((block_rows, GROUP_WIDTH), lambda i: (i, 0))
    x_spec = pl.BlockSpec((block_rows, d_model), lambda i: (i, 0))
    return pl.pallas_call(
        functools.partial(_mix_out_kernel, alpha),
        grid=(rows // block_rows,),
        in_specs=[x_spec, g_spec, g_spec, g_spec, g_spec,
                  _full_spec(w_out), _full_spec(gain), _full_spec(bias)],
        out_specs=x_spec,
        out_shape=jax.ShapeDtypeStruct(x2d.shape, F32),
        compiler_params=pltpu.CompilerParams(dimension_semantics=("arbitrary",),
                                             vmem_limit_bytes=VMEM_LIMIT),
        name="mix_out",
    )(x2d, oa, ga, ob, gb, w_out, gain, bias)


def kernel(x_prompt, x_sample, cache_k_sb, cache_v_sb, cache_k_moba, cache_v_moba, page_table,
           w_in, w_out, ln_gain, ln_bias):
    depth = w_in.shape[0]
    assert depth == 1, "single-layer trunk"
    bsz, seq, d_model = x_prompt.shape
    dbs, n_tok, _ = x_sample.shape
    alpha = (2.0 * depth) ** 0.25
    slopes = jnp.asarray([2.0 ** (-8.0 * (i + 1) / N_HEADS) for i in range(N_HEADS)], dtype=F32)
    slope_rows = jnp.broadcast_to(jnp.tile(slopes, n_tok)[:, None], (n_tok * N_HEADS, LANES))

    g = GROUP_WIDTH
    w = w_in[0]
    w_hi = w.astype(BF16)
    w_lo = (w[:, 4 * g:6 * g] - w_hi[:, 4 * g:6 * g].astype(F32)).astype(BF16)

    def col(a, c):
        return a[:, c * g:(c + 1) * g]

    w_row = jnp.concatenate([col(w_hi, c) for c in ROW_GROUPS], axis=1)
    w_row_lo = col(w_lo, 1)
    w_feat = jnp.concatenate([col(w_hi, c) for c in FEAT_GROUPS], axis=1).T
    w_feat_lo = col(w_lo, 0).T
    w_o = w_out[0].astype(BF16)
    gain = ln_gain[0][None, :]
    bias = ln_bias[0][None, :]

    def pages(c):
        return jnp.transpose(c[0], (0, 2, 3, 1)).reshape(c.shape[1], GROUP_WIDTH, c.shape[2])

    def heads_from_features(a):
        return jnp.transpose(a.reshape(bsz, N_HEADS, HEAD_DIM, seq), (0, 3, 1, 2))[None]

    def heads_from_rows(a):
        return a.reshape(1, dbs, n_tok, N_HEADS, HEAD_DIM)

    xp = x_prompt.reshape(bsz * seq, d_model)
    (ga, gb, ka_rows, kb_rows, kmean, qat, qbt, kat, vat, kbt, vbt) = _project_prompt(
        xp, w_row, w_row_lo, w_feat, w_feat_lo, bsz, 512)
    shp = (bsz, seq, GROUP_WIDTH)
    o_a = _sb_prompt(qat, ka_rows.reshape(shp), vat)
    o_b = _moba_prompt(slopes, qbt, kb_rows.reshape(shp), vbt, kmean.reshape(bsz, seq // MOBA_BLOCK, GROUP_WIDTH))
    y_p = _mix_out(xp, o_a.reshape(xp.shape[0], GROUP_WIDTH), ga, o_b.reshape(xp.shape[0], GROUP_WIDTH), gb,
                   w_o, gain, bias, alpha, 512)

    xs = x_sample.reshape(dbs * n_tok, d_model)
    sqa, ska, sva, sga, sqb, skb, svb, sgb = _project_rows(xs, w_hi, w_lo)
    sshp = (dbs, n_tok, GROUP_WIDTH)
    so_a, so_b = _sample_attention(
        page_table, slope_rows, sqa.reshape(sshp), sqb.reshape(sshp),
        ska.reshape(sshp), sva.reshape(sshp), skb.reshape(sshp), svb.reshape(sshp),
        pages(cache_k_sb), pages(cache_v_sb), pages(cache_k_moba), pages(cache_v_moba))
    y_s = _mix_out(xs, so_a.reshape(xs.shape[0], GROUP_WIDTH), sga, so_b.reshape(xs.shape[0], GROUP_WIDTH), sgb,
                   w_o, gain, bias, alpha, dbs * n_tok)

    return (y_p.reshape(x_prompt.shape), y_s.reshape(x_sample.shape),
            heads_from_features(kat), heads_from_features(vat),
            heads_from_features(kbt), heads_from_features(vbt),
            heads_from_rows(ska), heads_from_rows(sva), heads_from_rows(skb), heads_from_rows(svb))
```

```python
import functools
import math

import jax
import jax.numpy as jnp
from jax import lax
from jax.experimental import pallas as pl
from jax.experimental.pallas import tpu as pltpu

HEAD_DIM = 64
N_HEADS = 8
GROUP_WIDTH = N_HEADS * HEAD_DIM
HEAD_PAIR_WIDTH = 2 * HEAD_DIM
LANES = 128
MOBA_BLOCK = 256
MOBA_TOPK = 3
ATT_BLOCK = 256
QK_SCALE = HEAD_DIM ** -0.5
LOG2E = math.log2(math.e)
LN_EPS = 1e-5
NEG_BIG = -1e30
SOFTPLUS2_CLAMP = 100.0
RUN_DEAD = -160.0
RUN_DEAD_LN = -112.0
NEW_PAD = 128
PAGES_PER_STEP = 8
SB_GROUP = 2
ONES_ROWS = 16
VMEM_LIMIT = 48 * 1024 * 1024

AUG_ONE = 0
AUG_KEYPOS = 3
AUG_BLOCK = 8

F32 = jnp.float32
BF16 = jnp.bfloat16


def _dot(a, b):
    return jnp.dot(a, b, preferred_element_type=F32)


def _dot_nt(a, b):
    return lax.dot_general(a, b, (((1,), (1,)), ((), ())), preferred_element_type=F32)


def _split(x):
    hi = x.astype(BF16)
    lo = (x - hi.astype(F32)).astype(BF16)
    return hi, lo


def _split3(x):
    hi = x.astype(BF16)
    r = x - hi.astype(F32)
    mid = r.astype(BF16)
    lo = (r - mid.astype(F32)).astype(BF16)
    return hi, mid, lo


def _dot_3pass(a, b):
    ah, al = _split(a)
    bh, bl = _split(b)
    return _dot(ah, bh) + (_dot(ah, bl) + _dot(al, bh))


def _softplus(z):
    return jnp.maximum(z, 0.0) + jnp.log(1.0 + jnp.exp(-jnp.abs(z)))


def _softplus2(z):
    return jnp.maximum(jnp.log2(1.0 + jnp.exp2(jnp.minimum(z, SOFTPLUS2_CLAMP))), z)


def _full_spec(a, single_buffer=False):
    mode = pl.Buffered(1) if single_buffer else None
    return pl.BlockSpec(a.shape, lambda *_: (0,) * a.ndim, pipeline_mode=mode)


def _proj_rows_kernel(x_ref, w_ref, wlo_ref, *out_refs):
    xh, xl = _split(x_ref[...])
    g = GROUP_WIDTH
    for c, o_ref in enumerate(out_refs):
        wc = w_ref[:, c * g:(c + 1) * g]
        out = _dot(xh, wc)
        if c in (4, 5):
            out = out + (_dot(xh, wlo_ref[:, (c - 4) * g:(c - 3) * g]) + _dot(xl, wc))
        o_ref[...] = out


def _project_rows(x2d, w_hi, w_lo):
    rows, _ = x2d.shape
    out = jax.ShapeDtypeStruct((rows, GROUP_WIDTH), F32)
    return pl.pallas_call(
        _proj_rows_kernel,
        grid=(1,),
        in_specs=[_full_spec(x2d), _full_spec(w_hi), _full_spec(w_lo)],
        out_specs=[pl.BlockSpec((rows, GROUP_WIDTH), lambda i: (0, 0))] * 8,
        out_shape=[out] * 8,
        compiler_params=pltpu.CompilerParams(dimension_semantics=("arbitrary",),
                                             vmem_limit_bytes=VMEM_LIMIT),
        name="proj_rows",
    )(x2d, w_hi, w_lo)


ROW_GROUPS = (3, 7, 1, 5)
FEAT_GROUPS = (0, 4, 1, 2, 5, 6)
FEAT_3PASS = (1,)


def _proj_prompt_kernel(x_ref, wr_ref, wrlo_ref, wf_ref, wflo_ref,
                        ga_ref, gb_ref, kar_ref, kbr_ref, kmean_ref,
                        qat_ref, qbt_ref, kat_ref, vat_ref, kbt_ref, vbt_ref):
    x = x_ref[...]
    xh, xl = _split(x)
    g = GROUP_WIDTH
    for c, o_ref in enumerate((ga_ref, gb_ref, kar_ref, kbr_ref)):
        o_ref[...] = _dot(xh, wr_ref[:, c * g:(c + 1) * g]).astype(o_ref.dtype)
    lo_slot = 0
    for c, o_ref in enumerate((qat_ref, qbt_ref, kat_ref, vat_ref, kbt_ref, vbt_ref)):
        wc = wf_ref[c * g:(c + 1) * g, :]
        out = _dot_nt(wc, xh)
        if c in FEAT_3PASS:
            out = out + (_dot_nt(wflo_ref[lo_slot * g:(lo_slot + 1) * g, :], xh) + _dot_nt(wc, xl))
            lo_slot += 1
        o_ref[0] = out
    n_blk = x.shape[0] // MOBA_BLOCK
    row8 = lax.broadcasted_iota(jnp.int32, (8, x.shape[1]), 0)
    xbar = jnp.zeros((8, x.shape[1]), F32)
    for i in range(n_blk):
        mean_i = jnp.sum(x[i * MOBA_BLOCK:(i + 1) * MOBA_BLOCK], axis=0, keepdims=True) * (1.0 / MOBA_BLOCK)
        xbar = jnp.where(row8 == i, mean_i, xbar)
    bh, bl = _split(xbar)
    wk = wr_ref[:, 3 * g:4 * g]
    km = _dot(bh, wk) + (_dot(bh, wrlo_ref[...]) + _dot(bl, wk))
    kmean_ref[0] = km[0:n_blk]


def _project_prompt(x2d, w_row, w_row_lo, w_feat, w_feat_lo, batch, block_rows):
    rows, d_model = x2d.shape
    seq = rows // batch
    per_seq = seq // block_rows
    n_blk = block_rows // MOBA_BLOCK
    row_spec = pl.BlockSpec((block_rows, GROUP_WIDTH), lambda i: (i, 0))
    feat_spec = pl.BlockSpec((1, GROUP_WIDTH, block_rows), lambda i: (i // per_seq, 0, i % per_seq))
    row_bf16 = jax.ShapeDtypeStruct((rows, GROUP_WIDTH), BF16)
    feat = jax.ShapeDtypeStruct((batch, GROUP_WIDTH, seq), F32)
    kmean = jax.ShapeDtypeStruct((rows // block_rows, n_blk, GROUP_WIDTH), F32)
    return pl.pallas_call(
        _proj_prompt_kernel,
        grid=(rows // block_rows,),
        in_specs=[pl.BlockSpec((block_rows, d_model), lambda i: (i, 0)),
                  _full_spec(w_row, True), _full_spec(w_row_lo, True),
                  _full_spec(w_feat, True), _full_spec(w_feat_lo, True)],
        out_specs=[row_spec] * 4 + [pl.BlockSpec((1, n_blk, GROUP_WIDTH), lambda i: (i, 0, 0))] + [feat_spec] * 6,
        out_shape=[row_bf16] * 4 + [kmean] + [feat] * 6,
        compiler_params=pltpu.CompilerParams(dimension_semantics=("arbitrary",),
                                             vmem_limit_bytes=VMEM_LIMIT),
        name="proj_prompt",
    )(x2d, w_row, w_row_lo, w_feat, w_feat_lo)


def _pair_columns(qt, tq):
    row = lax.broadcasted_iota(jnp.int32, qt.shape, 0)
    zero = jnp.zeros_like(qt)
    return jnp.concatenate([jnp.where(row < HEAD_DIM, qt, zero),
                            jnp.where(row >= HEAD_DIM, qt, zero)], axis=1)


def _pair_merge_rows(acc_t, tq):
    row = lax.broadcasted_iota(jnp.int32, (HEAD_PAIR_WIDTH, tq), 0)
    return jnp.where(row < HEAD_DIM, acc_t[:, :tq], acc_t[:, tq:]).T


def _blk(kj):
    return pl.ds(pl.multiple_of(kj * ATT_BLOCK, ATT_BLOCK), ATT_BLOCK)


def _head_pair_specs(seq):
    feat_spec = pl.BlockSpec((1, HEAD_PAIR_WIDTH, seq), lambda b, h: (b, h, 0))
    rows_spec = pl.BlockSpec((1, seq, HEAD_PAIR_WIDTH), lambda b, h: (b, 0, h))
    return feat_spec, rows_spec


def _sb_prompt_kernel(qt_ref, k_ref, vt_ref, o_ref, u_ref, vtb_ref, run_s, acc_s):
    b, hp = pl.program_id(0), pl.program_id(1)
    tq = ATT_BLOCK
    half = tq // 2
    u_rows = half + ONES_ROWS
    n_blocks = k_ref.shape[1] // tq

    @pl.when((b == 0) & (hp == 0))
    def _():
        s_i = lax.broadcasted_iota(jnp.int32, (u_rows, tq), 0)
        j_i = lax.broadcasted_iota(jnp.int32, (u_rows, tq), 1) & (half - 1)
        u_ref[...] = jnp.where((j_i >= s_i) | (s_i >= half), -1.0, 0.0).astype(BF16)

    vtb_ref[...] = vt_ref[0].astype(BF16)
    refs = (qt_ref, k_ref, o_ref, u_ref, vtb_ref, run_s, acc_s)

    def group(q_first, count):
        blocks = [_SbQueryBlock(refs, q_first + n, n) for n in range(count)]
        for blk in blocks:
            blk.reset()
        work = [(blk, t) for blk in blocks for t in (0, 1)]
        scored = [blk.scores(t, t == 0) for blk, t in work]
        locs = [blk.suffix(halves) for (blk, _), (_, halves) in zip(work, scored)]
        for (blk, t), (z, _), loc in zip(work, scored, locs):
            blk.weigh(t, z, loc)
        for blk in blocks:
            blk.earlier_blocks()
            blk.store()

    first = _SbQueryBlock(refs, 0, 0)
    first.reset()
    first.single(0, True)
    first.store()
    group(1, SB_GROUP - 1)

    def grouped(i, carry):
        group(i * SB_GROUP, SB_GROUP)
        return carry

    lax.fori_loop(1, n_blocks // SB_GROUP, grouped, 0)


class _SbQueryBlock:
    def __init__(self, refs, qi, slot):
        (self.qt_ref, self.k_ref, self.o_ref, self.u_ref, self.vtb_ref, self.run_s, self.acc_s) = refs
        self.qi, self.slot = qi, slot
        q = self.qt_ref[0, :, _blk(qi)] * (QK_SCALE * LOG2E)
        self.q_cols = _pair_columns(q.astype(BF16), ATT_BLOCK)

    def reset(self):
        self.run_s[self.slot] = jnp.zeros(self.run_s.shape[1:], F32)
        self.acc_s[self.slot] = jnp.zeros(self.acc_s.shape[1:], F32)

    def scores(self, t, diagonal):
        tq, half = ATT_BLOCK, ATT_BLOCK // 2
        z = _dot(self.k_ref[0, _blk(self.qi - t), :], self.q_cols)
        sp = _softplus2(z)
        if diagonal:
            key = lax.broadcasted_iota(jnp.int32, z.shape, 0)
            qry = lax.broadcasted_iota(jnp.int32, z.shape, 1) & (tq - 1)
            strict = key < qry
            sp = jnp.where(strict, sp, 0.0)
            z = jnp.where(strict, z, NEG_BIG)
        hi, lo = _split(sp)
        halves = [jnp.concatenate([hi[h * half:(h + 1) * half], lo[h * half:(h + 1) * half]], axis=0)
                  for h in range(2)]
        return z, halves

    def suffix(self, halves):
        neg_u = self.u_ref[...]
        return [_dot(neg_u, hl) for hl in halves]

    def weigh(self, t, z, locs):
        tq, half = ATT_BLOCK, ATT_BLOCK // 2
        run = self.run_s[self.slot]
        tot_first = locs[0][half:half + 1]
        tot_second = locs[1][half:half + 1]
        first = (z[0:half] + locs[0][0:half]) + (run + tot_second)
        second = (z[half:tq] + locs[1][0:half]) + run
        w = jnp.exp2(jnp.concatenate([first, second], axis=0)).astype(BF16)
        self.run_s[self.slot] = run + (tot_first + tot_second)
        self.acc_s[self.slot] = self.acc_s[self.slot] + _dot(self.vtb_ref[:, _blk(self.qi - t)], w)

    def single(self, t, diagonal):
        z, halves = self.scores(t, diagonal)
        self.weigh(t, z, self.suffix(halves))

    def pair(self, t, diagonal):
        z0, h0 = self.scores(t, diagonal)
        z1, h1 = self.scores(t + 1, False)
        l0 = self.suffix(h0)
        l1 = self.suffix(h1)
        self.weigh(t, z0, l0)
        self.weigh(t + 1, z1, l1)

    def alive(self):
        return jnp.max(self.run_s[self.slot]) > RUN_DEAD

    def earlier_blocks(self):
        qi = self.qi

        def more_pairs(c):
            return (c[0] + 1 <= qi) & c[1]

        def next_pair(c):
            self.pair(c[0], False)
            return c[0] + 2, self.alive()

        t, live = lax.while_loop(more_pairs, next_pair, (jnp.int32(2), self.alive()))

        @pl.when((t == qi) & live)
        def _():
            self.single(t, False)

    def store(self):
        out = _pair_merge_rows(self.acc_s[self.slot], ATT_BLOCK)
        self.o_ref[0, _blk(self.qi), :] = out.astype(self.o_ref.dtype)


def _sb_prompt(qt, k_rows, vt):
    bsz, _, seq = qt.shape
    tq = ATT_BLOCK
    assert seq % (SB_GROUP * tq) == 0, "query blocks are processed SB_GROUP at a time"
    feat_spec, rows_spec = _head_pair_specs(seq)
    return pl.pallas_call(
        _sb_prompt_kernel,
        grid=(bsz, GROUP_WIDTH // HEAD_PAIR_WIDTH),
        in_specs=[feat_spec, rows_spec, feat_spec],
        out_specs=rows_spec,
        out_shape=jax.ShapeDtypeStruct((bsz, seq, GROUP_WIDTH), BF16),
        scratch_shapes=[pltpu.VMEM((tq // 2 + ONES_ROWS, tq), BF16),
                        pltpu.VMEM((HEAD_PAIR_WIDTH, seq), BF16),
                        pltpu.VMEM((SB_GROUP, 1, 2 * tq), F32),
                        pltpu.VMEM((SB_GROUP, HEAD_PAIR_WIDTH, 2 * tq), F32)],
        compiler_params=pltpu.CompilerParams(dimension_semantics=("arbitrary",) * 2,
                                             vmem_limit_bytes=VMEM_LIMIT),
        name="sb_prompt",
    )(qt, k_rows, vt)


def _rank_before_rows(g, n_valid, n_cand):
    blk = lax.broadcasted_iota(jnp.int32, g.shape, 0)
    cnt = jnp.zeros(g.shape, jnp.int32)
    for m in range(n_cand):
        gm = g[m:m + 1, :]
        beats = (gm > g) | ((gm == g) & (m < blk))
        cnt = cnt + jnp.where(beats & (m < n_valid), 1, 0)
    return cnt, blk


def _block_pairs(n_blocks):
    return [(q, 0) for q in range(n_blocks)] + [(q, t) for q in range(n_blocks) for t in range(1, q + 1)]


def _moba_prompt_kernel(slopes_ref, qt_ref, k_ref, vt_ref, kmean_ref, o_ref,
                        kaug_ref, vaug_ref, qaug_s, mask_s, s_s, p_s, pv_s, top_s, m_all, acc_all,
                        pair_q, pair_t):
    b, hp = pl.program_id(0), pl.program_id(1)
    tq = ATT_BLOCK
    seq = k_ref.shape[1]
    n_blocks = seq // MOBA_BLOCK
    pairs = _block_pairs(n_blocks)
    n_pairs = len(pairs)

    @pl.when((b == 0) & (hp == 0))
    def _():
        for i, (q, t) in enumerate(pairs):
            pair_q[i] = q
            pair_t[i] = t
        key = lax.broadcasted_iota(jnp.int32, (tq, 2 * tq), 0)
        qry = lax.broadcasted_iota(jnp.int32, (tq, 2 * tq), 1) & (tq - 1)
        mask_s[...] = jnp.where(key <= qry, 0.0, NEG_BIG)

    kaug_ref[:, 0:HEAD_PAIR_WIDTH] = k_ref[0]
    lane = lax.broadcasted_iota(jnp.int32, (MOBA_BLOCK, LANES), 1)
    key = lax.broadcasted_iota(jnp.int32, (MOBA_BLOCK, LANES), 0).astype(F32)
    base = jnp.where(lane < AUG_ONE + 3, 1.0,
                     jnp.where(lane < AUG_KEYPOS + 3, key, 0.0))
    for n in range(n_blocks):
        kaug_ref[n * MOBA_BLOCK:(n + 1) * MOBA_BLOCK, HEAD_PAIR_WIDTH:] = (
            jnp.where(lane == AUG_BLOCK + n, 1.0, base).astype(BF16))
    for h in range(2):
        vaug_ref[h, 0:HEAD_DIM, :] = vt_ref[0, h * HEAD_DIM:(h + 1) * HEAD_DIM, :].astype(BF16)
        vaug_ref[h, HEAD_DIM:, :] = jnp.ones((ONES_ROWS, seq), BF16)

    lane = lax.broadcasted_iota(jnp.int32, (1, 2 * tq), 1)
    slope = jnp.where(lane < tq, slopes_ref[2 * hp], slopes_ref[2 * hp + 1]) * LOG2E
    q_off = (lane & (tq - 1)).astype(F32)
    row8 = lax.broadcasted_iota(jnp.int32, (8, 2 * tq), 0)
    small = jnp.zeros((8, 2 * tq), F32)
    for first, terms in ((AUG_ONE, _split3(-slope * q_off)), (AUG_KEYPOS, _split3(slope))):
        for r, v in enumerate(terms):
            small = jnp.where(row8 == first + r, v.astype(F32), small)
    kmean = kmean_ref[0]

    def prepare(qb, carry):
        qt = qt_ref[0, :, _blk(qb)]
        q_cols = _pair_columns(qt, tq)
        gate = _dot_3pass(kmean, q_cols)
        cnt, blk = _rank_before_rows(gate, qb, n_blocks)
        chosen = ((blk < qb) & (cnt < MOBA_TOPK)) | (blk == qb)
        choice_bias = jnp.where(chosen, 0.0, NEG_BIG)
        qaug_s[qb] = jnp.concatenate(
            [q_cols * (QK_SCALE * LOG2E), small, choice_bias,
             jnp.zeros((LANES - 8 - n_blocks, 2 * tq), F32)], axis=0).astype(BF16)
        m_all[qb] = jnp.full(m_all.shape[1:], NEG_BIG, F32)
        acc_all[qb] = jnp.zeros(acc_all.shape[1:], F32)
        return carry

    lax.fori_loop(0, n_blocks, prepare, 0, unroll=4)

    def score(i, slot):
        q, t = pair_q[i], pair_t[i]
        s_s[slot & 1] = _dot(kaug_ref[_blk(q - t), :], qaug_s[q])

    def probs(i, slot):
        t = pair_t[i]
        s = s_s[slot & 1]
        if isinstance(i, int) and i < n_blocks:
            s = s + mask_s[...]
        top = jnp.max(s, axis=0, keepdims=True)
        p_s[slot & 1] = jnp.exp2(s - top).astype(BF16)
        top_s[slot & 3, 0:1, :] = top - slope * (t * MOBA_BLOCK).astype(F32)

    def value(i, slot):
        q, t = pair_q[i], pair_t[i]
        for h in range(2):
            cols = slice(h * tq, (h + 1) * tq)
            pv_s[slot & 1, :, cols] = _dot(vaug_ref[h, :, _blk(q - t)], p_s[slot & 1, :, cols])

    def merge(i, slot):
        q = pair_q[i]
        top = top_s[slot & 3, 0:1, :]
        m_run = m_all[q, 0:1, :]
        m_new = jnp.maximum(m_run, top)
        acc_all[q] = acc_all[q] * jnp.exp2(m_run - m_new) + pv_s[slot & 1] * jnp.exp2(top - m_new)
        m_all[q, 0:1, :] = m_new

    def tick(i, slot, first=0, last=3):
        stages = (score, probs, value, merge)
        for k in range(first, last + 1):
            stages[k](i - k, slot - k)

    depth = 3
    unroll = 8
    n_static = depth + unroll * -(-(n_blocks + 1 - depth) // unroll)
    for i in range(n_static):
        tick(i, i, last=min(i, depth))
    n_groups = (n_pairs - n_static) // unroll

    def steady(g, carry):
        base = n_static + g * unroll
        for u in range(unroll):
            tick(base + u, n_static + u)
        return carry

    lax.fori_loop(0, n_groups, steady, 0)
    for i in range(n_static + n_groups * unroll, n_pairs):
        tick(i, i)
    for k in range(1, depth + 1):
        tick(n_pairs - 1 + k, n_pairs - 1 + k, first=k)

    def finish(qb, carry):
        acc = acc_all[qb]
        out_t = acc[0:HEAD_DIM] / acc[HEAD_DIM:HEAD_DIM + 1]
        rows = jnp.concatenate([out_t[:, :tq], out_t[:, tq:]], axis=0).T
        o_ref[0, _blk(qb), :] = rows.astype(o_ref.dtype)
        return carry

    lax.fori_loop(0, n_blocks, finish, 0, unroll=2)


def _moba_prompt(slopes, qt, k_rows, vt, kmean):
    bsz, _, seq = qt.shape
    n_blocks = seq // MOBA_BLOCK
    assert AUG_BLOCK + n_blocks <= LANES
    tq = ATT_BLOCK
    n_pairs = len(_block_pairs(n_blocks))
    acc_rows = HEAD_DIM + ONES_ROWS
    feat_spec, rows_spec = _head_pair_specs(seq)
    return pl.pallas_call(
        _moba_prompt_kernel,
        grid=(bsz, GROUP_WIDTH // HEAD_PAIR_WIDTH),
        in_specs=[pl.BlockSpec(memory_space=pltpu.SMEM), feat_spec, rows_spec, feat_spec,
                  pl.BlockSpec((1, n_blocks, HEAD_PAIR_WIDTH), lambda b, h: (b, 0, h))],
        out_specs=rows_spec,
        out_shape=jax.ShapeDtypeStruct((bsz, seq, GROUP_WIDTH), BF16),
        scratch_shapes=[pltpu.VMEM((seq, HEAD_PAIR_WIDTH + LANES), BF16),
                        pltpu.VMEM((2, acc_rows, seq), BF16),
                        pltpu.VMEM((n_blocks, 2 * LANES, 2 * tq), BF16),
                        pltpu.VMEM((tq, 2 * tq), F32),
                        pltpu.VMEM((2, tq, 2 * tq), F32),
                        pltpu.VMEM((2, tq, 2 * tq), BF16),
                        pltpu.VMEM((2, acc_rows, 2 * tq), F32),
                        pltpu.VMEM((4, 8, 2 * tq), F32),
                        pltpu.VMEM((n_blocks, 8, 2 * tq), F32),
                        pltpu.VMEM((n_blocks, acc_rows, 2 * tq), F32),
                        pltpu.SMEM((n_pairs,), jnp.int32),
                        pltpu.SMEM((n_pairs,), jnp.int32)],
        compiler_params=pltpu.CompilerParams(dimension_semantics=("arbitrary",) * 2,
                                             vmem_limit_bytes=VMEM_LIMIT),
        name="moba_prompt",
    )(slopes, qt, k_rows, vt, kmean)


def _suffix_matrix(n):
    j = lax.broadcasted_iota(jnp.int32, (n, n), 0)
    s = lax.broadcasted_iota(jnp.int32, (n, n), 1)
    return jnp.where(j > s, 1.0, 0.0).astype(BF16)


def _suffix_sums(x, t2):
    hi, lo = _split(x)
    return _dot(jnp.concatenate([hi, lo], axis=1), t2)


def _rank_before_lanes(g, n_cand):
    lane = lax.broadcasted_iota(jnp.int32, g.shape, 1)
    cnt = jnp.zeros(g.shape, jnp.int32)
    for m in range(n_cand):
        gm = g[:, m:m + 1]
        beats = (gm > g) | ((gm == g) & (m < lane))
        cnt = cnt + jnp.where(beats, 1, 0)
    return cnt, lane


def _sample_kernel(pt_ref, slope_ref, qa_ref, qb_ref, kan_ref, van_ref, kbn_ref, vbn_ref,
                   cka_hbm, cva_hbm, *rest):
    pps = PAGES_PER_STEP
    kb_refs, vb_refs = (rest[i * pps:(i + 1) * pps] for i in range(2))
    (oa_ref, ob_ref, t2_ref, qa_s, qb_s, run_s, acca_s, gate_s, mx_s, l_s, accb_s,
     mo_s, lo_s, acco_s, new_s, live_s, ka_ring, va_ring, ring_sem, inflight_s) = rest[2 * pps:]
    b, j = pl.program_id(0), pl.program_id(1)
    n_batch = pl.num_programs(0)
    n_steps = pl.num_programs(1)
    blk = MOBA_BLOCK
    bps = pps * NEW_PAD // blk
    n_cached = accb_s.shape[0]
    n_first = (n_steps - 1 - j) * bps
    n_tok = qa_ref.shape[1]
    n_rows = n_tok * N_HEADS

    row = lax.broadcasted_iota(jnp.int32, (n_rows, GROUP_WIDTH), 0)
    lane_w = lax.broadcasted_iota(jnp.int32, (n_rows, GROUP_WIDTH), 1)
    head_mask = (lane_w // HEAD_DIM) == (row % N_HEADS)
    slope = slope_ref[:, 0:1]
    lane = lax.broadcasted_iota(jnp.int32, (n_rows, LANES), 1)

    @pl.when(j == 0)
    def _():
        t = _suffix_matrix(blk)
        t2_ref[0:blk, :] = t
        t2_ref[blk:, :] = t
        live_s[0] = 1

        def expand(q):
            rep = jnp.concatenate([jnp.broadcast_to(q[t:t + 1], (N_HEADS, GROUP_WIDTH))
                                   for t in range(n_tok)], axis=0)
            return jnp.where(head_mask, rep, 0.0)

        qa_s[...] = (expand(qa_ref[0]) * QK_SCALE).astype(BF16)
        qh, ql = _split(expand(qb_ref[0]))
        qb_s[...] = jnp.concatenate([qh, ql], axis=0)
        gate_s[...] = jnp.zeros_like(gate_s)
        mx_s[...] = jnp.zeros_like(mx_s)
        l_s[...] = jnp.zeros_like(l_s)

        def padded(ref):
            new_s[...] = jnp.zeros_like(new_s)
            new_s[0:n_tok, :] = ref[0]
            return new_s[...].astype(BF16)

        tok = (lax.broadcasted_iota(jnp.int32, (n_rows, NEW_PAD), 0) // N_HEADS)
        col = lax.broadcasted_iota(jnp.int32, (n_rows, NEW_PAD), 1)
        z = _dot_nt(qa_s[...], padded(kan_ref))
        strict = col < tok
        sp = _softplus(z)
        log_keep = jnp.where(strict, -sp, 0.0)
        hi, lo = _split(log_keep)
        tn = t2_ref[0:NEW_PAD, 0:NEW_PAD]
        between = _dot(hi, tn) + _dot(lo, tn)
        w = jnp.where(strict, jnp.exp((z - sp) + between), 0.0)
        run_s[...] = jnp.sum(log_keep, axis=1, keepdims=True)
        acca_s[...] = _dot(w.astype(BF16), padded(van_ref))
        s = _dot_nt(qb_s[0:n_rows, :], padded(kbn_ref)) * QK_SCALE
        s = s - slope * (tok - col).astype(F32)
        s = jnp.where(col <= tok, s, NEG_BIG)
        m_o = jnp.max(s, axis=1, keepdims=True)
        p = jnp.exp(s - m_o)
        mo_s[...] = m_o
        lo_s[...] = jnp.sum(p, axis=1, keepdims=True)
        acco_s[...] = _dot(p.astype(BF16), padded(vbn_ref))

    def pages(refs):
        return jnp.concatenate([r[0] for r in refs], axis=1)

    def ring_pages(ring, slot):
        return jnp.concatenate([ring[slot, p] for p in range(pps)], axis=1)

    def ring_copies(bb, jj, slot):
        first = pps * (n_steps - 1 - jj)
        copies = []
        for p in range(pps):
            pg = pt_ref[bb, first + p]
            copies.append(pltpu.make_async_copy(cka_hbm.at[pg], ka_ring.at[slot, p], ring_sem.at[0, slot]))
            copies.append(pltpu.make_async_copy(cva_hbm.at[pg], va_ring.at[slot, p], ring_sem.at[1, slot]))
        return copies

    slot = j & 1

    @pl.when((b == 0) & (j == 0))
    def _():
        for c in ring_copies(0, 0, 0):
            c.start()
        inflight_s[0] = 1

    last = j == n_steps - 1
    want_next = jnp.where(last, b + 1 < n_batch, live_s[0] == 1)

    @pl.when(want_next)
    def _():
        for c in ring_copies(jnp.where(last, b + 1, b), jnp.where(last, 0, j + 1), 1 - slot):
            c.start()

    inflight_s[1 - slot] = want_next.astype(jnp.int32)

    @pl.when(inflight_s[slot] == 1)
    def _():
        for c in ring_copies(b, j, slot):
            c.wait()

    @pl.when(live_s[0] == 1)
    def _():
        z = _dot(qa_s[...], ring_pages(ka_ring, slot).astype(BF16))
        sp = _softplus(z)
        log_keep = -sp
        t2 = t2_ref[...]
        run = run_s[...]
        betweens = [None] * bps
        for i in reversed(range(bps)):
            lk = log_keep[:, i * blk:(i + 1) * blk]
            betweens[i] = _suffix_sums(lk, t2) + run
            run = run + jnp.sum(lk, axis=1, keepdims=True)
        run_s[...] = run
        live_s[0] = (jnp.max(run) > RUN_DEAD_LN).astype(jnp.int32)
        w = jnp.exp((z - sp) + jnp.concatenate(betweens, axis=1))
        acca_s[...] = acca_s[...] + _dot_nt(w.astype(BF16), ring_pages(va_ring, slot).astype(BF16))

    kb_h, kb_l = _split(pages(kb_refs))
    vb = pages(vb_refs).astype(BF16)
    raw = _dot(qb_s[...], kb_h)
    raw_hh = raw[0:n_rows]
    raw3 = raw_hh + (raw[n_rows:] + _dot(qb_s[0:n_rows, :], kb_l))
    tok = lax.broadcasted_iota(jnp.int32, (n_rows, blk), 0) // N_HEADS
    col = lax.broadcasted_iota(jnp.int32, (n_rows, blk), 1)
    for i in range(bps):
        n = n_first + i
        cols = slice(i * blk, (i + 1) * blk)
        gate_n = jnp.sum(raw3[:, cols], axis=1, keepdims=True) * (1.0 / blk)
        dist = ((n_cached - n) * blk + tok - col).astype(F32)
        s = raw_hh[:, cols] * QK_SCALE - slope * dist
        m_n = jnp.max(s, axis=1, keepdims=True)
        p = jnp.exp(s - m_n)
        l_n = jnp.sum(p, axis=1, keepdims=True)
        here = lane == n
        gate_s[...] = jnp.where(here, gate_n, gate_s[...])
        mx_s[...] = jnp.where(here, m_n, mx_s[...])
        l_s[...] = jnp.where(here, l_n, l_s[...])
        accb_s[n] = _dot_nt(p.astype(BF16), vb[:, cols])

    @pl.when(j == n_steps - 1)
    def _():
        def fold(acc):
            kept = jnp.where(head_mask, acc, 0.0)
            return jnp.sum(kept.reshape(n_tok, N_HEADS, GROUP_WIDTH), axis=1)

        oa_ref[0] = fold(acca_s[...])
        cnt, ln = _rank_before_lanes(gate_s[...], n_cached)
        sel = (ln < n_cached) & (cnt < MOBA_TOPK)
        mx = jnp.where(sel, mx_s[...], NEG_BIG)
        m_o = mo_s[...]
        m_all = jnp.maximum(m_o, jnp.max(mx, axis=1, keepdims=True))
        c = jnp.where(sel, jnp.exp(mx - m_all), 0.0)
        c_o = jnp.exp(m_o - m_all)
        l_tot = c_o * lo_s[...] + jnp.sum(c * l_s[...], axis=1, keepdims=True)
        acc = c_o * acco_s[...]
        for nb in range(n_cached):
            acc = acc + c[:, nb:nb + 1] * accb_s[nb]
        ob_ref[0] = fold(acc / l_tot)


def _sample_attention(page_table, slope_rows, qa, qb, ka_new, va_new, kb_new, vb_new,
                      ck_sb, cv_sb, ck_mb, cv_mb):
    dbs, n_tok, _ = qa.shape
    n_pages = page_table.shape[1]
    page = ck_sb.shape[2]
    pps = PAGES_PER_STEP
    assert page == NEW_PAD and n_pages % pps == 0 and (pps * page) % MOBA_BLOCK == 0
    n_steps = n_pages // pps
    assert n_steps % 2 == 0, "the two-slot page ring alternates slots across consecutive grid steps"
    n_cached = n_pages * page // MOBA_BLOCK
    assert n_cached <= LANES
    n_rows = n_tok * N_HEADS

    tok_spec = pl.BlockSpec((1, n_tok, GROUP_WIDTH), lambda b, j, pt: (b, 0, 0))

    def page_spec(which):
        return pl.BlockSpec((1, GROUP_WIDTH, page),
                            lambda b, j, pt: (pt[b, pps * (n_steps - 1 - j) + which], 0, 0))

    grid_spec = pltpu.PrefetchScalarGridSpec(
        num_scalar_prefetch=1,
        grid=(dbs, n_steps),
        in_specs=[pl.BlockSpec((n_rows, LANES), lambda b, j, pt: (0, 0))] + [tok_spec] * 6
                 + [pl.BlockSpec(memory_space=pl.ANY)] * 2 + [page_spec(p) for p in range(pps)] * 2,
        out_specs=[tok_spec, tok_spec],
        scratch_shapes=[
            pltpu.VMEM((2 * MOBA_BLOCK, MOBA_BLOCK), BF16),
            pltpu.VMEM((n_rows, GROUP_WIDTH), BF16),
            pltpu.VMEM((2 * n_rows, GROUP_WIDTH), BF16),
            pltpu.VMEM((n_rows, 1), F32),
            pltpu.VMEM((n_rows, GROUP_WIDTH), F32),
            pltpu.VMEM((n_rows, LANES), F32),
            pltpu.VMEM((n_rows, LANES), F32),
            pltpu.VMEM((n_rows, LANES), F32),
            pltpu.VMEM((n_cached, n_rows, GROUP_WIDTH), F32),
            pltpu.VMEM((n_rows, 1), F32),
            pltpu.VMEM((n_rows, 1), F32),
            pltpu.VMEM((n_rows, GROUP_WIDTH), F32),
            pltpu.VMEM((NEW_PAD, GROUP_WIDTH), F32),
            pltpu.SMEM((1,), jnp.int32),
            pltpu.VMEM((2, pps, GROUP_WIDTH, page), F32),
            pltpu.VMEM((2, pps, GROUP_WIDTH, page), F32),
            pltpu.SemaphoreType.DMA((2, 2)),
            pltpu.SMEM((2,), jnp.int32),
        ],
    )
    out = jax.ShapeDtypeStruct((dbs, n_tok, GROUP_WIDTH), F32)
    caches = [ck_sb, cv_sb] + [c for c in (ck_mb, cv_mb) for _ in range(pps)]
    return pl.pallas_call(
        _sample_kernel,
        grid_spec=grid_spec,
        out_shape=[out, out],
        compiler_params=pltpu.CompilerParams(dimension_semantics=("arbitrary", "arbitrary"),
                                             vmem_limit_bytes=VMEM_LIMIT),
        name="sample_attn",
    )(page_table, slope_rows, qa, qb, ka_new, va_new, kb_new, vb_new, *caches)


def _mix_out_kernel(alpha, x_ref, oa_ref, ga_ref, ob_ref, gb_ref, w_ref, gain_ref, bias_ref, y_ref):
    def gated(o_ref, g_ref):
        return o_ref[...].astype(F32) * jax.nn.silu(g_ref[...].astype(F32))

    h = jnp.concatenate([gated(oa_ref, ga_ref), gated(ob_ref, gb_ref)], axis=-1)
    out = _dot(h.astype(BF16), w_ref[...])
    y = alpha * x_ref[...] + out
    mu = jnp.mean(y, axis=-1, keepdims=True)
    var = jnp.mean(jnp.square(y - mu), axis=-1, keepdims=True)
    y_ref[...] = (y - mu) * lax.rsqrt(var + LN_EPS) * gain_ref[...] + bias_ref[...]


def _mix_out(x2d, oa, ga, ob, gb, w_out, gain, bias, alpha, block_rows):
    rows, d_model = x2d.shape
    g_spec = pl.BlockSpec((block_rows, GROUP_WIDTH), lambda i: (i, 0))
    x_spec = pl.BlockSpec((block_rows, d_model), lambda i: (i, 0))
    return pl.pallas_call(
        functools.partial(_mix_out_kernel, alpha),
        grid=(rows // block_rows,),
        in_specs=[x_spec, g_spec, g_spec, g_spec, g_spec,
                  _full_spec(w_out), _full_spec(gain), _full_spec(bias)],
        out_specs=x_spec,
        out_shape=jax.ShapeDtypeStruct(x2d.shape, F32),
        compiler_params=pltpu.CompilerParams(dimension_semantics=("arbitrary",),
                                             vmem_limit_bytes=VMEM_LIMIT),
        name="mix_out",
    )(x2d, oa, ga, ob, gb, w_out, gain, bias)


def kernel(x_prompt, x_sample, cache_k_sb, cache_v_sb, cache_k_moba, cache_v_moba, page_table,
           w_in, w_out, ln_gain, ln_bias):
    depth = w_in.shape[0]
    assert depth == 1, "single-layer trunk"
    bsz, seq, d_model = x_prompt.shape
    dbs, n_tok, _ = x_sample.shape
    alpha = (2.0 * depth) ** 0.25
    slopes = jnp.asarray([2.0 ** (-8.0 * (i + 1) / N_HEADS) for i in range(N_HEADS)], dtype=F32)
    slope_rows = jnp.broadcast_to(jnp.tile(slopes, n_tok)[:, None], (n_tok * N_HEADS, LANES))

    g = GROUP_WIDTH
    w = w_in[0]
    w_hi = w.astype(BF16)
    w_lo = (w[:, 4 * g:6 * g] - w_hi[:, 4 * g:6 * g].astype(F32)).astype(BF16)

    def col(a, c):
        return a[:, c * g:(c + 1) * g]

    w_row = jnp.concatenate([col(w_hi, c) for c in ROW_GROUPS], axis=1)
    w_row_lo = col(w_lo, 1)
    w_feat = jnp.concatenate([col(w_hi, c) for c in FEAT_GROUPS], axis=1).T
    w_feat_lo = col(w_lo, 0).T
    w_o = w_out[0].astype(BF16)
    gain = ln_gain[0][None, :]
    bias = ln_bias[0][None, :]

    def pages(c):
        return jnp.transpose(c[0], (0, 2, 3, 1)).reshape(c.shape[1], GROUP_WIDTH, c.shape[2])

    def heads_from_features(a):
        return jnp.transpose(a.reshape(bsz, N_HEADS, HEAD_DIM, seq), (0, 3, 1, 2))[None]

    def heads_from_rows(a):
        return a.reshape(1, dbs, n_tok, N_HEADS, HEAD_DIM)

    xp = x_prompt.reshape(bsz * seq, d_model)
    (ga, gb, ka_rows, kb_rows, kmean, qat, qbt, kat, vat, kbt, vbt) = _project_prompt(
        xp, w_row, w_row_lo, w_feat, w_feat_lo, bsz, 512)
    shp = (bsz, seq, GROUP_WIDTH)
    o_a = _sb_prompt(qat, ka_rows.reshape(shp), vat)
    o_b = _moba_prompt(slopes, qbt, kb_rows.reshape(shp), vbt, kmean.reshape(bsz, seq // MOBA_BLOCK, GROUP_WIDTH))
    y_p = _mix_out(xp, o_a.reshape(xp.shape[0], GROUP_WIDTH), ga, o_b.reshape(xp.shape[0], GROUP_WIDTH), gb,
                   w_o, gain, bias, alpha, 512)

    xs = x_sample.reshape(dbs * n_tok, d_model)
    sqa, ska, sva, sga, sqb, skb, svb, sgb = _project_rows(xs, w_hi, w_lo)
    sshp = (dbs, n_tok, GROUP_WIDTH)
    so_a, so_b = _sample_attention(
        page_table, slope_rows, sqa.reshape(sshp), sqb.reshape(sshp),
        ska.reshape(sshp), sva.reshape(sshp), skb.reshape(sshp), svb.reshape(sshp),
        pages(cache_k_sb), pages(cache_v_sb), pages(cache_k_moba), pages(cache_v_moba))
    y_s = _mix_out(xs, so_a.reshape(xs.shape[0], GROUP_WIDTH), sga, so_b.reshape(xs.shape[0], GROUP_WIDTH), sgb,
                   w_o, gain, bias, alpha, dbs * n_tok)

    return (y_p.reshape(x_prompt.shape), y_s.reshape(x_sample.shape),
            heads_from_features(kat), heads_from_features(vat),
            heads_from_features(kbt), heads_from_features(vbt),
            heads_from_rows(ska), heads_from_rows(sva), heads_from_rows(skb), heads_from_rows(svb))
```

```python
import functools
import math

import jax
import jax.numpy as jnp
from jax import lax
from jax.experimental import pallas as pl
from jax.experimental.pallas import tpu as pltpu

HEAD_DIM = 64
N_HEADS = 8
GROUP_WIDTH = N_HEADS * HEAD_DIM
HEAD_PAIR_WIDTH = 2 * HEAD_DIM
LANES = 128
MOBA_BLOCK = 256
MOBA_TOPK = 3
ATT_BLOCK = 256
QK_SCALE = HEAD_DIM ** -0.5
LOG2E = math.log2(math.e)
LN_EPS = 1e-5
NEG_BIG = -1e30
SOFTPLUS2_CLAMP = 100.0
RUN_DEAD = -160.0
RUN_DEAD_LN = -112.0
NEW_PAD = 128
PAGES_PER_STEP = 8
SB_GROUP = 2
ONES_ROWS = 16
VMEM_LIMIT = 48 * 1024 * 1024

AUG_ONE = 0
AUG_KEYPOS = 3
AUG_BLOCK = 8

F32 = jnp.float32
BF16 = jnp.bfloat16


def _dot(a, b):
    return jnp.dot(a, b, preferred_element_type=F32)


def _dot_nt(a, b):
    return lax.dot_general(a, b, (((1,), (1,)), ((), ())), preferred_element_type=F32)


def _split(x):
    hi = x.astype(BF16)
    lo = (x - hi.astype(F32)).astype(BF16)
    return hi, lo


def _split3(x):
    hi = x.astype(BF16)
    r = x - hi.astype(F32)
    mid = r.astype(BF16)
    lo = (r - mid.astype(F32)).astype(BF16)
    return hi, mid, lo


def _dot_3pass(a, b):
    ah, al = _split(a)
    bh, bl = _split(b)
    return _dot(ah, bh) + (_dot(ah, bl) + _dot(al, bh))


def _softplus(z):
    return jnp.maximum(z, 0.0) + jnp.log(1.0 + jnp.exp(-jnp.abs(z)))


def _softplus2(z):
    return jnp.maximum(jnp.log2(1.0 + jnp.exp2(jnp.minimum(z, SOFTPLUS2_CLAMP))), z)


def _full_spec(a, single_buffer=False):
    mode = pl.Buffered(1) if single_buffer else None
    return pl.BlockSpec(a.shape, lambda *_: (0,) * a.ndim, pipeline_mode=mode)


def _proj_rows_kernel(x_ref, w_ref, wlo_ref, *out_refs):
    xh, xl = _split(x_ref[...])
    g = GROUP_WIDTH
    for c, o_ref in enumerate(out_refs):
        wc = w_ref[:, c * g:(c + 1) * g]
        out = _dot(xh, wc)
        if c in (4, 5):
            out = out + (_dot(xh, wlo_ref[:, (c - 4) * g:(c - 3) * g]) + _dot(xl, wc))
        o_ref[...] = out


def _project_rows(x2d, w_hi, w_lo):
    rows, _ = x2d.shape
    out = jax.ShapeDtypeStruct((rows, GROUP_WIDTH), F32)
    return pl.pallas_call(
        _proj_rows_kernel,
        grid=(1,),
        in_specs=[_full_spec(x2d), _full_spec(w_hi), _full_spec(w_lo)],
        out_specs=[pl.BlockSpec((rows, GROUP_WIDTH), lambda i: (0, 0))] * 8,
        out_shape=[out] * 8,
        compiler_params=pltpu.CompilerParams(dimension_semantics=("arbitrary",),
                                             vmem_limit_bytes=VMEM_LIMIT),
        name="proj_rows",
    )(x2d, w_hi, w_lo)


ROW_GROUPS = (3, 7, 1, 5)
FEAT_GROUPS = (0, 4, 1, 2, 5, 6)
FEAT_3PASS = (1,)


def _proj_prompt_kernel(x_ref, wr_ref, wrlo_ref, wf_ref, wflo_ref,
                        ga_ref, gb_ref, kar_ref, kbr_ref, kmean_ref,
                        qat_ref, qbt_ref, kat_ref, vat_ref, kbt_ref, vbt_ref):
    x = x_ref[...]
    xh, xl = _split(x)
    g = GROUP_WIDTH
    for c, o_ref in enumerate((ga_ref, gb_ref)):
        o_ref[...] = _dot(xh, wr_ref[:, c * g:(c + 1) * g]).astype(o_ref.dtype)
    lo_slot = 0
    token_major_copy = {2: kar_ref, 4: kbr_ref}
    for c, o_ref in enumerate((qat_ref, qbt_ref, kat_ref, vat_ref, kbt_ref, vbt_ref)):
        wc = wf_ref[c * g:(c + 1) * g, :]
        out = _dot_nt(wc, xh)
        if c in FEAT_3PASS:
            out = out + (_dot_nt(wflo_ref[lo_slot * g:(lo_slot + 1) * g, :], xh) + _dot_nt(wc, xl))
            lo_slot += 1
        o_ref[0] = out
        if c in token_major_copy:
            rows_ref = token_major_copy[c]
            rows_ref[...] = out.T.astype(rows_ref.dtype)
    n_blk = x.shape[0] // MOBA_BLOCK
    row8 = lax.broadcasted_iota(jnp.int32, (8, x.shape[1]), 0)
    xbar = jnp.zeros((8, x.shape[1]), F32)
    for i in range(n_blk):
        mean_i = jnp.sum(x[i * MOBA_BLOCK:(i + 1) * MOBA_BLOCK], axis=0, keepdims=True) * (1.0 / MOBA_BLOCK)
        xbar = jnp.where(row8 == i, mean_i, xbar)
    bh, bl = _split(xbar)
    wk = wr_ref[:, 3 * g:4 * g]
    km = _dot(bh, wk) + (_dot(bh, wrlo_ref[...]) + _dot(bl, wk))
    kmean_ref[0] = km[0:n_blk]


def _project_prompt(x2d, w_row, w_row_lo, w_feat, w_feat_lo, batch, block_rows):
    rows, d_model = x2d.shape
    seq = rows // batch
    per_seq = seq // block_rows
    n_blk = block_rows // MOBA_BLOCK
    row_spec = pl.BlockSpec((block_rows, GROUP_WIDTH), lambda i: (i, 0))
    feat_spec = pl.BlockSpec((1, GROUP_WIDTH, block_rows), lambda i: (i // per_seq, 0, i % per_seq))
    row_bf16 = jax.ShapeDtypeStruct((rows, GROUP_WIDTH), BF16)
    feat = jax.ShapeDtypeStruct((batch, GROUP_WIDTH, seq), F32)
    kmean = jax.ShapeDtypeStruct((rows // block_rows, n_blk, GROUP_WIDTH), F32)
    return pl.pallas_call(
        _proj_prompt_kernel,
        grid=(rows // block_rows,),
        in_specs=[pl.BlockSpec((block_rows, d_model), lambda i: (i, 0)),
                  _full_spec(w_row, True), _full_spec(w_row_lo, True),
                  _full_spec(w_feat, True), _full_spec(w_feat_lo, True)],
        out_specs=[row_spec] * 4 + [pl.BlockSpec((1, n_blk, GROUP_WIDTH), lambda i: (i, 0, 0))] + [feat_spec] * 6,
        out_shape=[row_bf16] * 4 + [kmean] + [feat] * 6,
        compiler_params=pltpu.CompilerParams(dimension_semantics=("arbitrary",),
                                             vmem_limit_bytes=VMEM_LIMIT),
        name="proj_prompt",
    )(x2d, w_row, w_row_lo, w_feat, w_feat_lo)


def _pair_columns(qt, tq):
    row = lax.broadcasted_iota(jnp.int32, qt.shape, 0)
    zero = jnp.zeros_like(qt)
    return jnp.concatenate([jnp.where(row < HEAD_DIM, qt, zero),
                            jnp.where(row >= HEAD_DIM, qt, zero)], axis=1)


def _pair_merge_rows(acc_t, tq):
    row = lax.broadcasted_iota(jnp.int32, (HEAD_PAIR_WIDTH, tq), 0)
    return jnp.where(row < HEAD_DIM, acc_t[:, :tq], acc_t[:, tq:]).T


def _blk(kj):
    return pl.ds(pl.multiple_of(kj * ATT_BLOCK, ATT_BLOCK), ATT_BLOCK)


def _head_pair_specs(seq):
    feat_spec = pl.BlockSpec((1, HEAD_PAIR_WIDTH, seq), lambda b, h: (b, h, 0))
    rows_spec = pl.BlockSpec((1, seq, HEAD_PAIR_WIDTH), lambda b, h: (b, 0, h))
    return feat_spec, rows_spec


def _sb_prompt_kernel(qt_ref, k_ref, vt_ref, o_ref, u_ref, vtb_ref, run_s, acc_s):
    b, hp = pl.program_id(0), pl.program_id(1)
    tq = ATT_BLOCK
    half = tq // 2
    u_rows = half + ONES_ROWS
    n_blocks = k_ref.shape[1] // tq

    @pl.when((b == 0) & (hp == 0))
    def _():
        s_i = lax.broadcasted_iota(jnp.int32, (u_rows, tq), 0)
        j_i = lax.broadcasted_iota(jnp.int32, (u_rows, tq), 1) & (half - 1)
        u_ref[...] = jnp.where((j_i >= s_i) | (s_i >= half), -1.0, 0.0).astype(BF16)

    vtb_ref[...] = vt_ref[0].astype(BF16)
    refs = (qt_ref, k_ref, o_ref, u_ref, vtb_ref, run_s, acc_s)

    def group(q_first, count):
        blocks = [_SbQueryBlock(refs, q_first + n, n) for n in range(count)]
        for blk in blocks:
            blk.reset()
        work = [(blk, t) for blk in blocks for t in (0, 1)]
        scored = [blk.scores(t, t == 0) for blk, t in work]
        locs = [blk.suffix(halves) for (blk, _), (_, halves) in zip(work, scored)]
        for (blk, t), (z, _), loc in zip(work, scored, locs):
            blk.weigh(t, z, loc)
        for blk in blocks:
            blk.earlier_blocks()
            blk.store()

    first = _SbQueryBlock(refs, 0, 0)
    first.reset()
    first.single(0, True)
    first.store()
    group(1, SB_GROUP - 1)

    def grouped(i, carry):
        group(i * SB_GROUP, SB_GROUP)
        return carry

    lax.fori_loop(1, n_blocks // SB_GROUP, grouped, 0)


class _SbQueryBlock:
    def __init__(self, refs, qi, slot):
        (self.qt_ref, self.k_ref, self.o_ref, self.u_ref, self.vtb_ref, self.run_s, self.acc_s) = refs
        self.qi, self.slot = qi, slot
        q = self.qt_ref[0, :, _blk(qi)] * (QK_SCALE * LOG2E)
        self.q_cols = _pair_columns(q.astype(BF16), ATT_BLOCK)

    def reset(self):
        self.run_s[self.slot] = jnp.zeros(self.run_s.shape[1:], F32)
        self.acc_s[self.slot] = jnp.zeros(self.acc_s.shape[1:], F32)

    def scores(self, t, diagonal):
        tq, half = ATT_BLOCK, ATT_BLOCK // 2
        z = _dot(self.k_ref[0, _blk(self.qi - t), :], self.q_cols)
        sp = _softplus2(z)
        if diagonal:
            key = lax.broadcasted_iota(jnp.int32, z.shape, 0)
            qry = lax.broadcasted_iota(jnp.int32, z.shape, 1) & (tq - 1)
            strict = key < qry
            sp = jnp.where(strict, sp, 0.0)
            z = jnp.where(strict, z, NEG_BIG)
        hi, lo = _split(sp)
        halves = [jnp.concatenate([hi[h * half:(h + 1) * half], lo[h * half:(h + 1) * half]], axis=0)
                  for h in range(2)]
        return z, halves

    def suffix(self, halves):
        neg_u = self.u_ref[...]
        return [_dot(neg_u, hl) for hl in halves]

    def weigh(self, t, z, locs):
        tq, half = ATT_BLOCK, ATT_BLOCK // 2
        run = self.run_s[self.slot]
        tot_first = locs[0][half:half + 1]
        tot_second = locs[1][half:half + 1]
        first = (z[0:half] + locs[0][0:half]) + (run + tot_second)
        second = (z[half:tq] + locs[1][0:half]) + run
        w = jnp.exp2(jnp.concatenate([first, second], axis=0)).astype(BF16)
        self.run_s[self.slot] = run + (tot_first + tot_second)
        self.acc_s[self.slot] = self.acc_s[self.slot] + _dot(self.vtb_ref[:, _blk(self.qi - t)], w)

    def single(self, t, diagonal):
        z, halves = self.scores(t, diagonal)
        self.weigh(t, z, self.suffix(halves))

    def pair(self, t, diagonal):
        z0, h0 = self.scores(t, diagonal)
        z1, h1 = self.scores(t + 1, False)
        l0 = self.suffix(h0)
        l1 = self.suffix(h1)
        self.weigh(t, z0, l0)
        self.weigh(t + 1, z1, l1)

    def alive(self):
        return jnp.max(self.run_s[self.slot]) > RUN_DEAD

    def earlier_blocks(self):
        qi = self.qi

        def more_pairs(c):
            return (c[0] + 1 <= qi) & c[1]

        def next_pair(c):
            self.pair(c[0], False)
            return c[0] + 2, self.alive()

        t, live = lax.while_loop(more_pairs, next_pair, (jnp.int32(2), self.alive()))

        @pl.when((t == qi) & live)
        def _():
            self.single(t, False)

    def store(self):
        out = _pair_merge_rows(self.acc_s[self.slot], ATT_BLOCK)
        self.o_ref[0, _blk(self.qi), :] = out.astype(self.o_ref.dtype)


def _sb_prompt(qt, k_rows, vt):
    bsz, _, seq = qt.shape
    tq = ATT_BLOCK
    assert seq % (SB_GROUP * tq) == 0, "query blocks are processed SB_GROUP at a time"
    feat_spec, rows_spec = _head_pair_specs(seq)
    return pl.pallas_call(
        _sb_prompt_kernel,
        grid=(bsz, GROUP_WIDTH // HEAD_PAIR_WIDTH),
        in_specs=[feat_spec, rows_spec, feat_spec],
        out_specs=rows_spec,
        out_shape=jax.ShapeDtypeStruct((bsz, seq, GROUP_WIDTH), BF16),
        scratch_shapes=[pltpu.VMEM((tq // 2 + ONES_ROWS, tq), BF16),
                        pltpu.VMEM((HEAD_PAIR_WIDTH, seq), BF16),
                        pltpu.VMEM((SB_GROUP, 1, 2 * tq), F32),
                        pltpu.VMEM((SB_GROUP, HEAD_PAIR_WIDTH, 2 * tq), F32)],
        compiler_params=pltpu.CompilerParams(dimension_semantics=("arbitrary",) * 2,
                                             vmem_limit_bytes=VMEM_LIMIT),
        name="sb_prompt",
    )(qt, k_rows, vt)


def _rank_before_rows(g, n_valid, n_cand):
    blk = lax.broadcasted_iota(jnp.int32, g.shape, 0)
    cnt = jnp.zeros(g.shape, jnp.int32)
    for m in range(n_cand):
        gm = g[m:m + 1, :]
        beats = (gm > g) | ((gm == g) & (m < blk))
        cnt = cnt + jnp.where(beats & (m < n_valid), 1, 0)
    return cnt, blk


def _block_pairs(n_blocks):
    return [(q, 0) for q in range(n_blocks)] + [(q, t) for q in range(n_blocks) for t in range(1, q + 1)]


def _moba_prompt_kernel(slopes_ref, qt_ref, k_ref, vt_ref, kmean_ref, o_ref,
                        kaug_ref, vaug_ref, qaug_s, mask_s, s_s, p_s, pv_s, top_s, m_all, acc_all,
                        pair_q, pair_t):
    b, hp = pl.program_id(0), pl.program_id(1)
    tq = ATT_BLOCK
    seq = k_ref.shape[1]
    n_blocks = seq // MOBA_BLOCK
    pairs = _block_pairs(n_blocks)
    n_pairs = len(pairs)

    @pl.when((b == 0) & (hp == 0))
    def _():
        for i, (q, t) in enumerate(pairs):
            pair_q[i] = q
            pair_t[i] = t
        key = lax.broadcasted_iota(jnp.int32, (tq, 2 * tq), 0)
        qry = lax.broadcasted_iota(jnp.int32, (tq, 2 * tq), 1) & (tq - 1)
        mask_s[...] = jnp.where(key <= qry, 0.0, NEG_BIG)

    kaug_ref[:, 0:HEAD_PAIR_WIDTH] = k_ref[0]
    lane = lax.broadcasted_iota(jnp.int32, (MOBA_BLOCK, LANES), 1)
    key = lax.broadcasted_iota(jnp.int32, (MOBA_BLOCK, LANES), 0).astype(F32)
    base = jnp.where(lane < AUG_ONE + 3, 1.0,
                     jnp.where(lane < AUG_KEYPOS + 3, key, 0.0))
    for n in range(n_blocks):
        kaug_ref[n * MOBA_BLOCK:(n + 1) * MOBA_BLOCK, HEAD_PAIR_WIDTH:] = (
            jnp.where(lane == AUG_BLOCK + n, 1.0, base).astype(BF16))
    for h in range(2):
        vaug_ref[h, 0:HEAD_DIM, :] = vt_ref[0, h * HEAD_DIM:(h + 1) * HEAD_DIM, :].astype(BF16)
        vaug_ref[h, HEAD_DIM:, :] = jnp.ones((ONES_ROWS, seq), BF16)

    lane = lax.broadcasted_iota(jnp.int32, (1, 2 * tq), 1)
    slope = jnp.where(lane < tq, slopes_ref[2 * hp], slopes_ref[2 * hp + 1]) * LOG2E
    q_off = (lane & (tq - 1)).astype(F32)
    row8 = lax.broadcasted_iota(jnp.int32, (8, 2 * tq), 0)
    small = jnp.zeros((8, 2 * tq), F32)
    for first, terms in ((AUG_ONE, _split3(-slope * q_off)), (AUG_KEYPOS, _split3(slope))):
        for r, v in enumerate(terms):
            small = jnp.where(row8 == first + r, v.astype(F32), small)
    kmean = kmean_ref[0]

    def prepare(qb, carry):
        qt = qt_ref[0, :, _blk(qb)]
        q_cols = _pair_columns(qt, tq)
        gate = _dot_3pass(kmean, q_cols)
        cnt, blk = _rank_before_rows(gate, qb, n_blocks)
        chosen = ((blk < qb) & (cnt < MOBA_TOPK)) | (blk == qb)
        choice_bias = jnp.where(chosen, 0.0, NEG_BIG)
        qaug_s[qb] = jnp.concatenate(
            [q_cols * (QK_SCALE * LOG2E), small, choice_bias,
             jnp.zeros((LANES - 8 - n_blocks, 2 * tq), F32)], axis=0).astype(BF16)
        m_all[qb] = jnp.full(m_all.shape[1:], NEG_BIG, F32)
        acc_all[qb] = jnp.zeros(acc_all.shape[1:], F32)
        return carry

    lax.fori_loop(0, n_blocks, prepare, 0, unroll=4)

    def score(i, slot):
        q, t = pair_q[i], pair_t[i]
        s_s[slot & 1] = _dot(kaug_ref[_blk(q - t), :], qaug_s[q])

    def probs(i, slot):
        t = pair_t[i]
        s = s_s[slot & 1]
        if isinstance(i, int) and i < n_blocks:
            s = s + mask_s[...]
        top = jnp.max(s, axis=0, keepdims=True)
        p_s[slot & 1] = jnp.exp2(s - top).astype(BF16)
        top_s[slot & 3, 0:1, :] = top - slope * (t * MOBA_BLOCK).astype(F32)

    def value(i, slot):
        q, t = pair_q[i], pair_t[i]
        for h in range(2):
            cols = slice(h * tq, (h + 1) * tq)
            pv_s[slot & 1, :, cols] = _dot(vaug_ref[h, :, _blk(q - t)], p_s[slot & 1, :, cols])

    def merge(i, slot):
        q = pair_q[i]
        top = top_s[slot & 3, 0:1, :]
        m_run = m_all[q, 0:1, :]
        m_new = jnp.maximum(m_run, top)
        acc_all[q] = acc_all[q] * jnp.exp2(m_run - m_new) + pv_s[slot & 1] * jnp.exp2(top - m_new)
        m_all[q, 0:1, :] = m_new

    def tick(i, slot, first=0, last=3):
        stages = (score, probs, value, merge)
        for k in range(first, last + 1):
            stages[k](i - k, slot - k)

    depth = 3
    unroll = 8
    n_static = depth + unroll * -(-(n_blocks + 1 - depth) // unroll)
    for i in range(n_static):
        tick(i, i, last=min(i, depth))
    n_groups = (n_pairs - n_static) // unroll

    def steady(g, carry):
        base = n_static + g * unroll
        for u in range(unroll):
            tick(base + u, n_static + u)
        return carry

    lax.fori_loop(0, n_groups, steady, 0)
    for i in range(n_static + n_groups * unroll, n_pairs):
        tick(i, i)
    for k in range(1, depth + 1):
        tick(n_pairs - 1 + k, n_pairs - 1 + k, first=k)

    def finish(qb, carry):
        acc = acc_all[qb]
        out_t = acc[0:HEAD_DIM] / acc[HEAD_DIM:HEAD_DIM + 1]
        rows = jnp.concatenate([out_t[:, :tq], out_t[:, tq:]], axis=0).T
        o_ref[0, _blk(qb), :] = rows.astype(o_ref.dtype)
        return carry

    lax.fori_loop(0, n_blocks, finish, 0, unroll=2)


def _moba_prompt(slopes, qt, k_rows, vt, kmean):
    bsz, _, seq = qt.shape
    n_blocks = seq // MOBA_BLOCK
    assert AUG_BLOCK + n_blocks <= LANES
    tq = ATT_BLOCK
    n_pairs = len(_block_pairs(n_blocks))
    acc_rows = HEAD_DIM + ONES_ROWS
    feat_spec, rows_spec = _head_pair_specs(seq)
    return pl.pallas_call(
        _moba_prompt_kernel,
        grid=(bsz, GROUP_WIDTH // HEAD_PAIR_WIDTH),
        in_specs=[pl.BlockSpec(memory_space=pltpu.SMEM), feat_spec, rows_spec, feat_spec,
                  pl.BlockSpec((1, n_blocks, HEAD_PAIR_WIDTH), lambda b, h: (b, 0, h))],
        out_specs=rows_spec,
        out_shape=jax.ShapeDtypeStruct((bsz, seq, GROUP_WIDTH), BF16),
        scratch_shapes=[pltpu.VMEM((seq, HEAD_PAIR_WIDTH + LANES), BF16),
                        pltpu.VMEM((2, acc_rows, seq), BF16),
                        pltpu.VMEM((n_blocks, 2 * LANES, 2 * tq), BF16),
                        pltpu.VMEM((tq, 2 * tq), F32),
                        pltpu.VMEM((2, tq, 2 * tq), F32),
                        pltpu.VMEM((2, tq, 2 * tq), BF16),
                        pltpu.VMEM((2, acc_rows, 2 * tq), F32),
                        pltpu.VMEM((4, 8, 2 * tq), F32),
                        pltpu.VMEM((n_blocks, 8, 2 * tq), F32),
                        pltpu.VMEM((n_blocks, acc_rows, 2 * tq), F32),
                        pltpu.SMEM((n_pairs,), jnp.int32),
                        pltpu.SMEM((n_pairs,), jnp.int32)],
        compiler_params=pltpu.CompilerParams(dimension_semantics=("arbitrary",) * 2,
                                             vmem_limit_bytes=VMEM_LIMIT),
        name="moba_prompt",
    )(slopes, qt, k_rows, vt, kmean)


def _suffix_matrix(n):
    j = lax.broadcasted_iota(jnp.int32, (n, n), 0)
    s = lax.broadcasted_iota(jnp.int32, (n, n), 1)
    return jnp.where(j > s, 1.0, 0.0).astype(BF16)


def _suffix_sums(x, t2):
    hi, lo = _split(x)
    return _dot(jnp.concatenate([hi, lo], axis=1), t2)


def _rank_before_lanes(g, n_cand):
    lane = lax.broadcasted_iota(jnp.int32, g.shape, 1)
    cnt = jnp.zeros(g.shape, jnp.int32)
    for m in range(n_cand):
        gm = g[:, m:m + 1]
        beats = (gm > g) | ((gm == g) & (m < lane))
        cnt = cnt + jnp.where(beats, 1, 0)
    return cnt, lane


def _sample_kernel(pt_ref, slope_ref, qa_ref, qb_ref, kan_ref, van_ref, kbn_ref, vbn_ref,
                   cka_hbm, cva_hbm, *rest):
    pps = PAGES_PER_STEP
    kb_refs, vb_refs = (rest[i * pps:(i + 1) * pps] for i in range(2))
    (oa_ref, ob_ref, t2_ref, qa_s, qb_s, run_s, acca_s, gate_s, mx_s, l_s, accb_s,
     mo_s, lo_s, acco_s, new_s, live_s, ka_ring, va_ring, ring_sem, inflight_s) = rest[2 * pps:]
    b, j = pl.program_id(0), pl.program_id(1)
    n_batch = pl.num_programs(0)
    n_steps = pl.num_programs(1)
    blk = MOBA_BLOCK
    bps = pps * NEW_PAD // blk
    n_cached = accb_s.shape[0]
    n_first = (n_steps - 1 - j) * bps
    n_tok = qa_ref.shape[1]
    n_rows = n_tok * N_HEADS

    row = lax.broadcasted_iota(jnp.int32, (n_rows, GROUP_WIDTH), 0)
    lane_w = lax.broadcasted_iota(jnp.int32, (n_rows, GROUP_WIDTH), 1)
    head_mask = (lane_w // HEAD_DIM) == (row % N_HEADS)
    slope = slope_ref[:, 0:1]
    lane = lax.broadcasted_iota(jnp.int32, (n_rows, LANES), 1)

    @pl.when(j == 0)
    def _():
        t = _suffix_matrix(blk)
        t2_ref[0:blk, :] = t
        t2_ref[blk:, :] = t
        live_s[0] = 1

        def expand(q):
            rep = jnp.concatenate([jnp.broadcast_to(q[t:t + 1], (N_HEADS, GROUP_WIDTH))
                                   for t in range(n_tok)], axis=0)
            return jnp.where(head_mask, rep, 0.0)

        qa_s[...] = (expand(qa_ref[0]) * QK_SCALE).astype(BF16)
        qh, ql = _split(expand(qb_ref[0]))
        qb_s[...] = jnp.concatenate([qh, ql], axis=0)
        gate_s[...] = jnp.zeros_like(gate_s)
        mx_s[...] = jnp.zeros_like(mx_s)
        l_s[...] = jnp.zeros_like(l_s)

        def padded(ref):
            new_s[...] = jnp.zeros_like(new_s)
            new_s[0:n_tok, :] = ref[0]
            return new_s[...].astype(BF16)

        tok = (lax.broadcasted_iota(jnp.int32, (n_rows, NEW_PAD), 0) // N_HEADS)
        col = lax.broadcasted_iota(jnp.int32, (n_rows, NEW_PAD), 1)
        z = _dot_nt(qa_s[...], padded(kan_ref))
        strict = col < tok
        sp = _softplus(z)
        log_keep = jnp.where(strict, -sp, 0.0)
        hi, lo = _split(log_keep)
        tn = t2_ref[0:NEW_PAD, 0:NEW_PAD]
        between = _dot(hi, tn) + _dot(lo, tn)
        w = jnp.where(strict, jnp.exp((z - sp) + between), 0.0)
        run_s[...] = jnp.sum(log_keep, axis=1, keepdims=True)
        acca_s[...] = _dot(w.astype(BF16), padded(van_ref))
        s = _dot_nt(qb_s[0:n_rows, :], padded(kbn_ref)) * QK_SCALE
        s = s - slope * (tok - col).astype(F32)
        s = jnp.where(col <= tok, s, NEG_BIG)
        m_o = jnp.max(s, axis=1, keepdims=True)
        p = jnp.exp(s - m_o)
        mo_s[...] = m_o
        lo_s[...] = jnp.sum(p, axis=1, keepdims=True)
        acco_s[...] = _dot(p.astype(BF16), padded(vbn_ref))

    def pages(refs):
        return jnp.concatenate([r[0] for r in refs], axis=1)

    def ring_pages(ring, slot):
        return jnp.concatenate([ring[slot, p] for p in range(pps)], axis=1)

    def ring_copies(bb, jj, slot):
        first = pps * (n_steps - 1 - jj)
        copies = []
        for p in range(pps):
            pg = pt_ref[bb, first + p]
            copies.append(pltpu.make_async_copy(cka_hbm.at[pg], ka_ring.at[slot, p], ring_sem.at[0, slot]))
            copies.append(pltpu.make_async_copy(cva_hbm.at[pg], va_ring.at[slot, p], ring_sem.at[1, slot]))
        return copies

    slot = j & 1

    @pl.when((b == 0) & (j == 0))
    def _():
        for c in ring_copies(0, 0, 0):
            c.start()
        inflight_s[0] = 1

    last = j == n_steps - 1
    want_next = jnp.where(last, b + 1 < n_batch, live_s[0] == 1)

    @pl.when(want_next)
    def _():
        for c in ring_copies(jnp.where(last, b + 1, b), jnp.where(last, 0, j + 1), 1 - slot):
            c.start()

    inflight_s[1 - slot] = want_next.astype(jnp.int32)

    @pl.when(inflight_s[slot] == 1)
    def _():
        for c in ring_copies(b, j, slot):
            c.wait()

    @pl.when(live_s[0] == 1)
    def _():
        z = _dot(qa_s[...], ring_pages(ka_ring, slot).astype(BF16))
        sp = _softplus(z)
        log_keep = -sp
        t2 = t2_ref[...]
        run = run_s[...]
        betweens = [None] * bps
        for i in reversed(range(bps)):
            lk = log_keep[:, i * blk:(i + 1) * blk]
            betweens[i] = _suffix_sums(lk, t2) + run
            run = run + jnp.sum(lk, axis=1, keepdims=True)
        run_s[...] = run
        live_s[0] = (jnp.max(run) > RUN_DEAD_LN).astype(jnp.int32)
        w = jnp.exp((z - sp) + jnp.concatenate(betweens, axis=1))
        acca_s[...] = acca_s[...] + _dot_nt(w.astype(BF16), ring_pages(va_ring, slot).astype(BF16))

    kb_h, kb_l = _split(pages(kb_refs))
    vb = pages(vb_refs).astype(BF16)
    raw = _dot(qb_s[...], kb_h)
    raw_hh = raw[0:n_rows]
    raw3 = raw_hh + (raw[n_rows:] + _dot(qb_s[0:n_rows, :], kb_l))
    tok = lax.broadcasted_iota(jnp.int32, (n_rows, blk), 0) // N_HEADS
    col = lax.broadcasted_iota(jnp.int32, (n_rows, blk), 1)
    for i in range(bps):
        n = n_first + i
        cols = slice(i * blk, (i + 1) * blk)
        gate_n = jnp.sum(raw3[:, cols], axis=1, keepdims=True) * (1.0 / blk)
        dist = ((n_cached - n) * blk + tok - col).astype(F32)
        s = raw_hh[:, cols] * QK_SCALE - slope * dist
        m_n = jnp.max(s, axis=1, keepdims=True)
        p = jnp.exp(s - m_n)
        l_n = jnp.sum(p, axis=1, keepdims=True)
        here = lane == n
        gate_s[...] = jnp.where(here, gate_n, gate_s[...])
        mx_s[...] = jnp.where(here, m_n, mx_s[...])
        l_s[...] = jnp.where(here, l_n, l_s[...])
        accb_s[n] = _dot_nt(p.astype(BF16), vb[:, cols])

    @pl.when(j == n_steps - 1)
    def _():
        def fold(acc):
            kept = jnp.where(head_mask, acc, 0.0)
            return jnp.sum(kept.reshape(n_tok, N_HEADS, GROUP_WIDTH), axis=1)

        oa_ref[0] = fold(acca_s[...])
        cnt, ln = _rank_before_lanes(gate_s[...], n_cached)
        sel = (ln < n_cached) & (cnt < MOBA_TOPK)
        mx = jnp.where(sel, mx_s[...], NEG_BIG)
        m_o = mo_s[...]
        m_all = jnp.maximum(m_o, jnp.max(mx, axis=1, keepdims=True))
        c = jnp.where(sel, jnp.exp(mx - m_all), 0.0)
        c_o = jnp.exp(m_o - m_all)
        l_tot = c_o * lo_s[...] + jnp.sum(c * l_s[...], axis=1, keepdims=True)
        acc = c_o * acco_s[...]
        for nb in range(n_cached):
            acc = acc + c[:, nb:nb + 1] * accb_s[nb]
        ob_ref[0] = fold(acc / l_tot)


def _sample_attention(page_table, slope_rows, qa, qb, ka_new, va_new, kb_new, vb_new,
                      ck_sb, cv_sb, ck_mb, cv_mb):
    dbs, n_tok, _ = qa.shape
    n_pages = page_table.shape[1]
    page = ck_sb.shape[2]
    pps = PAGES_PER_STEP
    assert page == NEW_PAD and n_pages % pps == 0 and (pps * page) % MOBA_BLOCK == 0
    n_steps = n_pages // pps
    assert n_steps % 2 == 0, "the two-slot page ring alternates slots across consecutive grid steps"
    n_cached = n_pages * page // MOBA_BLOCK
    assert n_cached <= LANES
    n_rows = n_tok * N_HEADS

    tok_spec = pl.BlockSpec((1, n_tok, GROUP_WIDTH), lambda b, j, pt: (b, 0, 0))

    def page_spec(which):
        return pl.BlockSpec((1, GROUP_WIDTH, page),
                            lambda b, j, pt: (pt[b, pps * (n_steps - 1 - j) + which], 0, 0))

    grid_spec = pltpu.PrefetchScalarGridSpec(
        num_scalar_prefetch=1,
        grid=(dbs, n_steps),
        in_specs=[pl.BlockSpec((n_rows, LANES), lambda b, j, pt: (0, 0))] + [tok_spec] * 6
                 + [pl.BlockSpec(memory_space=pl.ANY)] * 2 + [page_spec(p) for p in range(pps)] * 2,
        out_specs=[tok_spec, tok_spec],
        scratch_shapes=[
            pltpu.VMEM((2 * MOBA_BLOCK, MOBA_BLOCK), BF16),
            pltpu.VMEM((n_rows, GROUP_WIDTH), BF16),
            pltpu.VMEM((2 * n_rows, GROUP_WIDTH), BF16),
            pltpu.VMEM((n_rows, 1), F32),
            pltpu.VMEM((n_rows, GROUP_WIDTH), F32),
            pltpu.VMEM((n_rows, LANES), F32),
            pltpu.VMEM((n_rows, LANES), F32),
            pltpu.VMEM((n_rows, LANES), F32),
            pltpu.VMEM((n_cached, n_rows, GROUP_WIDTH), F32),
            pltpu.VMEM((n_rows, 1), F32),
            pltpu.VMEM((n_rows, 1), F32),
            pltpu.VMEM((n_rows, GROUP_WIDTH), F32),
            pltpu.VMEM((NEW_PAD, GROUP_WIDTH), F32),
            pltpu.SMEM((1,), jnp.int32),
            pltpu.VMEM((2, pps, GROUP_WIDTH, page), F32),
            pltpu.VMEM((2, pps, GROUP_WIDTH, page), F32),
            pltpu.SemaphoreType.DMA((2, 2)),
            pltpu.SMEM((2,), jnp.int32),
        ],
    )
    out = jax.ShapeDtypeStruct((dbs, n_tok, GROUP_WIDTH), F32)
    caches = [ck_sb, cv_sb] + [c for c in (ck_mb, cv_mb) for _ in range(pps)]
    return pl.pallas_call(
        _sample_kernel,
        grid_spec=grid_spec,
        out_shape=[out, out],
        compiler_params=pltpu.CompilerParams(dimension_semantics=("arbitrary", "arbitrary"),
                                             vmem_limit_bytes=VMEM_LIMIT),
        name="sample_attn",
    )(page_table, slope_rows, qa, qb, ka_new, va_new, kb_new, vb_new, *caches)


def _mix_out_kernel(alpha, x_ref, oa_ref, ga_ref, ob_ref, gb_ref, w_ref, gain_ref, bias_ref, y_ref):
    def gated(o_ref, g_ref):
        return o_ref[...].astype(F32) * jax.nn.silu(g_ref[...].astype(F32))

    h = jnp.concatenate([gated(oa_ref, ga_ref), gated(ob_ref, gb_ref)], axis=-1)
    out = _dot(h.astype(BF16), w_ref[...])
    y = alpha * x_ref[...] + out
    mu = jnp.mean(y, axis=-1, keepdims=True)
    var = jnp.mean(jnp.square(y - mu), axis=-1, keepdims=True)
    y_ref[...] = (y - mu) * lax.rsqrt(var + LN_EPS) * gain_ref[...] + bias_ref[...]


def _mix_out(x2d, oa, ga, ob, gb, w_out, gain, bias, alpha, block_rows):
    rows, d_model = x2d.shape
    g_spec = pl.BlockSpec((block_rows, GROUP_WIDTH), lambda i: (i, 0))
    x_spec = pl.BlockSpec((block_rows, d_model), lambda i: (i, 0))
    return pl.pallas_call(
        functools.partial(_mix_out_kernel, alpha),
        grid=(rows // block_rows,),
        in_specs=[x_spec, g_spec, g_spec, g_spec, g_spec,
                  _full_spec(w_out), _full_spec(gain), _full_spec(bias)],
        out_specs=x_spec,
        out_shape=jax.ShapeDtypeStruct(x2d.shape, F32),
        compiler_params=pltpu.CompilerParams(dimension_semantics=("arbitrary",),
                                             vmem_limit_bytes=VMEM_LIMIT),
        name="mix_out",
    )(x2d, oa, ga, ob, gb, w_out, gain, bias)


def kernel(x_prompt, x_sample, cache_k_sb, cache_v_sb, cache_k_moba, cache_v_moba, page_table,
           w_in, w_out, ln_gain, ln_bias):
    depth = w_in.shape[0]
    assert depth == 1, "single-layer trunk"
    bsz, seq, d_model = x_prompt.shape
    dbs, n_tok, _ = x_sample.shape
    alpha = (2.0 * depth) ** 0.25
    slopes = jnp.asarray([2.0 ** (-8.0 * (i + 1) / N_HEADS) for i in range(N_HEADS)], dtype=F32)
    slope_rows = jnp.broadcast_to(jnp.tile(slopes, n_tok)[:, None], (n_tok * N_HEADS, LANES))

    g = GROUP_WIDTH
    w = w_in[0]
    w_hi = w.astype(BF16)
    w_lo = (w[:, 4 * g:6 * g] - w_hi[:, 4 * g:6 * g].astype(F32)).astype(BF16)

    def col(a, c):
        return a[:, c * g:(c + 1) * g]

    w_row = jnp.concatenate([col(w_hi, c) for c in ROW_GROUPS], axis=1)
    w_row_lo = col(w_lo, 1)
    w_feat = jnp.concatenate([col(w_hi, c) for c in FEAT_GROUPS], axis=1).T
    w_feat_lo = col(w_lo, 0).T
    w_o = w_out[0].astype(BF16)
    gain = ln_gain[0][None, :]
    bias = ln_bias[0][None, :]

    def pages(c):
        return jnp.transpose(c[0], (0, 2, 3, 1)).reshape(c.shape[1], GROUP_WIDTH, c.shape[2])

    def heads_from_features(a):
        return jnp.transpose(a.reshape(bsz, N_HEADS, HEAD_DIM, seq), (0, 3, 1, 2))[None]

    def heads_from_rows(a):
        return a.reshape(1, dbs, n_tok, N_HEADS, HEAD_DIM)

    xp = x_prompt.reshape(bsz * seq, d_model)
    (ga, gb, ka_rows, kb_rows, kmean, qat, qbt, kat, vat, kbt, vbt) = _project_prompt(
        xp, w_row, w_row_lo, w_feat, w_feat_lo, bsz, 512)
    shp = (bsz, seq, GROUP_WIDTH)
    o_a = _sb_prompt(qat, ka_rows.reshape(shp), vat)
    o_b = _moba_prompt(slopes, qbt, kb_rows.reshape(shp), vbt, kmean.reshape(bsz, seq // MOBA_BLOCK, GROUP_WIDTH))
    y_p = _mix_out(xp, o_a.reshape(xp.shape[0], GROUP_WIDTH), ga, o_b.reshape(xp.shape[0], GROUP_WIDTH), gb,
                   w_o, gain, bias, alpha, 512)

    xs = x_sample.reshape(dbs * n_tok, d_model)
    sqa, ska, sva, sga, sqb, skb, svb, sgb = _project_rows(xs, w_hi, w_lo)
    sshp = (dbs, n_tok, GROUP_WIDTH)
    so_a, so_b = _sample_attention(
        page_table, slope_rows, sqa.reshape(sshp), sqb.reshape(sshp),
        ska.reshape(sshp), sva.reshape(sshp), skb.reshape(sshp), svb.reshape(sshp),
        pages(cache_k_sb), pages(cache_v_sb), pages(cache_k_moba), pages(cache_v_moba))
    y_s = _mix_out(xs, so_a.reshape(xs.shape[0], GROUP_WIDTH), sga, so_b.reshape(xs.shape[0], GROUP_WIDTH), sgb,
                   w_o, gain, bias, alpha, dbs * n_tok)

    return (y_p.reshape(x_prompt.shape), y_s.reshape(x_sample.shape),
            heads_from_features(kat), heads_from_features(vat),
            heads_from_features(kbt), heads_from_features(vbt),
            heads_from_rows(ska), heads_from_rows(sva), heads_from_rows(skb), heads_from_rows(svb))
```

```python
import functools
import math

import jax
import jax.numpy as jnp
from jax import lax
from jax.experimental import pallas as pl
from jax.experimental.pallas import tpu as pltpu

HEAD_DIM = 64
N_HEADS = 8
GROUP_WIDTH = N_HEADS * HEAD_DIM
HEAD_PAIR_WIDTH = 2 * HEAD_DIM
LANES = 128
MOBA_BLOCK = 256
MOBA_TOPK = 3
ATT_BLOCK = 256
QK_SCALE = HEAD_DIM ** -0.5
LOG2E = math.log2(math.e)
LN_EPS = 1e-5
NEG_BIG = -1e30
SOFTPLUS2_CLAMP = 100.0
RUN_DEAD = -160.0
RUN_DEAD_LN = -112.0
NEW_PAD = 128
PAGES_PER_STEP = 8
SB_GROUP = 2
ONES_ROWS = 16
VMEM_LIMIT = 48 * 1024 * 1024

AUG_ONE = 0
AUG_KEYPOS = 3
AUG_BLOCK = 8

F32 = jnp.float32
BF16 = jnp.bfloat16


def _dot(a, b):
    return jnp.dot(a, b, preferred_element_type=F32)


def _dot_nt(a, b):
    return lax.dot_general(a, b, (((1,), (1,)), ((), ())), preferred_element_type=F32)


def _split(x):
    hi = x.astype(BF16)
    lo = (x - hi.astype(F32)).astype(BF16)
    return hi, lo


def _split3(x):
    hi = x.astype(BF16)
    r = x - hi.astype(F32)
    mid = r.astype(BF16)
    lo = (r - mid.astype(F32)).astype(BF16)
    return hi, mid, lo


def _dot_3pass(a, b):
    ah, al = _split(a)
    bh, bl = _split(b)
    return _dot(ah, bh) + (_dot(ah, bl) + _dot(al, bh))


def _softplus(z):
    return jnp.maximum(z, 0.0) + jnp.log(1.0 + jnp.exp(-jnp.abs(z)))


def _softplus2(z):
    return jnp.maximum(jnp.log2(1.0 + jnp.exp2(jnp.minimum(z, SOFTPLUS2_CLAMP))), z)


def _full_spec(a, single_buffer=False):
    mode = pl.Buffered(1) if single_buffer else None
    return pl.BlockSpec(a.shape, lambda *_: (0,) * a.ndim, pipeline_mode=mode)


def _proj_rows_kernel(x_ref, w_ref, wlo_ref, *out_refs):
    xh, xl = _split(x_ref[...])
    g = GROUP_WIDTH
    for c, o_ref in enumerate(out_refs):
        wc = w_ref[:, c * g:(c + 1) * g]
        out = _dot(xh, wc)
        if c in (4, 5):
            out = out + (_dot(xh, wlo_ref[:, (c - 4) * g:(c - 3) * g]) + _dot(xl, wc))
        o_ref[...] = out


def _project_rows(x2d, w_hi, w_lo):
    rows, _ = x2d.shape
    out = jax.ShapeDtypeStruct((rows, GROUP_WIDTH), F32)
    return pl.pallas_call(
        _proj_rows_kernel,
        grid=(1,),
        in_specs=[_full_spec(x2d), _full_spec(w_hi), _full_spec(w_lo)],
        out_specs=[pl.BlockSpec((rows, GROUP_WIDTH), lambda i: (0, 0))] * 8,
        out_shape=[out] * 8,
        compiler_params=pltpu.CompilerParams(dimension_semantics=("arbitrary",),
                                             vmem_limit_bytes=VMEM_LIMIT),
        name="proj_rows",
    )(x2d, w_hi, w_lo)


ROW_GROUPS = (3, 7, 1, 5)
FEAT_GROUPS = (0, 4, 1, 2, 5, 6)
FEAT_3PASS = (1,)


def _proj_prompt_kernel(x_ref, wr_ref, wrlo_ref, wf_ref, wflo_ref,
                        ga_ref, gb_ref, kar_ref, kbr_ref, kmean_ref,
                        qat_ref, qbt_ref, kat_ref, vat_ref, kbt_ref, vbt_ref):
    x = x_ref[...]
    xh, xl = _split(x)
    g = GROUP_WIDTH
    for c, o_ref in enumerate((ga_ref, gb_ref)):
        o_ref[...] = _dot(xh, wr_ref[:, c * g:(c + 1) * g]).astype(o_ref.dtype)
    lo_slot = 0
    token_major_copy = {2: kar_ref, 4: kbr_ref}
    for c, o_ref in enumerate((qat_ref, qbt_ref, kat_ref, vat_ref, kbt_ref, vbt_ref)):
        wc = wf_ref[c * g:(c + 1) * g, :]
        out = _dot_nt(wc, xh)
        if c in FEAT_3PASS:
            out = out + (_dot_nt(wflo_ref[lo_slot * g:(lo_slot + 1) * g, :], xh) + _dot_nt(wc, xl))
            lo_slot += 1
        o_ref[0] = out
        if c in token_major_copy:
            rows_ref = token_major_copy[c]
            rows_ref[...] = out.T.astype(rows_ref.dtype)
    n_blk = x.shape[0] // MOBA_BLOCK
    row8 = lax.broadcasted_iota(jnp.int32, (8, x.shape[1]), 0)
    xbar = jnp.zeros((8, x.shape[1]), F32)
    for i in range(n_blk):
        mean_i = jnp.sum(x[i * MOBA_BLOCK:(i + 1) * MOBA_BLOCK], axis=0, keepdims=True) * (1.0 / MOBA_BLOCK)
        xbar = jnp.where(row8 == i, mean_i, xbar)
    bh, bl = _split(xbar)
    wk = wr_ref[:, 3 * g:4 * g]
    km = _dot(bh, wk) + (_dot(bh, wrlo_ref[...]) + _dot(bl, wk))
    kmean_ref[0] = km[0:n_blk]


def _project_prompt(x2d, w_row, w_row_lo, w_feat, w_feat_lo, batch, block_rows):
    rows, d_model = x2d.shape
    seq = rows // batch
    per_seq = seq // block_rows
    n_blk = block_rows // MOBA_BLOCK
    row_spec = pl.BlockSpec((block_rows, GROUP_WIDTH), lambda i: (i, 0))
    feat_spec = pl.BlockSpec((1, GROUP_WIDTH, block_rows), lambda i: (i // per_seq, 0, i % per_seq))
    row_bf16 = jax.ShapeDtypeStruct((rows, GROUP_WIDTH), BF16)
    feat = jax.ShapeDtypeStruct((batch, GROUP_WIDTH, seq), F32)
    kmean = jax.ShapeDtypeStruct((rows // block_rows, n_blk, GROUP_WIDTH), F32)
    return pl.pallas_call(
        _proj_prompt_kernel,
        grid=(rows // block_rows,),
        in_specs=[pl.BlockSpec((block_rows, d_model), lambda i: (i, 0)),
                  _full_spec(w_row, True), _full_spec(w_row_lo, True),
                  _full_spec(w_feat, True), _full_spec(w_feat_lo, True)],
        out_specs=[row_spec] * 4 + [pl.BlockSpec((1, n_blk, GROUP_WIDTH), lambda i: (i, 0, 0))] + [feat_spec] * 6,
        out_shape=[row_bf16] * 4 + [kmean] + [feat] * 6,
        compiler_params=pltpu.CompilerParams(dimension_semantics=("arbitrary",),
                                             vmem_limit_bytes=VMEM_LIMIT),
        name="proj_prompt",
    )(x2d, w_row, w_row_lo, w_feat, w_feat_lo)


def _pair_columns(qt, tq):
    row = lax.broadcasted_iota(jnp.int32, qt.shape, 0)
    zero = jnp.zeros_like(qt)
    return jnp.concatenate([jnp.where(row < HEAD_DIM, qt, zero),
                            jnp.where(row >= HEAD_DIM, qt, zero)], axis=1)


def _pair_merge_rows(acc_t, tq):
    row = lax.broadcasted_iota(jnp.int32, (HEAD_PAIR_WIDTH, tq), 0)
    return jnp.where(row < HEAD_DIM, acc_t[:, :tq], acc_t[:, tq:]).T


def _blk(kj):
    return pl.ds(pl.multiple_of(kj * ATT_BLOCK, ATT_BLOCK), ATT_BLOCK)


def _head_pair_specs(seq):
    feat_spec = pl.BlockSpec((1, HEAD_PAIR_WIDTH, seq), lambda b, h: (b, h, 0))
    rows_spec = pl.BlockSpec((1, seq, HEAD_PAIR_WIDTH), lambda b, h: (b, 0, h))
    return feat_spec, rows_spec


def _sb_prompt_kernel(qt_ref, k_ref, vt_ref, o_ref, u_ref, vtb_ref, run_s, acc_s):
    b, hp = pl.program_id(0), pl.program_id(1)
    tq = ATT_BLOCK
    half = tq // 2
    u_rows = half + ONES_ROWS
    n_blocks = k_ref.shape[1] // tq

    @pl.when((b == 0) & (hp == 0))
    def _():
        s_i = lax.broadcasted_iota(jnp.int32, (u_rows, tq), 0)
        j_i = lax.broadcasted_iota(jnp.int32, (u_rows, tq), 1) & (half - 1)
        u_ref[...] = jnp.where((j_i >= s_i) | (s_i >= half), -1.0, 0.0).astype(BF16)

    vtb_ref[...] = vt_ref[0].astype(BF16)
    refs = (qt_ref, k_ref, o_ref, u_ref, vtb_ref, run_s, acc_s)

    def group(q_first, count):
        blocks = [_SbQueryBlock(refs, q_first + n, n) for n in range(count)]
        for blk in blocks:
            blk.reset()
        work = [(blk, t) for blk in blocks for t in (0, 1)]
        scored = [blk.scores(t, t == 0) for blk, t in work]
        locs = [blk.suffix(halves) for (blk, _), (_, halves) in zip(work, scored)]
        for (blk, t), (z, _), loc in zip(work, scored, locs):
            blk.weigh(t, z, loc)
        for blk in blocks:
            blk.earlier_blocks()
            blk.store()

    first = _SbQueryBlock(refs, 0, 0)
    first.reset()
    first.single(0, True)
    first.store()
    group(1, SB_GROUP - 1)

    def grouped(i, carry):
        group(i * SB_GROUP, SB_GROUP)
        return carry

    lax.fori_loop(1, n_blocks // SB_GROUP, grouped, 0)


class _SbQueryBlock:
    def __init__(self, refs, qi, slot):
        (self.qt_ref, self.k_ref, self.o_ref, self.u_ref, self.vtb_ref, self.run_s, self.acc_s) = refs
        self.qi, self.slot = qi, slot
        q = self.qt_ref[0, :, _blk(qi)] * (QK_SCALE * LOG2E)
        self.q_cols = _pair_columns(q.astype(BF16), ATT_BLOCK)

    def reset(self):
        self.run_s[self.slot] = jnp.zeros(self.run_s.shape[1:], F32)
        self.acc_s[self.slot] = jnp.zeros(self.acc_s.shape[1:], F32)

    def scores(self, t, diagonal):
        tq, half = ATT_BLOCK, ATT_BLOCK // 2
        z = _dot(self.k_ref[0, _blk(self.qi - t), :], self.q_cols)
        sp = _softplus2(z)
        if diagonal:
            key = lax.broadcasted_iota(jnp.int32, z.shape, 0)
            qry = lax.broadcasted_iota(jnp.int32, z.shape, 1) & (tq - 1)
            strict = key < qry
            sp = jnp.where(strict, sp, 0.0)
            z = jnp.where(strict, z, NEG_BIG)
        hi, lo = _split(sp)
        halves = [jnp.concatenate([hi[h * half:(h + 1) * half], lo[h * half:(h + 1) * half]], axis=0)
                  for h in range(2)]
        return z, halves

    def suffix(self, halves):
        neg_u = self.u_ref[...]
        return [_dot(neg_u, hl) for hl in halves]

    def weigh(self, t, z, locs):
        tq, half = ATT_BLOCK, ATT_BLOCK // 2
        run = self.run_s[self.slot]
        tot_first = locs[0][half:half + 1]
        tot_second = locs[1][half:half + 1]
        first = (z[0:half] + locs[0][0:half]) + (run + tot_second)
        second = (z[half:tq] + locs[1][0:half]) + run
        w = jnp.exp2(jnp.concatenate([first, second], axis=0)).astype(BF16)
        self.run_s[self.slot] = run + (tot_first + tot_second)
        self.acc_s[self.slot] = self.acc_s[self.slot] + _dot(self.vtb_ref[:, _blk(self.qi - t)], w)

    def single(self, t, diagonal):
        z, halves = self.scores(t, diagonal)
        self.weigh(t, z, self.suffix(halves))

    def pair(self, t, diagonal):
        z0, h0 = self.scores(t, diagonal)
        z1, h1 = self.scores(t + 1, False)
        l0 = self.suffix(h0)
        l1 = self.suffix(h1)
        self.weigh(t, z0, l0)
        self.weigh(t + 1, z1, l1)

    def alive(self):
        return jnp.max(self.run_s[self.slot]) > RUN_DEAD

    def earlier_blocks(self):
        qi = self.qi

        def more_pairs(c):
            return (c[0] + 1 <= qi) & c[1]

        def next_pair(c):
            self.pair(c[0], False)
            return c[0] + 2, self.alive()

        t, live = lax.while_loop(more_pairs, next_pair, (jnp.int32(2), self.alive()))

        @pl.when((t == qi) & live)
        def _():
            self.single(t, False)

    def store(self):
        out = _pair_merge_rows(self.acc_s[self.slot], ATT_BLOCK)
        self.o_ref[0, _blk(self.qi), :] = out.astype(self.o_ref.dtype)


def _sb_prompt(qt, k_rows, vt):
    bsz, _, seq = qt.shape
    tq = ATT_BLOCK
    assert seq % (SB_GROUP * tq) == 0, "query blocks are processed SB_GROUP at a time"
    feat_spec, rows_spec = _head_pair_specs(seq)
    return pl.pallas_call(
        _sb_prompt_kernel,
        grid=(bsz, GROUP_WIDTH // HEAD_PAIR_WIDTH),
        in_specs=[feat_spec, rows_spec, feat_spec],
        out_specs=rows_spec,
        out_shape=jax.ShapeDtypeStruct((bsz, seq, GROUP_WIDTH), BF16),
        scratch_shapes=[pltpu.VMEM((tq // 2 + ONES_ROWS, tq), BF16),
                        pltpu.VMEM((HEAD_PAIR_WIDTH, seq), BF16),
                        pltpu.VMEM((SB_GROUP, 1, 2 * tq), F32),
                        pltpu.VMEM((SB_GROUP, HEAD_PAIR_WIDTH, 2 * tq), F32)],
        compiler_params=pltpu.CompilerParams(dimension_semantics=("arbitrary",) * 2,
                                             vmem_limit_bytes=VMEM_LIMIT),
        name="sb_prompt",
    )(qt, k_rows, vt)


def _rank_before_rows(g, n_valid, n_cand):
    blk = lax.broadcasted_iota(jnp.int32, g.shape, 0)
    cnt = jnp.zeros(g.shape, jnp.int32)
    for m in range(n_cand):
        gm = g[m:m + 1, :]
        beats = (gm > g) | ((gm == g) & (m < blk))
        cnt = cnt + jnp.where(beats & (m < n_valid), 1, 0)
    return cnt, blk


def _block_pairs(n_blocks):
    return [(q, 0) for q in range(n_blocks)] + [(q, t) for q in range(n_blocks) for t in range(1, q + 1)]


def _moba_prompt_kernel(slopes_ref, qt_ref, k_ref, vt_ref, kmean_ref, o_ref,
                        kaug_ref, vaug_ref, qaug_s, mask_s, s_s, p_s, pv_s, top_s, m_all, acc_all,
                        pair_q, pair_t):
    b, hp = pl.program_id(0), pl.program_id(1)
    tq = ATT_BLOCK
    seq = k_ref.shape[1]
    n_blocks = seq // MOBA_BLOCK
    pairs = _block_pairs(n_blocks)
    n_pairs = len(pairs)

    @pl.when((b == 0) & (hp == 0))
    def _():
        for i, (q, t) in enumerate(pairs):
            pair_q[i] = q
            pair_t[i] = t
        key = lax.broadcasted_iota(jnp.int32, (tq, 2 * tq), 0)
        qry = lax.broadcasted_iota(jnp.int32, (tq, 2 * tq), 1) & (tq - 1)
        mask_s[...] = jnp.where(key <= qry, 0.0, NEG_BIG)

    kaug_ref[:, 0:HEAD_PAIR_WIDTH] = k_ref[0]
    lane = lax.broadcasted_iota(jnp.int32, (MOBA_BLOCK, LANES), 1)
    key = lax.broadcasted_iota(jnp.int32, (MOBA_BLOCK, LANES), 0).astype(F32)
    base = jnp.where(lane < AUG_ONE + 3, 1.0,
                     jnp.where(lane < AUG_KEYPOS + 3, key, 0.0))
    for n in range(n_blocks):
        kaug_ref[n * MOBA_BLOCK:(n + 1) * MOBA_BLOCK, HEAD_PAIR_WIDTH:] = (
            jnp.where(lane == AUG_BLOCK + n, 1.0, base).astype(BF16))
    for h in range(2):
        vaug_ref[h, 0:HEAD_DIM, :] = vt_ref[0, h * HEAD_DIM:(h + 1) * HEAD_DIM, :].astype(BF16)
        vaug_ref[h, HEAD_DIM:, :] = jnp.ones((ONES_ROWS, seq), BF16)

    lane = lax.broadcasted_iota(jnp.int32, (1, 2 * tq), 1)
    slope = jnp.where(lane < tq, slopes_ref[2 * hp], slopes_ref[2 * hp + 1]) * LOG2E
    q_off = (lane & (tq - 1)).astype(F32)
    row8 = lax.broadcasted_iota(jnp.int32, (8, 2 * tq), 0)
    small = jnp.zeros((8, 2 * tq), F32)
    for first, terms in ((AUG_ONE, _split3(-slope * q_off)), (AUG_KEYPOS, _split3(slope))):
        for r, v in enumerate(terms):
            small = jnp.where(row8 == first + r, v.astype(F32), small)
    kmean = kmean_ref[0]

    def prepare(qb, carry):
        qt = qt_ref[0, :, _blk(qb)]
        q_cols = _pair_columns(qt, tq)
        gate = _dot_3pass(kmean, q_cols)
        cnt, blk = _rank_before_rows(gate, qb, n_blocks)
        chosen = ((blk < qb) & (cnt < MOBA_TOPK)) | (blk == qb)
        choice_bias = jnp.where(chosen, 0.0, NEG_BIG)
        qaug_s[qb] = jnp.concatenate(
            [q_cols * (QK_SCALE * LOG2E), small, choice_bias,
             jnp.zeros((LANES - 8 - n_blocks, 2 * tq), F32)], axis=0).astype(BF16)
        m_all[qb] = jnp.full(m_all.shape[1:], NEG_BIG, F32)
        acc_all[qb] = jnp.zeros(acc_all.shape[1:], F32)
        return carry

    lax.fori_loop(0, n_blocks, prepare, 0, unroll=4)

    def score(i, slot):
        q, t = pair_q[i], pair_t[i]
        s_s[slot & 1] = _dot(kaug_ref[_blk(q - t), :], qaug_s[q])

    def probs(i, slot):
        t = pair_t[i]
        s = s_s[slot & 1]
        if isinstance(i, int) and i < n_blocks:
            s = s + mask_s[...]
        top = jnp.max(s, axis=0, keepdims=True)
        p_s[slot & 1] = jnp.exp2(s - top).astype(BF16)
        top_s[slot & 3, 0:1, :] = top - slope * (t * MOBA_BLOCK).astype(F32)

    def value(i, slot):
        q, t = pair_q[i], pair_t[i]
        for h in range(2):
            cols = slice(h * tq, (h + 1) * tq)
            pv_s[slot & 1, :, cols] = _dot(vaug_ref[h, :, _blk(q - t)], p_s[slot & 1, :, cols])

    def merge(i, slot):
        q = pair_q[i]
        top = top_s[slot & 3, 0:1, :]
        m_run = m_all[q, 0:1, :]
        m_new = jnp.maximum(m_run, top)
        acc_all[q] = acc_all[q] * jnp.exp2(m_run - m_new) + pv_s[slot & 1] * jnp.exp2(top - m_new)
        m_all[q, 0:1, :] = m_new

    def tick(i, slot, first=0, last=3):
        stages = (score, probs, value, merge)
        for k in range(first, last + 1):
            stages[k](i - k, slot - k)

    depth = 3
    unroll = 12
    n_static = depth + unroll * -(-(n_blocks + 1 - depth) // unroll)
    for i in range(n_static):
        tick(i, i, last=min(i, depth))
    n_groups = (n_pairs - n_static) // unroll

    def steady(g, carry):
        base = n_static + g * unroll
        for u in range(unroll):
            tick(base + u, n_static + u)
        return carry

    lax.fori_loop(0, n_groups, steady, 0)
    for i in range(n_static + n_groups * unroll, n_pairs):
        tick(i, i)
    for k in range(1, depth + 1):
        tick(n_pairs - 1 + k, n_pairs - 1 + k, first=k)

    def finish(qb, carry):
        acc = acc_all[qb]
        out_t = acc[0:HEAD_DIM] / acc[HEAD_DIM:HEAD_DIM + 1]
        rows = jnp.concatenate([out_t[:, :tq], out_t[:, tq:]], axis=0).T
        o_ref[0, _blk(qb), :] = rows.astype(o_ref.dtype)
        return carry

    lax.fori_loop(0, n_blocks, finish, 0, unroll=2)


def _moba_prompt(slopes, qt, k_rows, vt, kmean):
    bsz, _, seq = qt.shape
    n_blocks = seq // MOBA_BLOCK
    assert AUG_BLOCK + n_blocks <= LANES
    tq = ATT_BLOCK
    n_pairs = len(_block_pairs(n_blocks))
    acc_rows = HEAD_DIM + ONES_ROWS
    feat_spec, rows_spec = _head_pair_specs(seq)
    return pl.pallas_call(
        _moba_prompt_kernel,
        grid=(bsz, GROUP_WIDTH // HEAD_PAIR_WIDTH),
        in_specs=[pl.BlockSpec(memory_space=pltpu.SMEM), feat_spec, rows_spec, feat_spec,
                  pl.BlockSpec((1, n_blocks, HEAD_PAIR_WIDTH), lambda b, h: (b, 0, h))],
        out_specs=rows_spec,
        out_shape=jax.ShapeDtypeStruct((bsz, seq, GROUP_WIDTH), BF16),
        scratch_shapes=[pltpu.VMEM((seq, HEAD_PAIR_WIDTH + LANES), BF16),
                        pltpu.VMEM((2, acc_rows, seq), BF16),
                        pltpu.VMEM((n_blocks, 2 * LANES, 2 * tq), BF16),
                        pltpu.VMEM((tq, 2 * tq), F32),
                        pltpu.VMEM((2, tq, 2 * tq), F32),
                        pltpu.VMEM((2, tq, 2 * tq), BF16),
                        pltpu.VMEM((2, acc_rows, 2 * tq), F32),
                        pltpu.VMEM((4, 8, 2 * tq), F32),
                        pltpu.VMEM((n_blocks, 8, 2 * tq), F32),
                        pltpu.VMEM((n_blocks, acc_rows, 2 * tq), F32),
                        pltpu.SMEM((n_pairs,), jnp.int32),
                        pltpu.SMEM((n_pairs,), jnp.int32)],
        compiler_params=pltpu.CompilerParams(dimension_semantics=("arbitrary",) * 2,
                                             vmem_limit_bytes=VMEM_LIMIT),
        name="moba_prompt",
    )(slopes, qt, k_rows, vt, kmean)


def _suffix_matrix(n):
    j = lax.broadcasted_iota(jnp.int32, (n, n), 0)
    s = lax.broadcasted_iota(jnp.int32, (n, n), 1)
    return jnp.where(j > s, 1.0, 0.0).astype(BF16)


def _suffix_sums(x, t2):
    hi, lo = _split(x)
    return _dot(jnp.concatenate([hi, lo], axis=1), t2)


def _rank_before_lanes(g, n_cand):
    lane = lax.broadcasted_iota(jnp.int32, g.shape, 1)
    cnt = jnp.zeros(g.shape, jnp.int32)
    for m in range(n_cand):
        gm = g[:, m:m + 1]
        beats = (gm > g) | ((gm == g) & (m < lane))
        cnt = cnt + jnp.where(beats, 1, 0)
    return cnt, lane


def _sample_kernel(pt_ref, slope_ref, qa_ref, qb_ref, kan_ref, van_ref, kbn_ref, vbn_ref,
                   cka_hbm, cva_hbm, *rest):
    pps = PAGES_PER_STEP
    kb_refs, vb_refs = (rest[i * pps:(i + 1) * pps] for i in range(2))
    (oa_ref, ob_ref, t2_ref, qa_s, qb_s, run_s, acca_s, gate_s, mx_s, l_s, accb_s,
     mo_s, lo_s, acco_s, new_s, live_s, ka_ring, va_ring, ring_sem, inflight_s) = rest[2 * pps:]
    b, j = pl.program_id(0), pl.program_id(1)
    n_batch = pl.num_programs(0)
    n_steps = pl.num_programs(1)
    blk = MOBA_BLOCK
    bps = pps * NEW_PAD // blk
    n_cached = accb_s.shape[0]
    n_first = (n_steps - 1 - j) * bps
    n_tok = qa_ref.shape[1]
    n_rows = n_tok * N_HEADS

    row = lax.broadcasted_iota(jnp.int32, (n_rows, GROUP_WIDTH), 0)
    lane_w = lax.broadcasted_iota(jnp.int32, (n_rows, GROUP_WIDTH), 1)
    head_mask = (lane_w // HEAD_DIM) == (row % N_HEADS)
    slope = slope_ref[:, 0:1]
    lane = lax.broadcasted_iota(jnp.int32, (n_rows, LANES), 1)

    @pl.when(j == 0)
    def _():
        t = _suffix_matrix(blk)
        t2_ref[0:blk, :] = t
        t2_ref[blk:, :] = t
        live_s[0] = 1

        def expand(q):
            rep = jnp.concatenate([jnp.broadcast_to(q[t:t + 1], (N_HEADS, GROUP_WIDTH))
                                   for t in range(n_tok)], axis=0)
            return jnp.where(head_mask, rep, 0.0)

        qa_s[...] = (expand(qa_ref[0]) * QK_SCALE).astype(BF16)
        qh, ql = _split(expand(qb_ref[0]))
        qb_s[...] = jnp.concatenate([qh, ql], axis=0)
        gate_s[...] = jnp.zeros_like(gate_s)
        mx_s[...] = jnp.zeros_like(mx_s)
        l_s[...] = jnp.zeros_like(l_s)

        def padded(ref):
            new_s[...] = jnp.zeros_like(new_s)
            new_s[0:n_tok, :] = ref[0]
            return new_s[...].astype(BF16)

        tok = (lax.broadcasted_iota(jnp.int32, (n_rows, NEW_PAD), 0) // N_HEADS)
        col = lax.broadcasted_iota(jnp.int32, (n_rows, NEW_PAD), 1)
        z = _dot_nt(qa_s[...], padded(kan_ref))
        strict = col < tok
        sp = _softplus(z)
        log_keep = jnp.where(strict, -sp, 0.0)
        hi, lo = _split(log_keep)
        tn = t2_ref[0:NEW_PAD, 0:NEW_PAD]
        between = _dot(hi, tn) + _dot(lo, tn)
        w = jnp.where(strict, jnp.exp((z - sp) + between), 0.0)
        run_s[...] = jnp.sum(log_keep, axis=1, keepdims=True)
        acca_s[...] = _dot(w.astype(BF16), padded(van_ref))
        s = _dot_nt(qb_s[0:n_rows, :], padded(kbn_ref)) * QK_SCALE
        s = s - slope * (tok - col).astype(F32)
        s = jnp.where(col <= tok, s, NEG_BIG)
        m_o = jnp.max(s, axis=1, keepdims=True)
        p = jnp.exp(s - m_o)
        mo_s[...] = m_o
        lo_s[...] = jnp.sum(p, axis=1, keepdims=True)
        acco_s[...] = _dot(p.astype(BF16), padded(vbn_ref))

    def pages(refs):
        return jnp.concatenate([r[0] for r in refs], axis=1)

    def ring_pages(ring, slot):
        return jnp.concatenate([ring[slot, p] for p in range(pps)], axis=1)

    def ring_copies(bb, jj, slot):
        first = pps * (n_steps - 1 - jj)
        copies = []
        for p in range(pps):
            pg = pt_ref[bb, first + p]
            copies.append(pltpu.make_async_copy(cka_hbm.at[pg], ka_ring.at[slot, p], ring_sem.at[0, slot]))
            copies.append(pltpu.make_async_copy(cva_hbm.at[pg], va_ring.at[slot, p], ring_sem.at[1, slot]))
        return copies

    slot = j & 1

    @pl.when((b == 0) & (j == 0))
    def _():
        for c in ring_copies(0, 0, 0):
            c.start()
        inflight_s[0] = 1

    last = j == n_steps - 1
    want_next = jnp.where(last, b + 1 < n_batch, live_s[0] == 1)

    @pl.when(want_next)
    def _():
        for c in ring_copies(jnp.where(last, b + 1, b), jnp.where(last, 0, j + 1), 1 - slot):
            c.start()

    inflight_s[1 - slot] = want_next.astype(jnp.int32)

    @pl.when(inflight_s[slot] == 1)
    def _():
        for c in ring_copies(b, j, slot):
            c.wait()

    @pl.when(live_s[0] == 1)
    def _():
        z = _dot(qa_s[...], ring_pages(ka_ring, slot).astype(BF16))
        sp = _softplus(z)
        log_keep = -sp
        t2 = t2_ref[...]
        run = run_s[...]
        betweens = [None] * bps
        for i in reversed(range(bps)):
            lk = log_keep[:, i * blk:(i + 1) * blk]
            betweens[i] = _suffix_sums(lk, t2) + run
            run = run + jnp.sum(lk, axis=1, keepdims=True)
        run_s[...] = run
        live_s[0] = (jnp.max(run) > RUN_DEAD_LN).astype(jnp.int32)
        w = jnp.exp((z - sp) + jnp.concatenate(betweens, axis=1))
        acca_s[...] = acca_s[...] + _dot_nt(w.astype(BF16), ring_pages(va_ring, slot).astype(BF16))

    kb_h, kb_l = _split(pages(kb_refs))
    vb = pages(vb_refs).astype(BF16)
    raw = _dot(qb_s[...], kb_h)
    raw_hh = raw[0:n_rows]
    raw3 = raw_hh + (raw[n_rows:] + _dot(qb_s[0:n_rows, :], kb_l))
    tok = lax.broadcasted_iota(jnp.int32, (n_rows, blk), 0) // N_HEADS
    col = lax.broadcasted_iota(jnp.int32, (n_rows, blk), 1)
    for i in range(bps):
        n = n_first + i
        cols = slice(i * blk, (i + 1) * blk)
        gate_n = jnp.sum(raw3[:, cols], axis=1, keepdims=True) * (1.0 / blk)
        dist = ((n_cached - n) * blk + tok - col).astype(F32)
        s = raw_hh[:, cols] * QK_SCALE - slope * dist
        m_n = jnp.max(s, axis=1, keepdims=True)
        p = jnp.exp(s - m_n)
        l_n = jnp.sum(p, axis=1, keepdims=True)
        here = lane == n
        gate_s[...] = jnp.where(here, gate_n, gate_s[...])
        mx_s[...] = jnp.where(here, m_n, mx_s[...])
        l_s[...] = jnp.where(here, l_n, l_s[...])
        accb_s[n] = _dot_nt(p.astype(BF16), vb[:, cols])

    @pl.when(j == n_steps - 1)
    def _():
        def fold(acc):
            kept = jnp.where(head_mask, acc, 0.0)
            return jnp.sum(kept.reshape(n_tok, N_HEADS, GROUP_WIDTH), axis=1)

        oa_ref[0] = fold(acca_s[...])
        cnt, ln = _rank_before_lanes(gate_s[...], n_cached)
        sel = (ln < n_cached) & (cnt < MOBA_TOPK)
        mx = jnp.where(sel, mx_s[...], NEG_BIG)
        m_o = mo_s[...]
        m_all = jnp.maximum(m_o, jnp.max(mx, axis=1, keepdims=True))
        c = jnp.where(sel, jnp.exp(mx - m_all), 0.0)
        c_o = jnp.exp(m_o - m_all)
        l_tot = c_o * lo_s[...] + jnp.sum(c * l_s[...], axis=1, keepdims=True)
        acc = c_o * acco_s[...]
        for nb in range(n_cached):
            acc = acc + c[:, nb:nb + 1] * accb_s[nb]
        ob_ref[0] = fold(acc / l_tot)


def _sample_attention(page_table, slope_rows, qa, qb, ka_new, va_new, kb_new, vb_new,
                      ck_sb, cv_sb, ck_mb, cv_mb):
    dbs, n_tok, _ = qa.shape
    n_pages = page_table.shape[1]
    page = ck_sb.shape[2]
    pps = PAGES_PER_STEP
    assert page == NEW_PAD and n_pages % pps == 0 and (pps * page) % MOBA_BLOCK == 0
    n_steps = n_pages // pps
    assert n_steps % 2 == 0, "the two-slot page ring alternates slots across consecutive grid steps"
    n_cached = n_pages * page // MOBA_BLOCK
    assert n_cached <= LANES
    n_rows = n_tok * N_HEADS

    tok_spec = pl.BlockSpec((1, n_tok, GROUP_WIDTH), lambda b, j, pt: (b, 0, 0))

    def page_spec(which):
        return pl.BlockSpec((1, GROUP_WIDTH, page),
                            lambda b, j, pt: (pt[b, pps * (n_steps - 1 - j) + which], 0, 0))

    grid_spec = pltpu.PrefetchScalarGridSpec(
        num_scalar_prefetch=1,
        grid=(dbs, n_steps),
        in_specs=[pl.BlockSpec((n_rows, LANES), lambda b, j, pt: (0, 0))] + [tok_spec] * 6
                 + [pl.BlockSpec(memory_space=pl.ANY)] * 2 + [page_spec(p) for p in range(pps)] * 2,
        out_specs=[tok_spec, tok_spec],
        scratch_shapes=[
            pltpu.VMEM((2 * MOBA_BLOCK, MOBA_BLOCK), BF16),
            pltpu.VMEM((n_rows, GROUP_WIDTH), BF16),
            pltpu.VMEM((2 * n_rows, GROUP_WIDTH), BF16),
            pltpu.VMEM((n_rows, 1), F32),
            pltpu.VMEM((n_rows, GROUP_WIDTH), F32),
            pltpu.VMEM((n_rows, LANES), F32),
            pltpu.VMEM((n_rows, LANES), F32),
            pltpu.VMEM((n_rows, LANES), F32),
            pltpu.VMEM((n_cached, n_rows, GROUP_WIDTH), F32),
            pltpu.VMEM((n_rows, 1), F32),
            pltpu.VMEM((n_rows, 1), F32),
            pltpu.VMEM((n_rows, GROUP_WIDTH), F32),
            pltpu.VMEM((NEW_PAD, GROUP_WIDTH), F32),
            pltpu.SMEM((1,), jnp.int32),
            pltpu.VMEM((2, pps, GROUP_WIDTH, page), F32),
            pltpu.VMEM((2, pps, GROUP_WIDTH, page), F32),
            pltpu.SemaphoreType.DMA((2, 2)),
            pltpu.SMEM((2,), jnp.int32),
        ],
    )
    out = jax.ShapeDtypeStruct((dbs, n_tok, GROUP_WIDTH), F32)
    caches = [ck_sb, cv_sb] + [c for c in (ck_mb, cv_mb) for _ in range(pps)]
    return pl.pallas_call(
        _sample_kernel,
        grid_spec=grid_spec,
        out_shape=[out, out],
        compiler_params=pltpu.CompilerParams(dimension_semantics=("arbitrary", "arbitrary"),
                                             vmem_limit_bytes=VMEM_LIMIT),
        name="sample_attn",
    )(page_table, slope_rows, qa, qb, ka_new, va_new, kb_new, vb_new, *caches)


def _mix_out_kernel(alpha, x_ref, oa_ref, ga_ref, ob_ref, gb_ref, w_ref, gain_ref, bias_ref, y_ref):
    def gated(o_ref, g_ref):
        return o_ref[...].astype(F32) * jax.nn.silu(g_ref[...].astype(F32))

    h = jnp.concatenate([gated(oa_ref, ga_ref), gated(ob_ref, gb_ref)], axis=-1)
    out = _dot(h.astype(BF16), w_ref[...])
    y = alpha * x_ref[...] + out
    mu = jnp.mean(y, axis=-1, keepdims=True)
    var = jnp.mean(jnp.square(y - mu), axis=-1, keepdims=True)
    y_ref[...] = (y - mu) * lax.rsqrt(var + LN_EPS) * gain_ref[...] + bias_ref[...]


def _mix_out(x2d, oa, ga, ob, gb, w_out, gain, bias, alpha, block_rows):
    rows, d_model = x2d.shape
    g_spec = pl.BlockSpec((block_rows, GROUP_WIDTH), lambda i: (i, 0))
    x_spec = pl.BlockSpec((block_rows, d_model), lambda i: (i, 0))
    return pl.pallas_call(
        functools.partial(_mix_out_kernel, alpha),
        grid=(rows // block_rows,),
        in_specs=[x_spec, g_spec, g_spec, g_spec, g_spec,
                  _full_spec(w_out), _full_spec(gain), _full_spec(bias)],
        out_specs=x_spec,
        out_shape=jax.ShapeDtypeStruct(x2d.shape, F32),
        compiler_params=pltpu.CompilerParams(dimension_semantics=("arbitrary",),
                                             vmem_limit_bytes=VMEM_LIMIT),
        name="mix_out",
    )(x2d, oa, ga, ob, gb, w_out, gain, bias)


def kernel(x_prompt, x_sample, cache_k_sb, cache_v_sb, cache_k_moba, cache_v_moba, page_table,
           w_in, w_out, ln_gain, ln_bias):
    depth = w_in.shape[0]
    assert depth == 1, "single-layer trunk"
    bsz, seq, d_model = x_prompt.shape
    dbs, n_tok, _ = x_sample.shape
    alpha = (2.0 * depth) ** 0.25
    slopes = jnp.asarray([2.0 ** (-8.0 * (i + 1) / N_HEADS) for i in range(N_HEADS)], dtype=F32)
    slope_rows = jnp.broadcast_to(jnp.tile(slopes, n_tok)[:, None], (n_tok * N_HEADS, LANES))

    g = GROUP_WIDTH
    w = w_in[0]
    w_hi = w.astype(BF16)
    w_lo = (w[:, 4 * g:6 * g] - w_hi[:, 4 * g:6 * g].astype(F32)).astype(BF16)

    def col(a, c):
        return a[:, c * g:(c + 1) * g]

    w_row = jnp.concatenate([col(w_hi, c) for c in ROW_GROUPS], axis=1)
    w_row_lo = col(w_lo, 1)
    w_feat = jnp.concatenate([col(w_hi, c) for c in FEAT_GROUPS], axis=1).T
    w_feat_lo = col(w_lo, 0).T
    w_o = w_out[0].astype(BF16)
    gain = ln_gain[0][None, :]
    bias = ln_bias[0][None, :]

    def pages(c):
        return jnp.transpose(c[0], (0, 2, 3, 1)).reshape(c.shape[1], GROUP_WIDTH, c.shape[2])

    def heads_from_features(a):
        return jnp.transpose(a.reshape(bsz, N_HEADS, HEAD_DIM, seq), (0, 3, 1, 2))[None]

    def heads_from_rows(a):
        return a.reshape(1, dbs, n_tok, N_HEADS, HEAD_DIM)

    xp = x_prompt.reshape(bsz * seq, d_model)
    (ga, gb, ka_rows, kb_rows, kmean, qat, qbt, kat, vat, kbt, vbt) = _project_prompt(
        xp, w_row, w_row_lo, w_feat, w_feat_lo, bsz, 512)
    shp = (bsz, seq, GROUP_WIDTH)
    o_a = _sb_prompt(qat, ka_rows.reshape(shp), vat)
    o_b = _moba_prompt(slopes, qbt, kb_rows.reshape(shp), vbt, kmean.reshape(bsz, seq // MOBA_BLOCK, GROUP_WIDTH))
    y_p = _mix_out(xp, o_a.reshape(xp.shape[0], GROUP_WIDTH), ga, o_b.reshape(xp.shape[0], GROUP_WIDTH), gb,
                   w_o, gain, bias, alpha, 512)

    xs = x_sample.reshape(dbs * n_tok, d_model)
    sqa, ska, sva, sga, sqb, skb, svb, sgb = _project_rows(xs, w_hi, w_lo)
    sshp = (dbs, n_tok, GROUP_WIDTH)
    so_a, so_b = _sample_attention(
        page_table, slope_rows, sqa.reshape(sshp), sqb.reshape(sshp),
        ska.reshape(sshp), sva.reshape(sshp), skb.reshape(sshp), svb.reshape(sshp),
        pages(cache_k_sb), pages(cache_v_sb), pages(cache_k_moba), pages(cache_v_moba))
    y_s = _mix_out(xs, so_a.reshape(xs.shape[0], GROUP_WIDTH), sga, so_b.reshape(xs.shape[0], GROUP_WIDTH), sgb,
                   w_o, gain, bias, alpha, dbs * n_tok)

    return (y_p.reshape(x_prompt.shape), y_s.reshape(x_sample.shape),
            heads_from_features(kat), heads_from_features(vat),
            heads_from_features(kbt), heads_from_features(vbt),
            heads_from_rows(ska), heads_from_rows(sva), heads_from_rows(skb), heads_from_rows(svb))
```

```python
import functools
import math

import jax
import jax.numpy as jnp
from jax import lax
from jax.experimental import pallas as pl
from jax.experimental.pallas import tpu as pltpu

HEAD_DIM = 64
N_HEADS = 8
GROUP_WIDTH = N_HEADS * HEAD_DIM
HEAD_PAIR_WIDTH = 2 * HEAD_DIM
LANES = 128
MOBA_BLOCK = 256
MOBA_TOPK = 3
ATT_BLOCK = 256
QK_SCALE = HEAD_DIM ** -0.5
LOG2E = math.log2(math.e)
LN_EPS = 1e-5
NEG_BIG = -1e30
SOFTPLUS2_CLAMP = 100.0
RUN_DEAD = -160.0
RUN_DEAD_LN = -112.0
NEW_PAD = 128
PAGES_PER_STEP = 8
SB_GROUP = 2
ONES_ROWS = 16
VMEM_LIMIT = 48 * 1024 * 1024

AUG_ONE = 0
AUG_KEYPOS = 3
AUG_BLOCK = 8

F32 = jnp.float32
BF16 = jnp.bfloat16


def _dot(a, b):
    return jnp.dot(a, b, preferred_element_type=F32)


def _dot_nt(a, b):
    return lax.dot_general(a, b, (((1,), (1,)), ((), ())), preferred_element_type=F32)


def _split(x):
    hi = x.astype(BF16)
    lo = (x - hi.astype(F32)).astype(BF16)
    return hi, lo


def _split3(x):
    hi = x.astype(BF16)
    r = x - hi.astype(F32)
    mid = r.astype(BF16)
    lo = (r - mid.astype(F32)).astype(BF16)
    return hi, mid, lo


def _dot_3pass(a, b):
    ah, al = _split(a)
    bh, bl = _split(b)
    return _dot(ah, bh) + (_dot(ah, bl) + _dot(al, bh))


def _softplus(z):
    return jnp.maximum(z, 0.0) + jnp.log(1.0 + jnp.exp(-jnp.abs(z)))


def _softplus2(z):
    return jnp.maximum(jnp.log2(1.0 + jnp.exp2(jnp.minimum(z, SOFTPLUS2_CLAMP))), z)


def _full_spec(a, single_buffer=False):
    mode = pl.Buffered(1) if single_buffer else None
    return pl.BlockSpec(a.shape, lambda *_: (0,) * a.ndim, pipeline_mode=mode)


def _proj_rows_kernel(x_ref, w_ref, wlo_ref, *out_refs):
    xh, xl = _split(x_ref[...])
    g = GROUP_WIDTH
    for c, o_ref in enumerate(out_refs):
        wc = w_ref[:, c * g:(c + 1) * g]
        out = _dot(xh, wc)
        if c in (4, 5):
            out = out + (_dot(xh, wlo_ref[:, (c - 4) * g:(c - 3) * g]) + _dot(xl, wc))
        o_ref[...] = out


def _project_rows(x2d, w_hi, w_lo):
    rows, _ = x2d.shape
    out = jax.ShapeDtypeStruct((rows, GROUP_WIDTH), F32)
    return pl.pallas_call(
        _proj_rows_kernel,
        grid=(1,),
        in_specs=[_full_spec(x2d), _full_spec(w_hi), _full_spec(w_lo)],
        out_specs=[pl.BlockSpec((rows, GROUP_WIDTH), lambda i: (0, 0))] * 8,
        out_shape=[out] * 8,
        compiler_params=pltpu.CompilerParams(dimension_semantics=("arbitrary",),
                                             vmem_limit_bytes=VMEM_LIMIT),
        name="proj_rows",
    )(x2d, w_hi, w_lo)


ROW_GROUPS = (3, 7, 1, 5)
FEAT_GROUPS = (0, 4, 1, 2, 5, 6)
FEAT_3PASS = (1,)


def _proj_prompt_kernel(x_ref, wr_ref, wrlo_ref, wf_ref, wflo_ref,
                        ga_ref, gb_ref, kar_ref, kbr_ref, kmean_ref,
                        qat_ref, qbt_ref, kat_ref, vat_ref, kbt_ref, vbt_ref):
    x = x_ref[...]
    xh, xl = _split(x)
    g = GROUP_WIDTH
    for c, o_ref in enumerate((ga_ref, gb_ref)):
        o_ref[...] = _dot(xh, wr_ref[:, c * g:(c + 1) * g]).astype(o_ref.dtype)
    lo_slot = 0
    token_major_copy = {2: kar_ref, 4: kbr_ref}
    for c, o_ref in enumerate((qat_ref, qbt_ref, kat_ref, vat_ref, kbt_ref, vbt_ref)):
        wc = wf_ref[c * g:(c + 1) * g, :]
        out = _dot_nt(wc, xh)
        if c in FEAT_3PASS:
            out = out + (_dot_nt(wflo_ref[lo_slot * g:(lo_slot + 1) * g, :], xh) + _dot_nt(wc, xl))
            lo_slot += 1
        o_ref[0] = out
        if c in token_major_copy:
            rows_ref = token_major_copy[c]
            rows_ref[...] = out.T.astype(rows_ref.dtype)
    n_blk = x.shape[0] // MOBA_BLOCK
    row8 = lax.broadcasted_iota(jnp.int32, (8, x.shape[1]), 0)
    xbar = jnp.zeros((8, x.shape[1]), F32)
    for i in range(n_blk):
        mean_i = jnp.sum(x[i * MOBA_BLOCK:(i + 1) * MOBA_BLOCK], axis=0, keepdims=True) * (1.0 / MOBA_BLOCK)
        xbar = jnp.where(row8 == i, mean_i, xbar)
    bh, bl = _split(xbar)
    wk = wr_ref[:, 3 * g:4 * g]
    km = _dot(bh, wk) + (_dot(bh, wrlo_ref[...]) + _dot(bl, wk))
    kmean_ref[0] = km[0:n_blk]


def _project_prompt(x2d, w_row, w_row_lo, w_feat, w_feat_lo, batch, block_rows):
    rows, d_model = x2d.shape
    seq = rows // batch
    per_seq = seq // block_rows
    n_blk = block_rows // MOBA_BLOCK
    row_spec = pl.BlockSpec((block_rows, GROUP_WIDTH), lambda i: (i, 0))
    feat_spec = pl.BlockSpec((1, GROUP_WIDTH, block_rows), lambda i: (i // per_seq, 0, i % per_seq))
    row_bf16 = jax.ShapeDtypeStruct((rows, GROUP_WIDTH), BF16)
    feat = jax.ShapeDtypeStruct((batch, GROUP_WIDTH, seq), F32)
    kmean = jax.ShapeDtypeStruct((rows // block_rows, n_blk, GROUP_WIDTH), F32)
    return pl.pallas_call(
        _proj_prompt_kernel,
        grid=(rows // block_rows,),
        in_specs=[pl.BlockSpec((block_rows, d_model), lambda i: (i, 0)),
                  _full_spec(w_row, True), _full_spec(w_row_lo, True),
                  _full_spec(w_feat, True), _full_spec(w_feat_lo, True)],
        out_specs=[row_spec] * 4 + [pl.BlockSpec((1, n_blk, GROUP_WIDTH), lambda i: (i, 0, 0))] + [feat_spec] * 6,
        out_shape=[row_bf16] * 4 + [kmean] + [feat] * 6,
        compiler_params=pltpu.CompilerParams(dimension_semantics=("arbitrary",),
                                             vmem_limit_bytes=VMEM_LIMIT),
        name="proj_prompt",
    )(x2d, w_row, w_row_lo, w_feat, w_feat_lo)


def _pair_columns(qt, tq):
    row = lax.broadcasted_iota(jnp.int32, qt.shape, 0)
    zero = jnp.zeros_like(qt)
    return jnp.concatenate([jnp.where(row < HEAD_DIM, qt, zero),
                            jnp.where(row >= HEAD_DIM, qt, zero)], axis=1)


def _pair_merge_rows(acc_t, tq):
    row = lax.broadcasted_iota(jnp.int32, (HEAD_PAIR_WIDTH, tq), 0)
    return jnp.where(row < HEAD_DIM, acc_t[:, :tq], acc_t[:, tq:]).T


def _blk(kj):
    return pl.ds(pl.multiple_of(kj * ATT_BLOCK, ATT_BLOCK), ATT_BLOCK)


def _head_pair_specs(seq):
    feat_spec = pl.BlockSpec((1, HEAD_PAIR_WIDTH, seq), lambda b, h: (b, h, 0))
    rows_spec = pl.BlockSpec((1, seq, HEAD_PAIR_WIDTH), lambda b, h: (b, 0, h))
    return feat_spec, rows_spec


def _sb_prompt_kernel(qt_ref, k_ref, vt_ref, o_ref, u_ref, vtb_ref, run_s, acc_s):
    b, hp = pl.program_id(0), pl.program_id(1)
    tq = ATT_BLOCK
    half = tq // 2
    u_rows = half + ONES_ROWS
    n_blocks = k_ref.shape[1] // tq

    @pl.when((b == 0) & (hp == 0))
    def _():
        s_i = lax.broadcasted_iota(jnp.int32, (u_rows, tq), 0)
        j_i = lax.broadcasted_iota(jnp.int32, (u_rows, tq), 1) & (half - 1)
        u_ref[...] = jnp.where((j_i >= s_i) | (s_i >= half), -1.0, 0.0).astype(BF16)

    vtb_ref[...] = vt_ref[0].astype(BF16)
    refs = (qt_ref, k_ref, o_ref, u_ref, vtb_ref, run_s, acc_s)

    def group(q_first, count):
        blocks = [_SbQueryBlock(refs, q_first + n, n) for n in range(count)]
        for blk in blocks:
            blk.reset()
        work = [(blk, t) for blk in blocks for t in (0, 1)]
        scored = [blk.scores(t, t == 0) for blk, t in work]
        locs = [blk.suffix(halves) for (blk, _), (_, halves) in zip(work, scored)]
        for (blk, t), (z, _), loc in zip(work, scored, locs):
            blk.weigh(t, z, loc)
        for blk in blocks:
            blk.earlier_blocks()
            blk.store()

    first = _SbQueryBlock(refs, 0, 0)
    first.reset()
    first.single(0, True)
    first.store()
    group(1, SB_GROUP - 1)

    def grouped(i, carry):
        group(i * SB_GROUP, SB_GROUP)
        return carry

    lax.fori_loop(1, n_blocks // SB_GROUP, grouped, 0)


class _SbQueryBlock:
    def __init__(self, refs, qi, slot):
        (self.qt_ref, self.k_ref, self.o_ref, self.u_ref, self.vtb_ref, self.run_s, self.acc_s) = refs
        self.qi, self.slot = qi, slot
        q = self.qt_ref[0, :, _blk(qi)] * (QK_SCALE * LOG2E)
        self.q_cols = _pair_columns(q.astype(BF16), ATT_BLOCK)

    def reset(self):
        self.run_s[self.slot] = jnp.zeros(self.run_s.shape[1:], F32)
        self.acc_s[self.slot] = jnp.zeros(self.acc_s.shape[1:], F32)

    def scores(self, t, diagonal):
        tq, half = ATT_BLOCK, ATT_BLOCK // 2
        z = _dot(self.k_ref[0, _blk(self.qi - t), :], self.q_cols)
        sp = _softplus2(z)
        if diagonal:
            key = lax.broadcasted_iota(jnp.int32, z.shape, 0)
            qry = lax.broadcasted_iota(jnp.int32, z.shape, 1) & (tq - 1)
            strict = key < qry
            sp = jnp.where(strict, sp, 0.0)
            z = jnp.where(strict, z, NEG_BIG)
        hi, lo = _split(sp)
        halves = [jnp.concatenate([hi[h * half:(h + 1) * half], lo[h * half:(h + 1) * half]], axis=0)
                  for h in range(2)]
        return z, halves

    def suffix(self, halves):
        neg_u = self.u_ref[...]
        return [_dot(neg_u, hl) for hl in halves]

    def weigh(self, t, z, locs):
        tq, half = ATT_BLOCK, ATT_BLOCK // 2
        run = self.run_s[self.slot]
        tot_first = locs[0][half:half + 1]
        tot_second = locs[1][half:half + 1]
        first = (z[0:half] + locs[0][0:half]) + (run + tot_second)
        second = (z[half:tq] + locs[1][0:half]) + run
        w = jnp.exp2(jnp.concatenate([first, second], axis=0)).astype(BF16)
        self.run_s[self.slot] = run + (tot_first + tot_second)
        self.acc_s[self.slot] = self.acc_s[self.slot] + _dot(self.vtb_ref[:, _blk(self.qi - t)], w)

    def single(self, t, diagonal):
        z, halves = self.scores(t, diagonal)
        self.weigh(t, z, self.suffix(halves))

    def pair(self, t, diagonal):
        z0, h0 = self.scores(t, diagonal)
        z1, h1 = self.scores(t + 1, False)
        l0 = self.suffix(h0)
        l1 = self.suffix(h1)
        self.weigh(t, z0, l0)
        self.weigh(t + 1, z1, l1)

    def alive(self):
        return jnp.max(self.run_s[self.slot]) > RUN_DEAD

    def earlier_blocks(self):
        qi = self.qi

        def more_pairs(c):
            return (c[0] + 1 <= qi) & c[1]

        def next_pair(c):
            self.pair(c[0], False)
            return c[0] + 2, self.alive()

        t, live = lax.while_loop(more_pairs, next_pair, (jnp.int32(2), self.alive()))

        @pl.when((t == qi) & live)
        def _():
            self.single(t, False)

    def store(self):
        out = _pair_merge_rows(self.acc_s[self.slot], ATT_BLOCK)
        self.o_ref[0, _blk(self.qi), :] = out.astype(self.o_ref.dtype)


def _sb_prompt(qt, k_rows, vt):
    bsz, _, seq = qt.shape
    tq = ATT_BLOCK
    assert seq % (SB_GROUP * tq) == 0, "query blocks are processed SB_GROUP at a time"
    feat_spec, rows_spec = _head_pair_specs(seq)
    return pl.pallas_call(
        _sb_prompt_kernel,
        grid=(bsz, GROUP_WIDTH // HEAD_PAIR_WIDTH),
        in_specs=[feat_spec, rows_spec, feat_spec],
        out_specs=rows_spec,
        out_shape=jax.ShapeDtypeStruct((bsz, seq, GROUP_WIDTH), BF16),
        scratch_shapes=[pltpu.VMEM((tq // 2 + ONES_ROWS, tq), BF16),
                        pltpu.VMEM((HEAD_PAIR_WIDTH, seq), BF16),
                        pltpu.VMEM((SB_GROUP, 1, 2 * tq), F32),
                        pltpu.VMEM((SB_GROUP, HEAD_PAIR_WIDTH, 2 * tq), F32)],
        compiler_params=pltpu.CompilerParams(dimension_semantics=("arbitrary",) * 2,
                                             vmem_limit_bytes=VMEM_LIMIT),
        name="sb_prompt",
    )(qt, k_rows, vt)


def _rank_before_rows(g, n_valid, n_cand):
    blk = lax.broadcasted_iota(jnp.int32, g.shape, 0)
    cnt = jnp.zeros(g.shape, jnp.int32)
    for m in range(n_cand):
        gm = g[m:m + 1, :]
        beats = (gm > g) | ((gm == g) & (m < blk))
        cnt = cnt + jnp.where(beats & (m < n_valid), 1, 0)
    return cnt, blk


def _block_pairs(n_blocks):
    return [(q, 0) for q in range(n_blocks)] + [(q, t) for q in range(n_blocks) for t in range(1, q + 1)]


def _moba_prompt_kernel(slopes_ref, qt_ref, k_ref, vt_ref, kmean_ref, o_ref,
                        kaug_ref, vaug_ref, qaug_s, mask_s, s_s, p_s, pv_s, top_s, m_all, acc_all,
                        pair_q, pair_t):
    b, hp = pl.program_id(0), pl.program_id(1)
    tq = ATT_BLOCK
    seq = k_ref.shape[1]
    n_blocks = seq // MOBA_BLOCK
    pairs = _block_pairs(n_blocks)
    n_pairs = len(pairs)

    @pl.when((b == 0) & (hp == 0))
    def _():
        for i, (q, t) in enumerate(pairs):
            pair_q[i] = q
            pair_t[i] = t
        key = lax.broadcasted_iota(jnp.int32, (tq, 2 * tq), 0)
        qry = lax.broadcasted_iota(jnp.int32, (tq, 2 * tq), 1) & (tq - 1)
        mask_s[...] = jnp.where(key <= qry, 0.0, NEG_BIG)

    kaug_ref[:, 0:HEAD_PAIR_WIDTH] = k_ref[0]
    lane = lax.broadcasted_iota(jnp.int32, (MOBA_BLOCK, LANES), 1)
    key = lax.broadcasted_iota(jnp.int32, (MOBA_BLOCK, LANES), 0).astype(F32)
    base = jnp.where(lane < AUG_ONE + 3, 1.0,
                     jnp.where(lane < AUG_KEYPOS + 3, key, 0.0))
    for n in range(n_blocks):
        kaug_ref[n * MOBA_BLOCK:(n + 1) * MOBA_BLOCK, HEAD_PAIR_WIDTH:] = (
            jnp.where(lane == AUG_BLOCK + n, 1.0, base).astype(BF16))
    for h in range(2):
        vaug_ref[h, 0:HEAD_DIM, :] = vt_ref[0, h * HEAD_DIM:(h + 1) * HEAD_DIM, :].astype(BF16)
        vaug_ref[h, HEAD_DIM:, :] = jnp.ones((ONES_ROWS, seq), BF16)

    lane = lax.broadcasted_iota(jnp.int32, (1, 2 * tq), 1)
    slope = jnp.where(lane < tq, slopes_ref[2 * hp], slopes_ref[2 * hp + 1]) * LOG2E
    q_off = (lane & (tq - 1)).astype(F32)
    row8 = lax.broadcasted_iota(jnp.int32, (8, 2 * tq), 0)
    small = jnp.zeros((8, 2 * tq), F32)
    for first, terms in ((AUG_ONE, _split3(-slope * q_off)), (AUG_KEYPOS, _split3(slope))):
        for r, v in enumerate(terms):
            small = jnp.where(row8 == first + r, v.astype(F32), small)
    kmean = kmean_ref[0]

    def prepare(qb, carry):
        qt = qt_ref[0, :, _blk(qb)]
        q_cols = _pair_columns(qt, tq)
        gate = _dot_3pass(kmean, q_cols)
        cnt, blk = _rank_before_rows(gate, qb, n_blocks)
        chosen = ((blk < qb) & (cnt < MOBA_TOPK)) | (blk == qb)
        choice_bias = jnp.where(chosen, 0.0, NEG_BIG)
        qaug_s[qb] = jnp.concatenate(
            [q_cols * (QK_SCALE * LOG2E), small, choice_bias,
             jnp.zeros((LANES - 8 - n_blocks, 2 * tq), F32)], axis=0).astype(BF16)
        m_all[qb] = jnp.full(m_all.shape[1:], NEG_BIG, F32)
        acc_all[qb] = jnp.zeros(acc_all.shape[1:], F32)
        return carry

    lax.fori_loop(0, n_blocks, prepare, 0, unroll=4)

    def score(i, slot):
        q, t = pair_q[i], pair_t[i]
        s_s[slot & 1] = _dot(kaug_ref[_blk(q - t), :], qaug_s[q])

    def probs(i, slot):
        t = pair_t[i]
        s = s_s[slot & 1]
        if isinstance(i, int) and i < n_blocks:
            s = s + mask_s[...]
        top = jnp.max(s, axis=0, keepdims=True)
        p_s[slot & 1] = jnp.exp2(s - top).astype(BF16)
        top_s[slot & 3, 0:1, :] = top - slope * (t * MOBA_BLOCK).astype(F32)

    def value(i, slot):
        q, t = pair_q[i], pair_t[i]
        for h in range(2):
            cols = slice(h * tq, (h + 1) * tq)
            pv_s[slot & 1, :, cols] = _dot(vaug_ref[h, :, _blk(q - t)], p_s[slot & 1, :, cols])

    def merge(i, slot):
        q = pair_q[i]
        top = top_s[slot & 3, 0:1, :]
        m_run = m_all[q, 0:1, :]
        m_new = jnp.maximum(m_run, top)
        acc_all[q] = acc_all[q] * jnp.exp2(m_run - m_new) + pv_s[slot & 1] * jnp.exp2(top - m_new)
        m_all[q, 0:1, :] = m_new

    def tick(i, slot, first=0, last=3):
        stages = (score, probs, value, merge)
        for k in range(first, last + 1):
            stages[k](i - k, slot - k)

    depth = 3
    unroll = 12
    n_static = depth + unroll * -(-(n_blocks + 1 - depth) // unroll)
    for i in range(n_static):
        tick(i, i, last=min(i, depth))
    n_groups = (n_pairs - n_static) // unroll

    def steady(g, carry):
        base = n_static + g * unroll
        for u in range(unroll):
            tick(base + u, n_static + u)
        return carry

    lax.fori_loop(0, n_groups, steady, 0)
    for i in range(n_static + n_groups * unroll, n_pairs):
        tick(i, i)
    for k in range(1, depth + 1):
        tick(n_pairs - 1 + k, n_pairs - 1 + k, first=k)

    def finish(qb, carry):
        acc = acc_all[qb]
        out_t = acc[0:HEAD_DIM] / acc[HEAD_DIM:HEAD_DIM + 1]
        rows = jnp.concatenate([out_t[:, :tq], out_t[:, tq:]], axis=0).T
        o_ref[0, _blk(qb), :] = rows.astype(o_ref.dtype)
        return carry

    lax.fori_loop(0, n_blocks, finish, 0, unroll=2)


def _moba_prompt(slopes, qt, k_rows, vt, kmean):
    bsz, _, seq = qt.shape
    n_blocks = seq // MOBA_BLOCK
    assert AUG_BLOCK + n_blocks <= LANES
    tq = ATT_BLOCK
    n_pairs = len(_block_pairs(n_blocks))
    acc_rows = HEAD_DIM + ONES_ROWS
    feat_spec, rows_spec = _head_pair_specs(seq)
    return pl.pallas_call(
        _moba_prompt_kernel,
        grid=(bsz, GROUP_WIDTH // HEAD_PAIR_WIDTH),
        in_specs=[pl.BlockSpec(memory_space=pltpu.SMEM), feat_spec, rows_spec, feat_spec,
                  pl.BlockSpec((1, n_blocks, HEAD_PAIR_WIDTH), lambda b, h: (b, 0, h))],
        out_specs=rows_spec,
        out_shape=jax.ShapeDtypeStruct((bsz, seq, GROUP_WIDTH), BF16),
        scratch_shapes=[pltpu.VMEM((seq, HEAD_PAIR_WIDTH + LANES), BF16),
                        pltpu.VMEM((2, acc_rows, seq), BF16),
                        pltpu.VMEM((n_blocks, 2 * LANES, 2 * tq), BF16),
                        pltpu.VMEM((tq, 2 * tq), F32),
                        pltpu.VMEM((2, tq, 2 * tq), F32),
                        pltpu.VMEM((2, tq, 2 * tq), BF16),
                        pltpu.VMEM((2, acc_rows, 2 * tq), F32),
                        pltpu.VMEM((4, 8, 2 * tq), F32),
                        pltpu.VMEM((n_blocks, 8, 2 * tq), F32),
                        pltpu.VMEM((n_blocks, acc_rows, 2 * tq), F32),
                        pltpu.SMEM((n_pairs,), jnp.int32),
                        pltpu.SMEM((n_pairs,), jnp.int32)],
        compiler_params=pltpu.CompilerParams(dimension_semantics=("arbitrary",) * 2,
                                             vmem_limit_bytes=VMEM_LIMIT),
        name="moba_prompt",
    )(slopes, qt, k_rows, vt, kmean)


def _suffix_matrix(n):
    j = lax.broadcasted_iota(jnp.int32, (n, n), 0)
    s = lax.broadcasted_iota(jnp.int32, (n, n), 1)
    return jnp.where(j > s, 1.0, 0.0).astype(BF16)


def _suffix_sums(x, t2):
    hi, lo = _split(x)
    return _dot(jnp.concatenate([hi, lo], axis=1), t2)


def _rank_before_lanes(g, n_cand):
    lane = lax.broadcasted_iota(jnp.int32, g.shape, 1)
    cnt = jnp.zeros(g.shape, jnp.int32)
    for m in range(n_cand):
        gm = g[:, m:m + 1]
        beats = (gm > g) | ((gm == g) & (m < lane))
        cnt = cnt + jnp.where(beats, 1, 0)
    return cnt, lane


def _sample_kernel(pt_ref, slope_ref, qa_ref, qb_ref, kan_ref, van_ref, kbn_ref, vbn_ref,
                   cka_hbm, cva_hbm, *rest):
    pps = PAGES_PER_STEP
    kb_refs, vb_refs = (rest[i * pps:(i + 1) * pps] for i in range(2))
    (oa_ref, ob_ref, t2_ref, qa_s, qb_s, run_s, acca_s, gate_s, mx_s, l_s, accb_s,
     mo_s, lo_s, acco_s, new_s, live_s, ka_ring, va_ring, ring_sem, inflight_s) = rest[2 * pps:]
    b, j = pl.program_id(0), pl.program_id(1)
    n_batch = pl.num_programs(0)
    n_steps = pl.num_programs(1)
    blk = MOBA_BLOCK
    bps = pps * NEW_PAD // blk
    n_cached = accb_s.shape[0]
    n_first = (n_steps - 1 - j) * bps
    n_tok = qa_ref.shape[1]
    n_rows = n_tok * N_HEADS

    row = lax.broadcasted_iota(jnp.int32, (n_rows, GROUP_WIDTH), 0)
    lane_w = lax.broadcasted_iota(jnp.int32, (n_rows, GROUP_WIDTH), 1)
    head_mask = (lane_w // HEAD_DIM) == (row % N_HEADS)
    slope = slope_ref[:, 0:1]
    lane = lax.broadcasted_iota(jnp.int32, (n_rows, LANES), 1)

    @pl.when(j == 0)
    def _():
        t = _suffix_matrix(blk)
        t2_ref[0:blk, :] = t
        t2_ref[blk:, :] = t
        live_s[0] = 1

        def expand(q):
            rep = jnp.concatenate([jnp.broadcast_to(q[t:t + 1], (N_HEADS, GROUP_WIDTH))
                                   for t in range(n_tok)], axis=0)
            return jnp.where(head_mask, rep, 0.0)

        qa_s[...] = (expand(qa_ref[0]) * QK_SCALE).astype(BF16)
        qh, ql = _split(expand(qb_ref[0]))
        qb_s[...] = jnp.concatenate([qh, ql], axis=0)
        gate_s[...] = jnp.zeros_like(gate_s)
        mx_s[...] = jnp.zeros_like(mx_s)
        l_s[...] = jnp.zeros_like(l_s)

        def padded(ref):
            new_s[...] = jnp.zeros_like(new_s)
            new_s[0:n_tok, :] = ref[0]
            return new_s[...].astype(BF16)

        tok = (lax.broadcasted_iota(jnp.int32, (n_rows, NEW_PAD), 0) // N_HEADS)
        col = lax.broadcasted_iota(jnp.int32, (n_rows, NEW_PAD), 1)
        z = _dot_nt(qa_s[...], padded(kan_ref))
        strict = col < tok
        sp = _softplus(z)
        log_keep = jnp.where(strict, -sp, 0.0)
        hi, lo = _split(log_keep)
        tn = t2_ref[0:NEW_PAD, 0:NEW_PAD]
        between = _dot(hi, tn) + _dot(lo, tn)
        w = jnp.where(strict, jnp.exp((z - sp) + between), 0.0)
        run_s[...] = jnp.sum(log_keep, axis=1, keepdims=True)
        acca_s[...] = _dot(w.astype(BF16), padded(van_ref))
        s = _dot_nt(qb_s[0:n_rows, :], padded(kbn_ref)) * QK_SCALE
        s = s - slope * (tok - col).astype(F32)
        s = jnp.where(col <= tok, s, NEG_BIG)
        m_o = jnp.max(s, axis=1, keepdims=True)
        p = jnp.exp(s - m_o)
        mo_s[...] = m_o
        lo_s[...] = jnp.sum(p, axis=1, keepdims=True)
        acco_s[...] = _dot(p.astype(BF16), padded(vbn_ref))

    def pages(refs):
        return jnp.concatenate([r[0] for r in refs], axis=1)

    def ring_pages(ring, slot):
        return jnp.concatenate([ring[slot, p] for p in range(pps)], axis=1)

    def ring_copies(bb, jj, slot):
        first = pps * (n_steps - 1 - jj)
        copies = []
        for p in range(pps):
            pg = pt_ref[bb, first + p]
            copies.append(pltpu.make_async_copy(cka_hbm.at[pg], ka_ring.at[slot, p], ring_sem.at[0, slot]))
            copies.append(pltpu.make_async_copy(cva_hbm.at[pg], va_ring.at[slot, p], ring_sem.at[1, slot]))
        return copies

    slot = j & 1

    @pl.when((b == 0) & (j == 0))
    def _():
        for n, c in enumerate(ring_copies(0, 0, 0)):
            c.start(priority=n % 2)
        inflight_s[0] = 1

    last = j == n_steps - 1
    want_next = jnp.where(last, b + 1 < n_batch, live_s[0] == 1)

    @pl.when(want_next)
    def _():
        nxt = ring_copies(jnp.where(last, b + 1, b), jnp.where(last, 0, j + 1), 1 - slot)
        for n, c in enumerate(nxt):
            c.start(priority=n % 2)

    inflight_s[1 - slot] = want_next.astype(jnp.int32)

    @pl.when(inflight_s[slot] == 1)
    def _():
        for c in ring_copies(b, j, slot):
            c.wait()

    @pl.when(live_s[0] == 1)
    def _():
        z = _dot(qa_s[...], ring_pages(ka_ring, slot).astype(BF16))
        sp = _softplus(z)
        log_keep = -sp
        t2 = t2_ref[...]
        run = run_s[...]
        betweens = [None] * bps
        for i in reversed(range(bps)):
            lk = log_keep[:, i * blk:(i + 1) * blk]
            betweens[i] = _suffix_sums(lk, t2) + run
            run = run + jnp.sum(lk, axis=1, keepdims=True)
        run_s[...] = run
        live_s[0] = (jnp.max(run) > RUN_DEAD_LN).astype(jnp.int32)
        w = jnp.exp((z - sp) + jnp.concatenate(betweens, axis=1))
        acca_s[...] = acca_s[...] + _dot_nt(w.astype(BF16), ring_pages(va_ring, slot).astype(BF16))

    kb_h, kb_l = _split(pages(kb_refs))
    vb = pages(vb_refs).astype(BF16)
    raw = _dot(qb_s[...], kb_h)
    raw_hh = raw[0:n_rows]
    raw3 = raw_hh + (raw[n_rows:] + _dot(qb_s[0:n_rows, :], kb_l))
    tok = lax.broadcasted_iota(jnp.int32, (n_rows, blk), 0) // N_HEADS
    col = lax.broadcasted_iota(jnp.int32, (n_rows, blk), 1)
    for i in range(bps):
        n = n_first + i
        cols = slice(i * blk, (i + 1) * blk)
        gate_n = jnp.sum(raw3[:, cols], axis=1, keepdims=True) * (1.0 / blk)
        dist = ((n_cached - n) * blk + tok - col).astype(F32)
        s = raw_hh[:, cols] * QK_SCALE - slope * dist
        m_n = jnp.max(s, axis=1, keepdims=True)
        p = jnp.exp(s - m_n)
        l_n = jnp.sum(p, axis=1, keepdims=True)
        here = lane == n
        gate_s[...] = jnp.where(here, gate_n, gate_s[...])
        mx_s[...] = jnp.where(here, m_n, mx_s[...])
        l_s[...] = jnp.where(here, l_n, l_s[...])
        accb_s[n] = _dot_nt(p.astype(BF16), vb[:, cols])

    @pl.when(j == n_steps - 1)
    def _():
        def fold(acc):
            kept = jnp.where(head_mask, acc, 0.0)
            return jnp.sum(kept.reshape(n_tok, N_HEADS, GROUP_WIDTH), axis=1)

        oa_ref[0] = fold(acca_s[...])
        cnt, ln = _rank_before_lanes(gate_s[...], n_cached)
        sel = (ln < n_cached) & (cnt < MOBA_TOPK)
        mx = jnp.where(sel, mx_s[...], NEG_BIG)
        m_o = mo_s[...]
        m_all = jnp.maximum(m_o, jnp.max(mx, axis=1, keepdims=True))
        c = jnp.where(sel, jnp.exp(mx - m_all), 0.0)
        c_o = jnp.exp(m_o - m_all)
        l_tot = c_o * lo_s[...] + jnp.sum(c * l_s[...], axis=1, keepdims=True)
        acc = c_o * acco_s[...]
        for nb in range(n_cached):
            acc = acc + c[:, nb:nb + 1] * accb_s[nb]
        ob_ref[0] = fold(acc / l_tot)


def _sample_attention(page_table, slope_rows, qa, qb, ka_new, va_new, kb_new, vb_new,
                      ck_sb, cv_sb, ck_mb, cv_mb):
    dbs, n_tok, _ = qa.shape
    n_pages = page_table.shape[1]
    page = ck_sb.shape[2]
    pps = PAGES_PER_STEP
    assert page == NEW_PAD and n_pages % pps == 0 and (pps * page) % MOBA_BLOCK == 0
    n_steps = n_pages // pps
    assert n_steps % 2 == 0, "the two-slot page ring alternates slots across consecutive grid steps"
    n_cached = n_pages * page // MOBA_BLOCK
    assert n_cached <= LANES
    n_rows = n_tok * N_HEADS

    tok_spec = pl.BlockSpec((1, n_tok, GROUP_WIDTH), lambda b, j, pt: (b, 0, 0))

    def page_spec(which):
        return pl.BlockSpec((1, GROUP_WIDTH, page),
                            lambda b, j, pt: (pt[b, pps * (n_steps - 1 - j) + which], 0, 0))

    grid_spec = pltpu.PrefetchScalarGridSpec(
        num_scalar_prefetch=1,
        grid=(dbs, n_steps),
        in_specs=[pl.BlockSpec((n_rows, LANES), lambda b, j, pt: (0, 0))] + [tok_spec] * 6
                 + [pl.BlockSpec(memory_space=pl.ANY)] * 2 + [page_spec(p) for p in range(pps)] * 2,
        out_specs=[tok_spec, tok_spec],
        scratch_shapes=[
            pltpu.VMEM((2 * MOBA_BLOCK, MOBA_BLOCK), BF16),
            pltpu.VMEM((n_rows, GROUP_WIDTH), BF16),
            pltpu.VMEM((2 * n_rows, GROUP_WIDTH), BF16),
            pltpu.VMEM((n_rows, 1), F32),
            pltpu.VMEM((n_rows, GROUP_WIDTH), F32),
            pltpu.VMEM((n_rows, LANES), F32),
            pltpu.VMEM((n_rows, LANES), F32),
            pltpu.VMEM((n_rows, LANES), F32),
            pltpu.VMEM((n_cached, n_rows, GROUP_WIDTH), F32),
            pltpu.VMEM((n_rows, 1), F32),
            pltpu.VMEM((n_rows, 1), F32),
            pltpu.VMEM((n_rows, GROUP_WIDTH), F32),
            pltpu.VMEM((NEW_PAD, GROUP_WIDTH), F32),
            pltpu.SMEM((1,), jnp.int32),
            pltpu.VMEM((2, pps, GROUP_WIDTH, page), F32),
            pltpu.VMEM((2, pps, GROUP_WIDTH, page), F32),
            pltpu.SemaphoreType.DMA((2, 2)),
            pltpu.SMEM((2,), jnp.int32),
        ],
    )
    out = jax.ShapeDtypeStruct((dbs, n_tok, GROUP_WIDTH), F32)
    caches = [ck_sb, cv_sb] + [c for c in (ck_mb, cv_mb) for _ in range(pps)]
    return pl.pallas_call(
        _sample_kernel,
        grid_spec=grid_spec,
        out_shape=[out, out],
        compiler_params=pltpu.CompilerParams(dimension_semantics=("arbitrary", "arbitrary"),
                                             vmem_limit_bytes=VMEM_LIMIT),
        name="sample_attn",
    )(page_table, slope_rows, qa, qb, ka_new, va_new, kb_new, vb_new, *caches)


def _mix_out_kernel(alpha, x_ref, oa_ref, ga_ref, ob_ref, gb_ref, w_ref, gain_ref, bias_ref, y_ref):
    def gated(o_ref, g_ref):
        return o_ref[...].astype(F32) * jax.nn.silu(g_ref[...].astype(F32))

    h = jnp.concatenate([gated(oa_ref, ga_ref), gated(ob_ref, gb_ref)], axis=-1)
    out = _dot(h.astype(BF16), w_ref[...])
    y = alpha * x_ref[...] + out
    mu = jnp.mean(y, axis=-1, keepdims=True)
    var = jnp.mean(jnp.square(y - mu), axis=-1, keepdims=True)
    y_ref[...] = (y - mu) * lax.rsqrt(var + LN_EPS) * gain_ref[...] + bias_ref[...]


def _mix_out(x2d, oa, ga, ob, gb, w_out, gain, bias, alpha, block_rows):
    rows, d_model = x2d.shape
    g_spec = pl.BlockSpec((block_rows, GROUP_WIDTH), lambda i: (i, 0))
    x_spec = pl.BlockSpec((block_rows, d_model), lambda i: (i, 0))
    return pl.pallas_call(
        functools.partial(_mix_out_kernel, alpha),
        grid=(rows // block_rows,),
        in_specs=[x_spec, g_spec, g_spec, g_spec, g_spec,
                  _full_spec(w_out), _full_spec(gain), _full_spec(bias)],
        out_specs=x_spec,
        out_shape=jax.ShapeDtypeStruct(x2d.shape, F32),
        compiler_params=pltpu.CompilerParams(dimension_semantics=("arbitrary",),
                                             vmem_limit_bytes=VMEM_LIMIT),
        name="mix_out",
    )(x2d, oa, ga, ob, gb, w_out, gain, bias)


def kernel(x_prompt, x_sample, cache_k_sb, cache_v_sb, cache_k_moba, cache_v_moba, page_table,
           w_in, w_out, ln_gain, ln_bias):
    depth = w_in.shape[0]
    assert depth == 1, "single-layer trunk"
    bsz, seq, d_model = x_prompt.shape
    dbs, n_tok, _ = x_sample.shape
    alpha = (2.0 * depth) ** 0.25
    slopes = jnp.asarray([2.0 ** (-8.0 * (i + 1) / N_HEADS) for i in range(N_HEADS)], dtype=F32)
    slope_rows = jnp.broadcast_to(jnp.tile(slopes, n_tok)[:, None], (n_tok * N_HEADS, LANES))

    g = GROUP_WIDTH
    w = w_in[0]
    w_hi = w.astype(BF16)
    w_lo = (w[:, 4 * g:6 * g] - w_hi[:, 4 * g:6 * g].astype(F32)).astype(BF16)

    def col(a, c):
        return a[:, c * g:(c + 1) * g]

    w_row = jnp.concatenate([col(w_hi, c) for c in ROW_GROUPS], axis=1)
    w_row_lo = col(w_lo, 1)
    w_feat = jnp.concatenate([col(w_hi, c) for c in FEAT_GROUPS], axis=1).T
    w_feat_lo = col(w_lo, 0).T
    w_o = w_out[0].astype(BF16)
    gain = ln_gain[0][None, :]
    bias = ln_bias[0][None, :]

    def pages(c):
        return jnp.transpose(c[0], (0, 2, 3, 1)).reshape(c.shape[1], GROUP_WIDTH, c.shape[2])

    def heads_from_features(a):
        return jnp.transpose(a.reshape(bsz, N_HEADS, HEAD_DIM, seq), (0, 3, 1, 2))[None]

    def heads_from_rows(a):
        return a.reshape(1, dbs, n_tok, N_HEADS, HEAD_DIM)

    xp = x_prompt.reshape(bsz * seq, d_model)
    (ga, gb, ka_rows, kb_rows, kmean, qat, qbt, kat, vat, kbt, vbt) = _project_prompt(
        xp, w_row, w_row_lo, w_feat, w_feat_lo, bsz, 512)
    shp = (bsz, seq, GROUP_WIDTH)
    o_a = _sb_prompt(qat, ka_rows.reshape(shp), vat)
    o_b = _moba_prompt(slopes, qbt, kb_rows.reshape(shp), vbt, kmean.reshape(bsz, seq // MOBA_BLOCK, GROUP_WIDTH))
    y_p = _mix_out(xp, o_a.reshape(xp.shape[0], GROUP_WIDTH), ga, o_b.reshape(xp.shape[0], GROUP_WIDTH), gb,
                   w_o, gain, bias, alpha, 512)

    xs = x_sample.reshape(dbs * n_tok, d_model)
    sqa, ska, sva, sga, sqb, skb, svb, sgb = _project_rows(xs, w_hi, w_lo)
    sshp = (dbs, n_tok, GROUP_WIDTH)
    so_a, so_b = _sample_attention(
        page_table, slope_rows, sqa.reshape(sshp), sqb.reshape(sshp),
        ska.reshape(sshp), sva.reshape(sshp), skb.reshape(sshp), svb.reshape(sshp),
        pages(cache_k_sb), pages(cache_v_sb), pages(cache_k_moba), pages(cache_v_moba))
    y_s = _mix_out(xs, so_a.reshape(xs.shape[0], GROUP_WIDTH), sga, so_b.reshape(xs.shape[0], GROUP_WIDTH), sgb,
                   w_o, gain, bias, alpha, dbs * n_tok)

    return (y_p.reshape(x_prompt.shape), y_s.reshape(x_sample.shape),
            heads_from_features(kat), heads_from_features(vat),
            heads_from_features(kbt), heads_from_features(vbt),
            heads_from_rows(ska), heads_from_rows(sva), heads_from_rows(skb), heads_from_rows(svb))
```
